```python
import jax, jax.numpy as jnp
from jax import lax
import numpy as np

D_MODEL = 1024
BATCH = 32
SEQ = 256
DEPTH = 4
DEC_BATCH = 2
DEC_SEQ = 2048
PAST_LEN = 512

GRID_W = 64
ROPE_BASE = 10000.0
NORM_EPS = 1e-6
NEG_BIG = -1e30
F_MIN = 1e-6
N_ATTN_LAYERS = (DEPTH + 1) // 2
N_REC_LAYERS = DEPTH // 2
MLA_HEADS = 8
MLA_Q_LORA = 384
MLA_KV_LORA = 256
MLA_NOPE = 64
MLA_ROPE = 32
MLA_V = 64
MLA_SCALE = (MLA_NOPE + MLA_ROPE) ** -0.5
GQA_HEADS = 8
GQA_KV_HEADS = 2
GQA_GROUP = GQA_HEADS // GQA_KV_HEADS
GQA_HD = 64
GQA_SCALE = GQA_HD ** -0.5
WINDOW = 128
BAND_BLOCK = 128
Q_BLOCK = 128
ATTN_IN = MLA_Q_LORA + MLA_KV_LORA + MLA_ROPE + GQA_HEADS * GQA_HD + 2 * GQA_KV_HEADS * GQA_HD
ATTN_OUT = MLA_HEADS * MLA_V + GQA_HEADS * GQA_HD
HG_HEADS = 8
HG_DK = 128
HG_DV = 128
HG_CHUNK = 32
REC_IN = 3 * HG_HEADS * HG_DK + 2 * HG_HEADS * HG_DV
REC_OUT = HG_HEADS * HG_DV
N_EXPERTS = 16
EC_CAPACITY = 2
D_FF = 2048

kernel_name = "hybrid_diffusion_mla_swa_hgrn2_ec_step"


def _split(z, sizes):
    out, start = [], 0
    for s in sizes:
        out.append(z[..., start:start + s])
        start += s
    return out


def rms_norm(x, gain):
    xf = x.astype(jnp.float32)
    y = xf * lax.rsqrt(jnp.mean(xf * xf, axis=-1, keepdims=True) + NORM_EPS)
    return (y * gain.astype(jnp.float32)).astype(x.dtype)


def adaln(cvec, w, b):
    m = jax.nn.silu(cvec) @ w + b
    return jnp.split(m, 6, axis=-1)


def modulate(x, gain, shift, scale):
    return rms_norm(x, gain) * (1 + scale[:, None, :]) + shift[:, None, :]


def _rotate_half(x, ang):
    cos = jnp.cos(ang)[None, :, None, :].astype(x.dtype)
    sin = jnp.sin(ang)[None, :, None, :].astype(x.dtype)
    x1, x2 = jnp.split(x, 2, axis=-1)
    return jnp.concatenate([x1 * cos - x2 * sin, x2 * cos + x1 * sin], axis=-1)


def axial_rope(x):
    T, R = x.shape[1], x.shape[-1]
    half = R // 2
    t = jnp.arange(T)
    rows = (t // GRID_W).astype(jnp.float32)
    cols = (t % GRID_W).astype(jnp.float32)
    inv_freq = ROPE_BASE ** (-jnp.arange(0, half, 2, dtype=jnp.float32) / half)
    return jnp.concatenate([_rotate_half(x[..., :half], rows[:, None] * inv_freq),
                            _rotate_half(x[..., half:], cols[:, None] * inv_freq)], axis=-1)


def sink_softmax(s, sink):
    if sink is None:
        return jax.nn.softmax(s, axis=-1)
    col = jnp.broadcast_to(sink.astype(jnp.float32)[..., None, None], s.shape[:-1] + (1,))
    return jax.nn.softmax(jnp.concatenate([s, col], axis=-1), axis=-1)[..., :-1]


def blocked_attention(q, k, v, scale, sink):
    B, Tq, h, g, dk = q.shape
    nb = Tq // Q_BLOCK
    qb = jnp.moveaxis(q.reshape(B, nb, Q_BLOCK, h, g, dk), 1, 0)

    def one_block(qi):
        s = jnp.einsum("bqhgd,bkhd->bhgqk", qi, k).astype(jnp.float32) * scale
        p = sink_softmax(s, sink).astype(v.dtype)
        return jnp.einsum("bhgqk,bkhd->bqhgd", p, v)

    o = lax.map(one_block, qb)
    return jnp.moveaxis(o, 0, 1).reshape(B, Tq, h, g, v.shape[-1])


def banded_window_attention(q, k, v, k_ctx, v_ctx, sink, scale):
    B, T, h, g, d = q.shape
    L = k_ctx.shape[1]
    nb = T // BAND_BLOCK
    pad = [(0, 0), (BAND_BLOCK, BAND_BLOCK), (0, 0), (0, 0)]
    kb = jnp.pad(k, pad).reshape(B, nb + 2, BAND_BLOCK, h, d)
    vb = jnp.pad(v, pad).reshape(B, nb + 2, BAND_BLOCK, h, d)
    k_band = jnp.concatenate([kb[:, :-2], kb[:, 1:-1], kb[:, 2:]], axis=2)
    v_band = jnp.concatenate([vb[:, :-2], vb[:, 1:-1], vb[:, 2:]], axis=2)
    qb = q.reshape(B, nb, BAND_BLOCK, h, g, d)
    a = jnp.arange(BAND_BLOCK)[:, None] + BAND_BLOCK
    kk = jnp.arange(3 * BAND_BLOCK)[None, :]
    abs_k = (jnp.arange(nb)[:, None, None] - 1) * BAND_BLOCK + kk[None]
    valid = (jnp.abs(a - kk) <= WINDOW)[None] & (abs_k >= 0) & (abs_k < T)
    s_loc = jnp.einsum("bnqhgd,bnkhd->bnhgqk", qb, k_band).astype(jnp.float32) * scale
    s_loc = jnp.where(valid[None, :, None, None], s_loc, NEG_BIG)
    s_ctx = jnp.einsum("bnqhgd,bkhd->bnhgqk", qb, k_ctx).astype(jnp.float32) * scale
    p = sink_softmax(jnp.concatenate([s_loc, s_ctx], axis=-1), sink).astype(v.dtype)
    nk = 3 * BAND_BLOCK
    o = (jnp.einsum("bnhgqk,bnkhd->bnqhgd", p[..., :nk], v_band)
         + jnp.einsum("bnhgqk,bkhd->bnqhgd", p[..., nk:nk + L], v_ctx))
    return o.reshape(B, T, h, g, d)


def attn_project(h, w_in, q_norm, w_uq, kv_norm):
    B, T, _ = h.shape
    c_q, c_kv, k_rope, q_g, k_g, v_g = _split(
        h @ w_in, [MLA_Q_LORA, MLA_KV_LORA, MLA_ROPE, GQA_HEADS * GQA_HD,
                   GQA_KV_HEADS * GQA_HD, GQA_KV_HEADS * GQA_HD])
    q_m = (rms_norm(c_q, q_norm) @ w_uq).reshape(B, T, MLA_HEADS, MLA_NOPE + MLA_ROPE)
    ckv_n = rms_norm(c_kv, kv_norm)
    return (q_m, ckv_n, k_rope, q_g.reshape(B, T, GQA_HEADS, GQA_HD),
            k_g.reshape(B, T, GQA_KV_HEADS, GQA_HD), v_g.reshape(B, T, GQA_KV_HEADS, GQA_HD))


def mla_keys_values(ckv_n, k_rope, w_ukv):
    B, T, _ = ckv_n.shape
    kv = (ckv_n @ w_ukv).reshape(B, T, MLA_HEADS, MLA_NOPE + MLA_V)
    k = jnp.concatenate([kv[..., :MLA_NOPE],
                         jnp.broadcast_to(k_rope[:, :, None, :], (B, T, MLA_HEADS, MLA_ROPE))], axis=-1)
    return k, kv[..., MLA_NOPE:]


def merge_heads(o_m, o_g, w_out):
    B, T = o_m.shape[:2]
    return jnp.concatenate([o_m.reshape(B, T, -1), o_g.reshape(B, T, -1)], axis=-1) @ w_out


def attn_mixer_context(h, w_in, q_norm, w_uq, kv_norm, w_ukv, sink, w_out):
    B, L, _ = h.shape
    q_m, ckv_n, k_rope, q_g, k_g, v_g = attn_project(h, w_in, q_norm, w_uq, kv_norm)
    k_m, v_m = mla_keys_values(ckv_n, k_rope, w_ukv)
    o_m = blocked_attention(q_m[:, :, :, None], k_m, v_m, MLA_SCALE, None)
    o_g = blocked_attention(q_g.reshape(B, L, GQA_KV_HEADS, GQA_GROUP, GQA_HD), k_g, v_g, GQA_SCALE,
                            sink.reshape(GQA_KV_HEADS, GQA_GROUP))
    return merge_heads(o_m, o_g, w_out), ckv_n, k_rope, k_g, v_g


def attn_mixer_latent(h, ckv_ctx, krope_ctx, kg_ctx, vg_ctx, w_in, q_norm, w_uq, kv_norm, w_ukv, sink, w_out):
    B, T, _ = h.shape
    q_m, ckv_n, k_rope, q_g, k_g, v_g = attn_project(h, w_in, q_norm, w_uq, kv_norm)
    q_m = jnp.concatenate([q_m[..., :MLA_NOPE], axial_rope(q_m[..., MLA_NOPE:])], axis=-1)
    k_rope = axial_rope(k_rope[:, :, None, :])[:, :, 0, :]
    k_lat, v_lat = mla_keys_values(ckv_n, k_rope, w_ukv)
    k_ctx, v_ctx = mla_keys_values(ckv_ctx, krope_ctx, w_ukv)
    o_m = blocked_attention(q_m[:, :, :, None], jnp.concatenate([k_lat, k_ctx], axis=1),
                            jnp.concatenate([v_lat, v_ctx], axis=1), MLA_SCALE, None)
    q_g = axial_rope(q_g).reshape(B, T, GQA_KV_HEADS, GQA_GROUP, GQA_HD)
    o_g = banded_window_attention(q_g, axial_rope(k_g), v_g, kg_ctx, vg_ctx,
                                  sink.reshape(GQA_KV_HEADS, GQA_GROUP), GQA_SCALE)
    return merge_heads(o_m, o_g, w_out)


def hgrn_gate(z_f, lb):
    lbf = lb.astype(jnp.float32)
    f = lbf + (1.0 - lbf) * jax.nn.sigmoid(z_f.astype(jnp.float32))
    return jnp.log(jnp.maximum(f, F_MIN)), 1.0 - f


def hgrn_chunk_scan(q, k, log_f, v, s0):
    B, T, H, _ = q.shape
    n = T // HG_CHUNK
    to_chunks = lambda a: a.reshape(B, n, HG_CHUNK, H, a.shape[-1]).transpose(1, 0, 3, 2, 4)
    causal = jnp.tril(jnp.ones((HG_CHUNK, HG_CHUNK), dtype=bool))[:, :, None]

    def step(S, inp):
        qc, kc, gc, vc = inp
        b = jnp.cumsum(gc, axis=2)
        diff = b[:, :, :, None, :] - b[:, :, None, :, :]
        decay = jnp.exp(jnp.where(causal, diff, NEG_BIG))
        A = jnp.einsum("bhtd,bhsd,bhtsd->bhts", qc, kc, decay)
        o = jnp.einsum("bhts,bhsv->bhtv", A, vc) + jnp.einsum("bhtd,bhdv->bhtv", qc * jnp.exp(b), S)
        b_last = b[:, :, -1:, :]
        S_new = (jnp.exp(b_last[:, :, 0, :])[..., None] * S
                 + jnp.einsum("bhsd,bhsv->bhdv", kc * jnp.exp(b_last - b), vc))
        return S_new, o

    S_fin, o = lax.scan(step, s0.astype(jnp.float32),
                        (to_chunks(q), to_chunks(k), to_chunks(log_f), to_chunks(v)))
    return o.transpose(1, 0, 3, 2, 4).reshape(B, T, H, v.shape[-1]), S_fin


def rec_mixer(h, w_in, lb, out_gain, w_out, s0_fwd, s0_bwd):
    B, T, _ = h.shape
    hs = (B, T, HG_HEADS, HG_DK)
    vs = (B, T, HG_HEADS, HG_DV)
    zq, zf_fwd, zf_bwd, zi, zg = _split(h @ w_in, [HG_HEADS * HG_DK] * 3 + [HG_HEADS * HG_DV] * 2)
    q = jax.nn.silu(zq.astype(jnp.float32)).reshape(hs)
    v = zi.astype(jnp.float32).reshape(vs)
    lf_f, k_f = hgrn_gate(zf_fwd, lb[0])
    lf_b, k_b = hgrn_gate(zf_bwd, lb[1])
    rev = lambda a: jnp.flip(a, axis=1)
    o_f, s_f = hgrn_chunk_scan(q, k_f.reshape(hs), lf_f.reshape(hs), v, s0_fwd)
    o_b, s_b = hgrn_chunk_scan(rev(q), rev(k_b.reshape(hs)), rev(lf_b.reshape(hs)), rev(v), s0_bwd)
    o = rms_norm(o_f + rev(o_b), out_gain.reshape(HG_HEADS, HG_DV)) * jax.nn.silu(zg.astype(jnp.float32)).reshape(vs)
    y = o.astype(h.dtype).reshape(B, T, REC_OUT) @ w_out
    return y, jnp.stack([s_f, s_b], axis=1)


def ec_moe(h, w_router, w_gate, w_up, w_down):
    B, T, D = h.shape
    cap = EC_CAPACITY * T // N_EXPERTS
    aff = jax.nn.softmax(jnp.einsum("btd,de->bte", h, w_router).astype(jnp.float32), axis=-1)
    gate, idx = lax.top_k(jnp.swapaxes(aff, 1, 2), cap)
    xs = jax.vmap(lambda hb, ib: hb[ib])(h, idx)
    hid = jax.nn.silu(jnp.einsum("becd,edf->becf", xs, w_gate)) * jnp.einsum("becd,edf->becf", xs, w_up)
    ys = jnp.einsum("becf,efd->becd", hid, w_down) * gate[..., None].astype(h.dtype)
    scatter = lambda ib, yb: jnp.zeros((T, D), yb.dtype).at[ib.reshape(-1)].add(yb.reshape(-1, D))
    return jax.vmap(scatter)(idx, ys)


def setup_inputs(seed: int = 0) -> dict:
    key = jax.random.key(seed)
    ks = jax.random.split(key, 32)
    f32 = jnp.float32
    nrm = lambda k, shape, scale=1.0: jax.random.normal(k, shape, f32) * scale
    gain = lambda k, shape: 1.0 + 0.01 * jax.random.normal(k, shape, f32)
    return {
        "x_prompt": nrm(ks[0], (BATCH, SEQ, D_MODEL)),
        "x_sample": nrm(ks[1], (DEC_BATCH, DEC_SEQ, D_MODEL)),
        "c": nrm(ks[2], (DEC_BATCH, D_MODEL)),
        "cache_mla_ckv": nrm(ks[3], (DEC_BATCH, N_ATTN_LAYERS, PAST_LEN, MLA_KV_LORA)),
        "cache_mla_krope": nrm(ks[4], (DEC_BATCH, N_ATTN_LAYERS, PAST_LEN, MLA_ROPE)),
        "cache_gqa_k": nrm(ks[5], (DEC_BATCH, N_ATTN_LAYERS, PAST_LEN, GQA_KV_HEADS, GQA_HD)),
        "cache_gqa_v": nrm(ks[6], (DEC_BATCH, N_ATTN_LAYERS, PAST_LEN, GQA_KV_HEADS, GQA_HD)),
        "state_hgrn": nrm(ks[7], (DEC_BATCH, N_REC_LAYERS, 2, HG_HEADS, HG_DK, HG_DV), 0.5),
        "c_ctx": nrm(ks[8], (D_MODEL,)),
        "norm1": gain(ks[9], (DEPTH, D_MODEL)),
        "norm2": gain(ks[10], (DEPTH, D_MODEL)),
        "w_mod": nrm(ks[11], (DEPTH, D_MODEL, 6 * D_MODEL), 0.5 * D_MODEL ** -0.5),
        "b_mod": nrm(ks[12], (DEPTH, 6 * D_MODEL), 0.01),
        "w_attn_in": nrm(ks[13], (N_ATTN_LAYERS, D_MODEL, ATTN_IN), D_MODEL ** -0.5),
        "mla_q_norm": gain(ks[14], (N_ATTN_LAYERS, MLA_Q_LORA)),
        "w_mla_uq": nrm(ks[15], (N_ATTN_LAYERS, MLA_Q_LORA, MLA_HEADS * (MLA_NOPE + MLA_ROPE)), MLA_Q_LORA ** -0.5),
        "mla_kv_norm": gain(ks[16], (N_ATTN_LAYERS, MLA_KV_LORA)),
        "w_mla_ukv": nrm(ks[17], (N_ATTN_LAYERS, MLA_KV_LORA, MLA_HEADS * (MLA_NOPE + MLA_V)), MLA_KV_LORA ** -0.5),
        "gqa_sink": nrm(ks[18], (N_ATTN_LAYERS, GQA_HEADS), 0.5),
        "w_attn_out": nrm(ks[19], (N_ATTN_LAYERS, ATTN_OUT, D_MODEL), ATTN_OUT ** -0.5),
        "w_rec_in": nrm(ks[20], (N_REC_LAYERS, D_MODEL, REC_IN), D_MODEL ** -0.5),
        "rec_lb_logits": nrm(ks[21], (N_REC_LAYERS, 2, HG_HEADS * HG_DK), 0.5),
        "rec_out_norm": gain(ks[22], (N_REC_LAYERS, HG_HEADS * HG_DV)),
        "w_rec_out": nrm(ks[23], (N_REC_LAYERS, REC_OUT, D_MODEL), REC_OUT ** -0.5),
        "w_router": nrm(ks[24], (DEPTH, D_MODEL, N_EXPERTS), D_MODEL ** -0.5),
        "w_gate": nrm(ks[25], (DEPTH, N_EXPERTS, D_MODEL, D_FF), D_MODEL ** -0.5),
        "w_up": nrm(ks[26], (DEPTH, N_EXPERTS, D_MODEL, D_FF), D_MODEL ** -0.5),
        "w_down": nrm(ks[27], (DEPTH, N_EXPERTS, D_FF, D_MODEL), D_FF ** -0.5),
        "final_norm": gain(ks[28], (D_MODEL,)),
    }


def reference(x_prompt, x_sample, c, cache_mla_ckv, cache_mla_krope, cache_gqa_k, cache_gqa_v, state_hgrn,
              c_ctx, norm1, norm2, w_mod, b_mod, w_attn_in, mla_q_norm, w_mla_uq, mla_kv_norm, w_mla_ukv,
              gqa_sink, w_attn_out, w_rec_in, rec_lb_logits, rec_out_norm, w_rec_out, w_router, w_gate,
              w_up, w_down, final_norm):
    sm = jax.nn.softmax(rec_lb_logits.astype(jnp.float32), axis=0)
    lower = jnp.cumsum(sm, axis=0) - sm[0:1]
    B = x_prompt.shape[0]
    xc, xl = x_prompt, x_sample
    ckv_list, krope_list, kg_list, vg_list, st_list = [], [], [], [], []
    for layer in range(DEPTH):
        sc1, scl1, g1, sc2, scl2, g2 = adaln(c_ctx[None, :], w_mod[layer], b_mod[layer])
        sl1, sll1, gl1, sl2, sll2, gl2 = adaln(c, w_mod[layer], b_mod[layer])
        hc = modulate(xc, norm1[layer], sc1, scl1)
        hl = modulate(xl, norm1[layer], sl1, sll1)
        if layer % 2 == 0:
            a = layer // 2
            p = (w_attn_in[a], mla_q_norm[a], w_mla_uq[a], mla_kv_norm[a], w_mla_ukv[a], gqa_sink[a], w_attn_out[a])
            oc, ckv_n, k_rope, k_g, v_g = attn_mixer_context(hc, *p)
            ol = attn_mixer_latent(hl, cache_mla_ckv[:, a], cache_mla_krope[:, a],
                                   cache_gqa_k[:, a], cache_gqa_v[:, a], *p)
            ckv_list.append(ckv_n)
            krope_list.append(k_rope)
            kg_list.append(k_g)
            vg_list.append(v_g)
        else:
            r = layer // 2
            p = (w_rec_in[r], lower[r], rec_out_norm[r], w_rec_out[r])
            zero = jnp.zeros((B, HG_HEADS, HG_DK, HG_DV), jnp.float32)
            oc, st = rec_mixer(hc, *p, zero, zero)
            ol, _ = rec_mixer(hl, *p, state_hgrn[:, r, 0], state_hgrn[:, r, 1])
            st_list.append(st)
        xc = xc + g1[:, None, :] * oc
        xl = xl + gl1[:, None, :] * ol
        hc = modulate(xc, norm2[layer], sc2, scl2)
        hl = modulate(xl, norm2[layer], sl2, sll2)
        xc = xc + g2[:, None, :] * ec_moe(hc, w_router[layer], w_gate[layer], w_up[layer], w_down[layer])
        xl = xl + gl2[:, None, :] * ec_moe(hl, w_router[layer], w_gate[layer], w_up[layer], w_down[layer])
    y_prompt = rms_norm(xc, final_norm)
    y_sample = rms_norm(xl, final_norm)
    new_mla_ckv = jnp.stack(ckv_list, axis=1)
    new_mla_krope = jnp.stack(krope_list, axis=1)
    new_gqa_k = jnp.stack(kg_list, axis=1)
    new_gqa_v = jnp.stack(vg_list, axis=1)
    new_hgrn_state = jnp.stack(st_list, axis=1)
    return (y_prompt, y_sample, new_mla_ckv, new_mla_krope, new_gqa_k, new_gqa_v, new_hgrn_state)
```

```python
import functools

import jax
import jax.numpy as jnp
from jax import lax
from jax.experimental import pallas as pl
from jax.experimental.pallas import tpu as pltpu

F32 = jnp.float32
BF16 = jnp.bfloat16

D_MODEL = 1024
BATCH = 32
SEQ = 256
DEPTH = 4
DEC_BATCH = 2
DEC_SEQ = 2048
PAST_LEN = 512
GRID_W = 64
ROPE_BASE = 10000.0
NORM_EPS = 1e-6
NEG_BIG = -1e30
F_MIN = 1e-6
N_ATTN_LAYERS = (DEPTH + 1) // 2
N_REC_LAYERS = DEPTH // 2
MLA_HEADS = 8
MLA_Q_LORA = 384
MLA_KV_LORA = 256
MLA_NOPE = 64
MLA_ROPE = 32
MLA_V = 64
MLA_SCALE = (MLA_NOPE + MLA_ROPE) ** -0.5
GQA_HEADS = 8
GQA_KV_HEADS = 2
GQA_GROUP = GQA_HEADS // GQA_KV_HEADS
GQA_HD = 64
GQA_SCALE = GQA_HD ** -0.5
WINDOW = 128
BAND_BLOCK = 128
HG_HEADS = 8
HG_DK = 128
HG_DV = 128
REC_IN = 3 * HG_HEADS * HG_DK + 2 * HG_HEADS * HG_DV
N_EXPERTS = 16
EC_CAPACITY = 2
D_FF = 2048

LANES = 128
N_MOD_ROWS = 8
N_MOD_GROUPS = 1 + DEC_BATCH

ZC_Q = 0
ZC_KV = ZC_Q + MLA_Q_LORA
ZC_QG = ZC_KV + MLA_KV_LORA
ZC_KG = ZC_QG + GQA_HEADS * GQA_HD
ZC_VG = ZC_KG + GQA_KV_HEADS * GQA_HD
ZC_KR = ZC_VG + GQA_KV_HEADS * GQA_HD
ATTN_Z = ZC_KR + LANES
QN_W = MLA_HEADS * MLA_NOPE
QR_W = MLA_HEADS * MLA_ROPE
VM_W = MLA_HEADS * MLA_V
O_M_W = MLA_HEADS * MLA_V
O_G_W = GQA_HEADS * GQA_HD

SUBLANES = 8
SCAN_CHUNK = 128
SCAN_LEVELS = (64, 32, 16, 8)
SCAN_DIAG = 8
DIAG_CLAMP = 88.0
SELECT_BLOCK = 2048


def _dot(a, b):
    return jnp.dot(a, b, preferred_element_type=F32)


def _dot_nt(a, b):
    return lax.dot_general(a, b, (((1,), (1,)), ((), ())), preferred_element_type=F32)


def _dot_tn(a, b):
    return lax.dot_general(a, b, (((0,), (0,)), ((), ())), preferred_element_type=F32)


def _sigmoid(x):
    return 1.0 / (1.0 + jnp.exp(-x))


def _rms(x, gain):
    return x * lax.rsqrt(jnp.mean(x * x, axis=-1, keepdims=True) + NORM_EPS) * gain


def _mod_kernel(c_ref, w_ref, b_ref, o_ref):
    c = c_ref[...]
    s = (c * _sigmoid(c)).astype(BF16)
    o_ref[...] = _dot(s, w_ref[...].astype(BF16)) + b_ref[...]


def _mod_vectors(cvecs, w_mod, b_mod):
    bn = 1536
    nd = 6 * D_MODEL
    return pl.pallas_call(
        _mod_kernel,
        grid=(DEPTH, nd // bn),
        in_specs=[pl.BlockSpec((N_MOD_ROWS, D_MODEL), lambda l, j: (0, 0)),
                  pl.BlockSpec((None, D_MODEL, bn), lambda l, j: (l, 0, j)),
                  pl.BlockSpec((None, 1, bn), lambda l, j: (l, 0, j))],
        out_specs=pl.BlockSpec((None, N_MOD_ROWS, bn), lambda l, j: (l, 0, j)),
        out_shape=jax.ShapeDtypeStruct((DEPTH, N_MOD_ROWS, nd), F32),
        name="mod_vectors",
    )(cvecs, w_mod, b_mod.reshape(DEPTH, 1, nd))


def _group_map(rows_per_group, bm, base):
    if rows_per_group is None:
        return lambda i: base
    return lambda i: base + (i * bm) // rows_per_group


def _modproj_kernel(x_ref, g_ref, m_ref, w_ref, o_ref, *, shift_row, chunk):
    h = _rms(x_ref[...], g_ref[...])
    h = (h * (1.0 + m_ref[shift_row + 1:shift_row + 2, :]) + m_ref[shift_row:shift_row + 1, :]).astype(BF16)
    nout = o_ref.shape[1]
    for j in range(0, nout, chunk):
        o_ref[:, j:j + chunk] = _dot(h, w_ref[:, j:j + chunk])


def _modproj(x, gain, mods, w, *, sub, bm, grp):
    n, d = x.shape
    nout = w.shape[1]
    chunk = nout if nout <= 1536 else 1024
    kern = functools.partial(_modproj_kernel, shift_row=3 * sub, chunk=chunk)
    return pl.pallas_call(
        kern,
        grid=(n // bm,),
        in_specs=[pl.BlockSpec((bm, d), lambda i: (i, 0)),
                  pl.BlockSpec((1, d), lambda i: (0, 0)),
                  pl.BlockSpec((None, 6, d), lambda i: (grp(i), 0, 0)),
                  pl.BlockSpec((d, nout), lambda i: (0, 0))],
        out_specs=pl.BlockSpec((bm, nout), lambda i: (i, 0)),
        out_shape=jax.ShapeDtypeStruct((n, nout), F32),
        name="modproj",
    )(x, gain, mods, w)


def _outproj_kernel(a_ref, w_ref, x_ref, m_ref, o_ref, *, gate_row):
    y = _dot(a_ref[...], w_ref[...])
    o_ref[...] = x_ref[...] + m_ref[gate_row:gate_row + 1, :] * y


def _outproj(a, w, x, mods, *, sub, bm, grp):
    n, d = x.shape
    k = a.shape[1]
    kern = functools.partial(_outproj_kernel, gate_row=3 * sub + 2)
    return pl.pallas_call(
        kern,
        grid=(n // bm,),
        in_specs=[pl.BlockSpec((bm, k), lambda i: (i, 0)),
                  pl.BlockSpec((k, d), lambda i: (0, 0)),
                  pl.BlockSpec((bm, d), lambda i: (i, 0)),
                  pl.BlockSpec((None, 6, d), lambda i: (grp(i), 0, 0))],
        out_specs=pl.BlockSpec((bm, d), lambda i: (i, 0)),
        out_shape=jax.ShapeDtypeStruct((n, d), F32),
        name="outproj",
    )(a, w, x, mods)


def _softmax_pv(s, v, sink=None):
    m = jnp.max(s, axis=-1, keepdims=True)
    if sink is not None:
        m = jnp.maximum(m, sink)
    p = jnp.exp(s - m)
    l = jnp.sum(p, axis=-1, keepdims=True)
    if sink is not None:
        l = l + jnp.exp(sink - m)
    return _dot(p.astype(BF16), v) / l


def _swap_pairs(x, q):
    w = x.shape[1]
    lane = lax.broadcasted_iota(jnp.int32, x.shape, 1)
    first = (lane % (2 * q)) < q
    return jnp.where(first, pltpu.roll(x, w - q, axis=1), pltpu.roll(x, q, axis=1))


def _rope(x, cos, sin, q):
    return x * cos + _swap_pairs(x, q) * sin


def _attn_project(z, qn_ref, wuq_ref, kvn_ref, wukv_ref):
    cq = _rms(z[:, ZC_Q:ZC_Q + MLA_Q_LORA], qn_ref[...]).astype(BF16)
    qm = _dot(cq, wuq_ref[...])
    ckv_n = _rms(z[:, ZC_KV:ZC_KV + MLA_KV_LORA], kvn_ref[...])
    kv = _dot(ckv_n.astype(BF16), wukv_ref[...])
    return qm, ckv_n, kv


def _attn_ctx_kernel(sink_ref, z_ref, qn_ref, wuq_ref, kvn_ref, wukv_ref,
                     o_ref, ckv_ref, kr_ref, kg_ref, vg_ref):
    z = z_ref[...]
    qm, ckv_n, kv = _attn_project(z, qn_ref, wuq_ref, kvn_ref, wukv_ref)
    kr = z[:, ZC_KR:ZC_KR + MLA_ROPE]
    kg = z[:, ZC_KG:ZC_KG + GQA_KV_HEADS * GQA_HD]
    vg = z[:, ZC_VG:ZC_VG + GQA_KV_HEADS * GQA_HD]
    ckv_ref[...] = ckv_n
    kr_ref[...] = kr
    kg_ref[...] = kg
    vg_ref[...] = vg
    qn = (qm[:, :QN_W] * MLA_SCALE).astype(BF16)
    qr = (qm[:, QN_W:] * MLA_SCALE).astype(BF16)
    kn = kv[:, :QN_W].astype(BF16)
    vm = kv[:, QN_W:].astype(BF16)
    krb = kr.astype(BF16)
    for h in range(MLA_HEADS):
        s = (_dot_nt(qn[:, h * MLA_NOPE:(h + 1) * MLA_NOPE], kn[:, h * MLA_NOPE:(h + 1) * MLA_NOPE])
             + _dot_nt(qr[:, h * MLA_ROPE:(h + 1) * MLA_ROPE], krb))
        o = _softmax_pv(s, vm[:, h * MLA_V:(h + 1) * MLA_V])
        o_ref[:, h * MLA_V:(h + 1) * MLA_V] = o.astype(o_ref.dtype)
    qg = (z[:, ZC_QG:ZC_QG + O_G_W] * GQA_SCALE).astype(BF16)
    kgb = kg.astype(BF16)
    vgb = vg.astype(BF16)
    for h in range(GQA_HEADS):
        g = h // GQA_GROUP
        s = _dot_nt(qg[:, h * GQA_HD:(h + 1) * GQA_HD], kgb[:, g * GQA_HD:(g + 1) * GQA_HD])
        o = _softmax_pv(s, vgb[:, g * GQA_HD:(g + 1) * GQA_HD], sink_ref[h])
        o_ref[:, O_M_W + h * GQA_HD:O_M_W + (h + 1) * GQA_HD] = o.astype(o_ref.dtype)


def _attn_ctx(z, sink, q_norm, w_uq, kv_norm, w_ukv):
    nb = z.shape[0] // SEQ
    kvw = GQA_KV_HEADS * GQA_HD
    const = lambda b: (0, 0)
    row = lambda b: (b, 0)
    return pl.pallas_call(
        _attn_ctx_kernel,
        grid=(nb,),
        in_specs=[pl.BlockSpec(memory_space=pltpu.SMEM),
                  pl.BlockSpec((SEQ, ATTN_Z), row),
                  pl.BlockSpec((1, MLA_Q_LORA), const),
                  pl.BlockSpec(w_uq.shape, const),
                  pl.BlockSpec((1, MLA_KV_LORA), const),
                  pl.BlockSpec(w_ukv.shape, const)],
        out_specs=[pl.BlockSpec((SEQ, O_M_W + O_G_W), row),
                   pl.BlockSpec((SEQ, MLA_KV_LORA), row),
                   pl.BlockSpec((SEQ, MLA_ROPE), row),
                   pl.BlockSpec((SEQ, kvw), row),
                   pl.BlockSpec((SEQ, kvw), row)],
        out_shape=[jax.ShapeDtypeStruct((nb * SEQ, O_M_W + O_G_W), BF16),
                   jax.ShapeDtypeStruct((nb * SEQ, MLA_KV_LORA), F32),
                   jax.ShapeDtypeStruct((nb * SEQ, MLA_ROPE), F32),
                   jax.ShapeDtypeStruct((nb * SEQ, kvw), F32),
                   jax.ShapeDtypeStruct((nb * SEQ, kvw), F32)],
        name="attn_ctx",
    )(sink, z, q_norm, w_uq, kv_norm, w_ukv)


def _lat_prep_kernel(z_ref, qn_ref, wuq_ref, kvn_ref, wukv_ref, cm_ref, sm_ref, cg_ref, sg_ref,
                     qn_o, qr_o, kn_o, vm_o, kr_o, qg_o, kg_o, vg_o):
    z = z_ref[...]
    qm, ckv_n, kv = _attn_project(z, qn_ref, wuq_ref, kvn_ref, wukv_ref)
    cm = cm_ref[...]
    sm = sm_ref[...]
    cg = cg_ref[...]
    sg = sg_ref[...]
    qn_o[...] = (qm[:, :QN_W] * MLA_SCALE).astype(BF16)
    qr_o[...] = (_rope(qm[:, QN_W:], cm, sm, MLA_ROPE // 4) * MLA_SCALE).astype(BF16)
    kn_o[...] = kv[:, :QN_W].astype(BF16)
    vm_o[...] = kv[:, QN_W:].astype(BF16)
    kr_o[...] = _rope(z[:, ZC_KR:ZC_KR + LANES], cm[:, :LANES], sm[:, :LANES], MLA_ROPE // 4).astype(BF16)
    qg_o[...] = (_rope(z[:, ZC_QG:ZC_QG + O_G_W], cg, sg, GQA_HD // 4) * GQA_SCALE).astype(BF16)
    kvw = GQA_KV_HEADS * GQA_HD
    kg_o[...] = _rope(z[:, ZC_KG:ZC_KG + kvw], cg[:, :kvw], sg[:, :kvw], GQA_HD // 4).astype(BF16)
    vg_o[...] = z[:, ZC_VG:ZC_VG + kvw].astype(BF16)


def _lat_prep(z, q_norm, w_uq, kv_norm, w_ukv, cm, sm, cg, sg, *, bm):
    n = z.shape[0]
    per = DEC_SEQ // bm
    kvw = GQA_KV_HEADS * GQA_HD
    const = lambda i: (0, 0)
    row = lambda i: (i, 0)
    pos = lambda i: (i % per, 0)
    widths = (QN_W, QR_W, QN_W, VM_W, LANES, O_G_W, kvw, kvw)
    return pl.pallas_call(
        _lat_prep_kernel,
        grid=(n // bm,),
        in_specs=[pl.BlockSpec((bm, ATTN_Z), row),
                  pl.BlockSpec((1, MLA_Q_LORA), const),
                  pl.BlockSpec(w_uq.shape, const),
                  pl.BlockSpec((1, MLA_KV_LORA), const),
                  pl.BlockSpec(w_ukv.shape, const),
                  pl.BlockSpec((bm, QR_W), pos),
                  pl.BlockSpec((bm, QR_W), pos),
                  pl.BlockSpec((bm, O_G_W), pos),
                  pl.BlockSpec((bm, O_G_W), pos)],
        out_specs=[pl.BlockSpec((bm, w), row) for w in widths],
        out_shape=[jax.ShapeDtypeStruct((n, w), BF16) for w in widths],
        name="lat_prep",
    )(z, q_norm, w_uq, kv_norm, w_ukv, cm, sm, cg, sg)


def _cache_kv_kernel(c_ref, w_ref, kn_o, vm_o):
    kv = _dot(c_ref[...].astype(BF16), w_ref[...])
    kn_o[...] = kv[:, :QN_W].astype(BF16)
    vm_o[...] = kv[:, QN_W:].astype(BF16)


def _cache_kv(cache_ckv, a, w_ukv):
    nb = cache_ckv.shape[0]
    return pl.pallas_call(
        _cache_kv_kernel,
        grid=(nb,),
        in_specs=[pl.BlockSpec((None, None, PAST_LEN, MLA_KV_LORA), lambda b: (b, a, 0, 0)),
                  pl.BlockSpec(w_ukv.shape, lambda b: (0, 0))],
        out_specs=[pl.BlockSpec((None, PAST_LEN, QN_W), lambda b: (b, 0, 0)),
                   pl.BlockSpec((None, PAST_LEN, VM_W), lambda b: (b, 0, 0))],
        out_shape=[jax.ShapeDtypeStruct((nb, PAST_LEN, QN_W), BF16),
                   jax.ShapeDtypeStruct((nb, PAST_LEN, VM_W), BF16)],
        name="cache_kv",
    )(cache_ckv, w_ukv)


def _lat_attn_kernel(sink_ref, qn_ref, qr_ref, kn_ref, kr_ref, vm_ref, qg_ref,
                     kp_ref, kc_ref, kx_ref, vp_ref, vc_ref, vx_ref, kctx_ref, vctx_ref, o_ref):
    n = pl.program_id(1)
    qn = qn_ref[...]
    qr = qr_ref[...]
    kr = kr_ref[:, :MLA_ROPE]
    for h in range(MLA_HEADS):
        s = (_dot_nt(qn[:, h * MLA_NOPE:(h + 1) * MLA_NOPE], kn_ref[:, h * MLA_NOPE:(h + 1) * MLA_NOPE])
             + _dot_nt(qr[:, h * MLA_ROPE:(h + 1) * MLA_ROPE], kr))
        o = _softmax_pv(s, vm_ref[:, h * MLA_V:(h + 1) * MLA_V])
        o_ref[:, h * MLA_V:(h + 1) * MLA_V] = o.astype(o_ref.dtype)
    bb = BAND_BLOCK
    kb = jnp.concatenate([kp_ref[...], kc_ref[...], kx_ref[...]], axis=0)
    vb = jnp.concatenate([vp_ref[...], vc_ref[...], vx_ref[...]], axis=0)
    t = lax.broadcasted_iota(jnp.int32, (bb, 3 * bb), 0)
    kk = lax.broadcasted_iota(jnp.int32, (bb, 3 * bb), 1)
    abs_k = (n - 1) * bb + kk
    valid = (jnp.abs(t + bb - kk) <= WINDOW) & (abs_k >= 0) & (abs_k < DEC_SEQ)
    qg = qg_ref[...]
    kctx = kctx_ref[...]
    vctx = vctx_ref[...]
    for h in range(GQA_HEADS):
        g = h // GQA_GROUP
        hs = slice(g * GQA_HD, (g + 1) * GQA_HD)
        q = qg[:, h * GQA_HD:(h + 1) * GQA_HD]
        s_loc = jnp.where(valid, _dot_nt(q, kb[:, hs]), NEG_BIG)
        s_ctx = _dot_nt(q, kctx[:, hs])
        sink = sink_ref[h]
        m = jnp.maximum(jnp.maximum(jnp.max(s_loc, axis=-1, keepdims=True),
                                    jnp.max(s_ctx, axis=-1, keepdims=True)), sink)
        p_loc = jnp.exp(s_loc - m)
        p_ctx = jnp.exp(s_ctx - m)
        l = (jnp.sum(p_loc, axis=-1, keepdims=True) + jnp.sum(p_ctx, axis=-1, keepdims=True)
             + jnp.exp(sink - m))
        o = (_dot(p_loc.astype(BF16), vb[:, hs]) + _dot(p_ctx.astype(BF16), vctx[:, hs])) / l
        o_ref[:, O_M_W + h * GQA_HD:O_M_W + (h + 1) * GQA_HD] = o.astype(o_ref.dtype)


def _lat_attn(sink, qn, qr, kn_all, kr_all, vm_all, qg, kg, vg, kg_ctx, vg_ctx):
    nb = DEC_BATCH
    bb = BAND_BLOCK
    nq = DEC_SEQ // bb
    tk = kn_all.shape[1]
    kvw = GQA_KV_HEADS * GQA_HD
    qrow = lambda b, n: (b * nq + n, 0)
    prev = lambda b, n: (b * nq + jnp.maximum(n - 1, 0), 0)
    nxt = lambda b, n: (b * nq + jnp.minimum(n + 1, nq - 1), 0)
    per_b = lambda b, n: (b, 0, 0)
    return pl.pallas_call(
        _lat_attn_kernel,
        grid=(nb, nq),
        in_specs=[pl.BlockSpec(memory_space=pltpu.SMEM),
                  pl.BlockSpec((bb, QN_W), qrow),
                  pl.BlockSpec((bb, QR_W), qrow),
                  pl.BlockSpec((None, tk, QN_W), per_b),
                  pl.BlockSpec((None, tk, LANES), per_b),
                  pl.BlockSpec((None, tk, VM_W), per_b),
                  pl.BlockSpec((bb, O_G_W), qrow),
                  pl.BlockSpec((bb, kvw), prev),
                  pl.BlockSpec((bb, kvw), qrow),
                  pl.BlockSpec((bb, kvw), nxt),
                  pl.BlockSpec((bb, kvw), prev),
                  pl.BlockSpec((bb, kvw), qrow),
                  pl.BlockSpec((bb, kvw), nxt),
                  pl.BlockSpec((None, PAST_LEN, kvw), per_b),
                  pl.BlockSpec((None, PAST_LEN, kvw), per_b)],
        out_specs=pl.BlockSpec((bb, O_M_W + O_G_W), qrow),
        out_shape=jax.ShapeDtypeStruct((nb * DEC_SEQ, O_M_W + O_G_W), BF16),
        name="lat_attn",
    )(sink, qn, qr, kn_all, kr_all, vm_all, qg, kg, kg, kg, vg, vg, vg, kg_ctx, vg_ctx)


def _row_tiles(x):
    return [x[i * SUBLANES:(i + 1) * SUBLANES, :] for i in range(x.shape[0] // SUBLANES)]


def _scan_masks(mask_ref):
    c = SCAN_CHUNK
    t = lax.broadcasted_iota(jnp.int32, (c, c), 0)
    s = lax.broadcasted_iota(jnp.int32, (c, c), 1)
    for li, m in enumerate(SCAN_LEVELS):
        sh = m.bit_length() - 1
        mask_ref[li] = jnp.where((t >> sh) == (s >> sh), 1.0, 0.0)
    sh = SCAN_DIAG.bit_length() - 1
    same = (t >> sh) == (s >> sh)
    mask_ref[len(SCAN_LEVELS)] = jnp.where(same & (s <= t), 1.0, 0.0)
    mask_ref[len(SCAN_LEVELS) + 1] = jnp.where(same & (s >= t), 1.0, 0.0)


def _level_rows(m, rev):
    mt = m // SUBLANES
    nt = SCAN_CHUNK // SUBLANES
    groups = []
    for g in range(nt // (2 * mt)):
        first = list(range(g * 2 * mt, g * 2 * mt + mt))
        second = list(range(g * 2 * mt + mt, (g + 1) * 2 * mt))
        if rev:
            groups.append((first, second, second[0], 0))
        else:
            groups.append((second, first, first[-1], SUBLANES - 1))
    return groups


def _scan_chunks(chains, mask_ref):
    c = SCAN_CHUNK
    half = c // 2
    nt = c // SUBLANES
    n_lv = len(SCAN_LEVELS)
    work = []
    for q, zf, v, lb, st, tri, rev in chains:
        f = lb + (1.0 - lb) * _sigmoid(zf)
        lf = jnp.log2(jnp.maximum(f, F_MIN))
        hi = lf.astype(BF16)
        mid = (lf - hi.astype(F32)).astype(BF16)
        work.append(dict(q=q, k=1.0 - f, v=v, vb=v.astype(BF16), st=st, tri=tri, rev=rev, hi=hi, mid=mid))
    for w in work:
        w["b"] = _dot(w["tri"], w["hi"]) + _dot(w["tri"], w["mid"])
        w["qt"], w["kt"], w["bt"], w["vt"] = (_row_tiles(w[n]) for n in ("q", "k", "b", "v"))
    for w in work:
        mid_row = SCAN_DIAG // 2 if w["rev"] else SCAN_DIAG // 2 - 1
        e = jnp.concatenate([jnp.clip(t - t[mid_row:mid_row + 1, :], -DIAG_CLAMP, DIAG_CLAMP) for t in w["bt"]],
                            axis=0)
        a = _dot_nt((w["q"] * jnp.exp2(e)).astype(BF16), (w["k"] * jnp.exp2(-e)).astype(BF16))
        w["a"] = jnp.where(mask_ref[n_lv + (1 if w["rev"] else 0)] > 0.5, a, 0.0).astype(BF16)
    for w in work:
        o = _dot(w["a"], w["vb"]) + _dot_nt((w["q"] * jnp.exp2(w["b"])).astype(BF16), w["st"].astype(BF16))
        w["o"] = _row_tiles(o)
    for li, m in enumerate(SCAN_LEVELS):
        for w in work:
            q_parts, k_parts, w["q_idx"], w["k_idx"] = [], [], [], []
            for q_rows, k_rows, rt, rr in _level_rows(m, w["rev"]):
                r = w["bt"][rt][rr:rr + 1, :]
                q_parts += [w["qt"][i] * jnp.exp2(w["bt"][i] - r) for i in q_rows]
                k_parts += [w["kt"][i] * jnp.exp2(r - w["bt"][i]) for i in k_rows]
                w["q_idx"] += q_rows
                w["k_idx"] += k_rows
            al = _dot_nt(jnp.concatenate(q_parts, axis=0).astype(BF16),
                         jnp.concatenate(k_parts, axis=0).astype(BF16))
            if m != half:
                al = al * mask_ref[li, :half, :half]
            w["al"] = al.astype(BF16)
        for w in work:
            vl = jnp.concatenate([w["vt"][i] for i in w["k_idx"]], axis=0).astype(BF16)
            ol = _row_tiles(_dot(w["al"], vl))
            for j, i in enumerate(w["q_idx"]):
                w["o"][i] = w["o"][i] + ol[j]
    outs = []
    for w in work:
        b_tot = w["bt"][0][0:1, :] if w["rev"] else w["bt"][nt - 1][SUBLANES - 1:SUBLANES, :]
        k_end = (w["k"] * jnp.exp2(b_tot - w["b"])).astype(BF16)
        st_new = w["st"] * jnp.exp2(b_tot) + _dot_tn(w["vb"], k_end)
        outs.append((jnp.concatenate(w["o"], axis=0), st_new))
    return outs


def _scan_kernel(*refs, r_layer, has_s0, emit_state, n_heads):
    lbl_ref, zq_ref, zff_ref, zfb_ref, zi_ref, zg_ref, gain_ref = refs[:7]
    pos = 7
    s0_ref = None
    if has_s0:
        s0_ref = refs[pos]
        pos += 1
    o_ref = refs[pos]
    pos += 1
    st_ref = None
    if emit_state:
        st_ref = refs[pos]
        pos += 1
    q_scr, o_scr, st_scr, mask_scr = refs[pos:pos + 4]

    c = SCAN_CHUNK
    w = HG_DK
    t_len = zq_ref.shape[0]
    n_chunks = t_len // c
    lg = lbl_ref[...]
    mx = lg[0]
    for j in range(1, N_REC_LAYERS):
        mx = jnp.maximum(mx, lg[j])
    ex = [jnp.exp(lg[j] - mx) for j in range(N_REC_LAYERS)]
    den = ex[0]
    for j in range(1, N_REC_LAYERS):
        den = den + ex[j]
    lower = jnp.zeros_like(mx)
    for j in range(1, r_layer + 1):
        lower = lower + ex[j] / den

    zq = zq_ref[...]
    q_scr[...] = zq * _sigmoid(zq)
    _scan_masks(mask_scr)
    row = lax.broadcasted_iota(jnp.int32, (c, c), 0)
    col = lax.broadcasted_iota(jnp.int32, (c, c), 1)
    tris = [jnp.where(col <= row, 1.0, 0.0).astype(BF16), jnp.where(col >= row, 1.0, 0.0).astype(BF16)]
    zf_refs = (zff_ref, zfb_ref)
    for d in range(2):
        for hh in range(n_heads):
            st_scr[d, hh] = s0_ref[d, hh].T if has_s0 else jnp.zeros((HG_DV, HG_DK), F32)

    def body(i, carry):
        chains = []
        for hh in range(n_heads):
            lanes = slice(hh * w, (hh + 1) * w)
            for d in range(2):
                ci = (n_chunks - 1 - i) if d == 1 else i
                rows = pl.ds(ci * c, c) if isinstance(ci, int) else pl.ds(pl.multiple_of(ci * c, c), c)
                chains.append((d, hh, rows, lanes, q_scr[rows, lanes], zf_refs[d][rows, lanes],
                               zi_ref[rows, lanes], st_scr[d, hh]))
        outs = _scan_chunks([(q, zf, v, lower[d:d + 1, lanes], st, tris[d], d == 1)
                             for d, hh, rows, lanes, q, zf, v, st in chains], mask_scr)
        for (d, hh, rows, lanes, *_), (o, st_new) in zip(chains, outs):
            st_scr[d, hh] = st_new
            o_scr[d, rows, lanes] = o
        return carry

    if n_chunks <= 2:
        for i in range(n_chunks):
            body(i, 0)
    else:
        lax.fori_loop(0, n_chunks, body, 0)
    if emit_state:
        for d in range(2):
            for hh in range(n_heads):
                st_ref[d, hh] = st_scr[d, hh].T

    zg = zg_ref[...]
    gate = zg * _sigmoid(zg)
    for hh in range(n_heads):
        lanes = slice(hh * w, (hh + 1) * w)
        y = _rms(o_scr[0, :, lanes] + o_scr[1, :, lanes], gain_ref[:, lanes]) * gate[:, lanes]
        o_ref[:, lanes] = y.astype(o_ref.dtype)


def _scan(z, lb_logits, out_gain, r_layer, *, t_len, n_heads, s0=None, emit_state=False):
    nb = z.shape[0] // t_len
    nh = HG_HEADS
    sh = n_heads
    ng = nh // sh
    wb = sh * HG_DK
    col = lambda k: (lambda b, h: (b, k * ng + h))
    in_specs = [pl.BlockSpec((N_REC_LAYERS, 2, wb), lambda b, h: (0, 0, h))]
    in_specs += [pl.BlockSpec((t_len, wb), col(k)) for k in range(5)]
    in_specs += [pl.BlockSpec((1, wb), lambda b, h: (0, h))]
    args = [lb_logits, z, z, z, z, z, out_gain]
    if s0 is not None:
        in_specs.append(pl.BlockSpec((None, None, 2, sh, HG_DK, HG_DV),
                                     lambda b, h: (b, r_layer, 0, h, 0, 0)))
        args.append(s0)
    out_specs = [pl.BlockSpec((t_len, wb), lambda b, h: (b, h))]
    out_shape = [jax.ShapeDtypeStruct((nb * t_len, nh * HG_DV), BF16)]
    if emit_state:
        out_specs.append(pl.BlockSpec((None, 2, sh, HG_DK, HG_DV), lambda b, h: (b, 0, h, 0, 0)))
        out_shape.append(jax.ShapeDtypeStruct((nb, 2, nh, HG_DK, HG_DV), F32))
    kern = functools.partial(_scan_kernel, r_layer=r_layer, has_s0=s0 is not None, emit_state=emit_state,
                             n_heads=n_heads)
    return pl.pallas_call(
        kern,
        grid=(nb, ng),
        in_specs=in_specs,
        out_specs=out_specs,
        out_shape=out_shape,
        scratch_shapes=[pltpu.VMEM((t_len, wb), F32),
                        pltpu.VMEM((2, t_len, wb), F32),
                        pltpu.VMEM((2, sh, HG_DV, HG_DK), F32),
                        pltpu.VMEM((len(SCAN_LEVELS) + 2, SCAN_CHUNK, SCAN_CHUNK), F32)],
        name="hgrn_scan",
    )(*args)


def _router_kernel(x_ref, g_ref, m_ref, wr_ref, h_ref, aff_ref):
    h = _rms(x_ref[...], g_ref[...])
    h = (h * (1.0 + m_ref[4:5, :]) + m_ref[3:4, :]).astype(BF16)
    h_ref[...] = h
    logits = _dot_nt(wr_ref[...], h)
    e = jnp.exp(logits - jnp.max(logits, axis=0, keepdims=True))
    aff_ref[...] = e / jnp.sum(e, axis=0, keepdims=True)


def _router(x, gain, mods, wr_t, *, bm, grp):
    n, d = x.shape
    return pl.pallas_call(
        _router_kernel,
        grid=(n // bm,),
        in_specs=[pl.BlockSpec((bm, d), lambda i: (i, 0)),
                  pl.BlockSpec((1, d), lambda i: (0, 0)),
                  pl.BlockSpec((None, 6, d), lambda i: (grp(i), 0, 0)),
                  pl.BlockSpec((N_EXPERTS, d), lambda i: (0, 0))],
        out_specs=[pl.BlockSpec((bm, d), lambda i: (i, 0)),
                   pl.BlockSpec((N_EXPERTS, bm), lambda i: (0, i))],
        out_shape=[jax.ShapeDtypeStruct((n, d), BF16),
                   jax.ShapeDtypeStruct((N_EXPERTS, n), F32)],
        name="router",
    )(x, gain, mods, wr_t)


def _one_hot_rows(posm, e0, g, cap):
    t_len = posm.shape[1]
    slot = lax.broadcasted_iota(jnp.int32, (cap, t_len), 0)
    parts = [jnp.where(posm[e:e + 1, :] == slot, 1.0, 0.0).astype(BF16) for e in range(e0, e0 + g)]
    return parts[0] if g == 1 else jnp.concatenate(parts, axis=0)


PREFIX_PIECE = 256


def _prefix_count(mask_f, before):
    pieces = []
    carry = jnp.zeros((mask_f.shape[0], 1), F32)
    for p0 in range(0, mask_f.shape[1], PREFIX_PIECE):
        piece = mask_f[:, p0:p0 + PREFIX_PIECE]
        pieces.append(_dot(piece.astype(BF16), before) + carry)
        carry = carry + jnp.sum(piece, axis=1, keepdims=True)
    return pieces[0] if len(pieces) == 1 else jnp.concatenate(pieces, axis=1)


def _sort_desc_segments(x, seg):
    w = x.shape[1]
    lane = lax.broadcasted_iota(jnp.int32, x.shape, 1)
    k = 2
    while k <= seg:
        j = k // 2
        while j >= 1:
            is_first = (lane & j) == 0
            partner = jnp.where(is_first, pltpu.roll(x, w - j, axis=1), pltpu.roll(x, j, axis=1))
            keep_max = is_first if k == seg else (is_first == ((lane & k) == 0))
            x = jnp.where(keep_max, jnp.maximum(x, partner), jnp.minimum(x, partner))
            j //= 2
        k *= 2
    return x


def _select_kernel(aff_ref, h_ref, g_ref, gate_ref, pos_ref, *, t_len):
    a = aff_ref[...]
    ne, w = a.shape
    cap = EC_CAPACITY * t_len // N_EXPERTS
    capf = jnp.float32(cap)
    srt = jnp.concatenate([_sort_desc_segments(a[r0:r0 + SUBLANES, :], t_len)
                           for r0 in range(0, ne, SUBLANES)], axis=0)
    src = lax.broadcasted_iota(jnp.int32, (PREFIX_PIECE, PREFIX_PIECE), 0)
    dst = lax.broadcasted_iota(jnp.int32, (PREFIX_PIECE, PREFIX_PIECE), 1)
    before = jnp.where(src < dst, 1.0, 0.0).astype(BF16)
    slot = lax.broadcasted_iota(jnp.int32, (cap, t_len), 0)
    g = max(1, min(ne, 512 // cap))
    for s in range(w // t_len):
        t0 = s * t_len
        c0 = s * cap
        seg = a[:, t0:t0 + t_len]
        thr = srt[:, t0 + cap - 1:t0 + cap]
        gt = jnp.where(seg > thr, 1.0, 0.0)
        eq = jnp.where(seg == thr, 1.0, 0.0)
        need = capf - jnp.sum(gt, axis=1, keepdims=True)
        eq_rank = _prefix_count(eq, before)
        sel = gt + eq * jnp.where(eq_rank < need, 1.0, 0.0)
        pos = _prefix_count(sel, before)
        posm = jnp.where(sel > 0.5, pos.astype(jnp.int32), -1)
        pos_ref[:, t0:t0 + t_len] = posm
        h = h_ref[t0:t0 + t_len, :]
        for e0 in range(0, ne, g):
            rows = _dot(_one_hot_rows(posm, e0, g, cap), h)
            g_ref[e0:e0 + g, c0:c0 + cap, :] = rows.reshape(g, cap, h.shape[1]).astype(g_ref.dtype)
        for e in range(ne):
            gate_ref[e, c0:c0 + cap, :] = jnp.sum(jnp.where(posm[e:e + 1, :] == slot, seg[e:e + 1, :], 0.0),
                                                  axis=1, keepdims=True)


def _select(aff_t, h, *, t_len):
    ne, n = aff_t.shape
    d = h.shape[1]
    w = SELECT_BLOCK
    capw = EC_CAPACITY * w // N_EXPERTS
    kern = functools.partial(_select_kernel, t_len=t_len)
    return pl.pallas_call(
        kern,
        grid=(n // w,),
        in_specs=[pl.BlockSpec((ne, w), lambda s: (0, s)),
                  pl.BlockSpec((w, d), lambda s: (s, 0))],
        out_specs=[pl.BlockSpec((ne, capw, d), lambda s: (0, s, 0)),
                   pl.BlockSpec((ne, capw, 1), lambda s: (0, s, 0)),
                   pl.BlockSpec((ne, w), lambda s: (0, s))],
        out_shape=[jax.ShapeDtypeStruct((ne, n // w * capw, d), BF16),
                   jax.ShapeDtypeStruct((ne, n // w * capw, 1), F32),
                   jax.ShapeDtypeStruct((ne, n), jnp.int32)],
        name="ec_select",
    )(aff_t, h)


def _ffn_kernel(xc_ref, xl_ref, gc_ref, gl_ref, wg_ref, wu_ref, wd_ref, yc_ref, yl_ref, accc, accl):
    f = pl.program_id(1)
    wg = wg_ref[...].astype(BF16)
    wu = wu_ref[...].astype(BF16)
    wd = wd_ref[...].astype(BF16)
    for x_ref, gate_ref, y_ref, acc in ((xc_ref, gc_ref, yc_ref, accc), (xl_ref, gl_ref, yl_ref, accl)):
        x = x_ref[...]
        hg = _dot(x, wg)
        hid = (hg * _sigmoid(hg) * _dot(x, wu)).astype(BF16)
        y = _dot(hid, wd)

        @pl.when(f == 0)
        def _(acc=acc, y=y):
            acc[...] = y

        @pl.when(f > 0)
        def _(acc=acc, y=y):
            acc[...] = acc[...] + y

        @pl.when(f == pl.num_programs(1) - 1)
        def _(acc=acc, y_ref=y_ref, gate_ref=gate_ref):
            y_ref[...] = (acc[...] * gate_ref[...]).astype(y_ref.dtype)


def _ffn(xc, xl, gc, gl, w_gate, w_up, w_down, layer, *, bf):
    ne, nc, d = xc.shape
    nl = xl.shape[1]
    ff = w_gate.shape[3]
    per_e = lambda e, f: (e, 0, 0)
    return pl.pallas_call(
        _ffn_kernel,
        grid=(ne, ff // bf),
        in_specs=[pl.BlockSpec((None, nc, d), per_e),
                  pl.BlockSpec((None, nl, d), per_e),
                  pl.BlockSpec((None, nc, 1), per_e),
                  pl.BlockSpec((None, nl, 1), per_e),
                  pl.BlockSpec((None, None, d, bf), lambda e, f: (layer, e, 0, f)),
                  pl.BlockSpec((None, None, d, bf), lambda e, f: (layer, e, 0, f)),
                  pl.BlockSpec((None, None, bf, d), lambda e, f: (layer, e, f, 0))],
        out_specs=[pl.BlockSpec((None, nc, d), per_e),
                   pl.BlockSpec((None, nl, d), per_e)],
        out_shape=[jax.ShapeDtypeStruct((ne, nc, d), BF16),
                   jax.ShapeDtypeStruct((ne, nl, d), BF16)],
        scratch_shapes=[pltpu.VMEM((nc, d), F32), pltpu.VMEM((nl, d), F32)],
        name="expert_ffn",
    )(xc, xl, gc, gl, w_gate, w_up, w_down)


def _combine_kernel(pos_ref, y_ref, x_ref, m_ref, o_ref, *, cap):
    posm = pos_ref[...]
    ne = posm.shape[0]
    d = x_ref.shape[1]
    g = max(1, min(ne, 512 // cap))
    acc = jnp.zeros(x_ref.shape, F32)
    for e0 in range(0, ne, g):
        acc = acc + _dot_tn(_one_hot_rows(posm, e0, g, cap), y_ref[e0:e0 + g].reshape(g * cap, d))
    o_ref[...] = x_ref[...] + m_ref[5:6, :] * acc


def _combine(posm, y, x, mods, *, t_len, tt, base):
    ne, n = posm.shape
    d = x.shape[1]
    ns = n // t_len
    cap = y.shape[1] // ns
    nt = t_len // tt
    kern = functools.partial(_combine_kernel, cap=cap)
    return pl.pallas_call(
        kern,
        grid=(ns, nt),
        in_specs=[pl.BlockSpec((ne, tt), lambda s, j: (0, s * nt + j)),
                  pl.BlockSpec((ne, cap, d), lambda s, j: (0, s, 0)),
                  pl.BlockSpec((tt, d), lambda s, j: (s * nt + j, 0)),
                  pl.BlockSpec((None, 6, d), lambda s, j: (base(s), 0, 0))],
        out_specs=pl.BlockSpec((tt, d), lambda s, j: (s * nt + j, 0)),
        out_shape=jax.ShapeDtypeStruct((n, d), F32),
        name="ec_combine",
    )(posm, y, x, mods)


def _final_norm_kernel(x_ref, g_ref, o_ref):
    o_ref[...] = _rms(x_ref[...], g_ref[...])


def _final_norm(x, gain, *, bm):
    n, d = x.shape
    return pl.pallas_call(
        _final_norm_kernel,
        grid=(n // bm,),
        in_specs=[pl.BlockSpec((bm, d), lambda i: (i, 0)), pl.BlockSpec((1, d), lambda i: (0, 0))],
        out_specs=pl.BlockSpec((bm, d), lambda i: (i, 0)),
        out_shape=jax.ShapeDtypeStruct((n, d), F32),
        name="final_norm",
    )(x, gain)


def _attn_in_weight(w):
    o_kr = MLA_Q_LORA + MLA_KV_LORA
    o_qg = o_kr + MLA_ROPE
    cols = jnp.concatenate([w[:, :o_kr], w[:, o_qg:], w[:, o_kr:o_qg]], axis=1)
    return jnp.pad(cols, ((0, 0), (0, ATTN_Z - cols.shape[1]))).astype(BF16)


def _heads_split(w, n_heads, first, second):
    k = w.shape[0]
    w3 = w.reshape(k, n_heads, first + second)
    return jnp.concatenate([w3[:, :, :first].reshape(k, n_heads * first),
                            w3[:, :, first:].reshape(k, n_heads * second)], axis=1).astype(BF16)


def _rope_tables(width, n_rep):
    half = width // 2
    t = jnp.arange(DEC_SEQ)
    rows = (t // GRID_W).astype(F32)
    cols = (t % GRID_W).astype(F32)
    inv_freq = ROPE_BASE ** (-jnp.arange(0, half, 2, dtype=F32) / half)
    ar = rows[:, None] * inv_freq
    ac = cols[:, None] * inv_freq
    cos = jnp.concatenate([jnp.cos(ar), jnp.cos(ar), jnp.cos(ac), jnp.cos(ac)], axis=1)
    sin = jnp.concatenate([-jnp.sin(ar), jnp.sin(ar), -jnp.sin(ac), jnp.sin(ac)], axis=1)
    return jnp.tile(cos, (1, n_rep)), jnp.tile(sin, (1, n_rep))


def kernel(x_prompt, x_sample, c, cache_mla_ckv, cache_mla_krope, cache_gqa_k, cache_gqa_v, state_hgrn,
           c_ctx, norm1, norm2, w_mod, b_mod, w_attn_in, mla_q_norm, w_mla_uq, mla_kv_norm, w_mla_ukv,
           gqa_sink, w_attn_out, w_rec_in, rec_lb_logits, rec_out_norm, w_rec_out, w_router, w_gate,
           w_up, w_down, final_norm):
    d = D_MODEL
    xc = x_prompt.reshape(BATCH * SEQ, d)
    xl = x_sample.reshape(DEC_BATCH * DEC_SEQ, d)
    cvecs = jnp.concatenate([c_ctx[None, :], c, jnp.zeros((N_MOD_ROWS - N_MOD_GROUPS, d), F32)], axis=0)
    mods_all = _mod_vectors(cvecs, w_mod, b_mod).reshape(DEPTH, N_MOD_ROWS, 6, d)[:, :N_MOD_GROUPS]
    grp_c = _group_map(None, 0, 0)
    cm, sm = _rope_tables(MLA_ROPE, MLA_HEADS)
    cg, sg = _rope_tables(GQA_HD, GQA_HEADS)
    kvw = GQA_KV_HEADS * GQA_HD

    ckv_l, kr_l, kg_l, vg_l, st_l = [], [], [], [], []
    for layer in range(DEPTH):
        mods = mods_all[layer]
        g1 = norm1[layer].reshape(1, d)
        g2 = norm2[layer].reshape(1, d)
        if layer % 2 == 0:
            a = layer // 2
            w_in = _attn_in_weight(w_attn_in[a])
            w_uq = _heads_split(w_mla_uq[a], MLA_HEADS, MLA_NOPE, MLA_ROPE)
            w_ukv = _heads_split(w_mla_ukv[a], MLA_HEADS, MLA_NOPE, MLA_V)
            w_out = w_attn_out[a].astype(BF16)
            qn_g = mla_q_norm[a].reshape(1, -1)
            kvn_g = mla_kv_norm[a].reshape(1, -1)
            sink = gqa_sink[a]
            zc = _modproj(xc, g1, mods, w_in, sub=0, bm=512, grp=grp_c)
            zl = _modproj(xl, g1, mods, w_in, sub=0, bm=512, grp=_group_map(DEC_SEQ, 512, 1))
            oc, ckv_n, k_rope, k_g, v_g = _attn_ctx(zc, sink, qn_g, w_uq, kvn_g, w_ukv)
            ckv_l.append(ckv_n.reshape(BATCH, SEQ, MLA_KV_LORA))
            kr_l.append(k_rope.reshape(BATCH, SEQ, MLA_ROPE))
            kg_l.append(k_g.reshape(BATCH, SEQ, GQA_KV_HEADS, GQA_HD))
            vg_l.append(v_g.reshape(BATCH, SEQ, GQA_KV_HEADS, GQA_HD))
            qn, qr, kn, vm, kr, qg, kg, vg = _lat_prep(zl, qn_g, w_uq, kvn_g, w_ukv, cm, sm, cg, sg, bm=256)
            kn_c, vm_c = _cache_kv(cache_mla_ckv, a, w_ukv)
            kr_c = jnp.pad(cache_mla_krope[:, a], ((0, 0), (0, 0), (0, LANES - MLA_ROPE))).astype(BF16)
            cat = lambda lat, ctx: jnp.concatenate([lat.reshape(DEC_BATCH, DEC_SEQ, -1), ctx], axis=1)
            ol = _lat_attn(sink, qn, qr, cat(kn, kn_c), cat(kr, kr_c), cat(vm, vm_c), qg, kg, vg,
                           cache_gqa_k[:, a].reshape(DEC_BATCH, PAST_LEN, kvw).astype(BF16),
                           cache_gqa_v[:, a].reshape(DEC_BATCH, PAST_LEN, kvw).astype(BF16))
        else:
            r = layer // 2
            w_in = w_rec_in[r].astype(BF16)
            w_out = w_rec_out[r].astype(BF16)
            og = rec_out_norm[r].reshape(1, -1)
            zc = _modproj(xc, g1, mods, w_in, sub=0, bm=256, grp=grp_c)
            zl = _modproj(xl, g1, mods, w_in, sub=0, bm=256, grp=_group_map(DEC_SEQ, 256, 1))
            oc, st = _scan(zc, rec_lb_logits, og, r, t_len=SEQ, n_heads=4, emit_state=True)
            (ol,) = _scan(zl, rec_lb_logits, og, r, t_len=DEC_SEQ, n_heads=2, s0=state_hgrn)
            st_l.append(st)
        xc = _outproj(oc, w_out, xc, mods, sub=0, bm=512, grp=grp_c)
        xl = _outproj(ol, w_out, xl, mods, sub=0, bm=512, grp=_group_map(DEC_SEQ, 512, 1))

        wr_t = w_router[layer].T.astype(BF16)
        hc, aff_c = _router(xc, g2, mods, wr_t, bm=512, grp=grp_c)
        hl, aff_l = _router(xl, g2, mods, wr_t, bm=512, grp=_group_map(DEC_SEQ, 512, 1))
        gxc, gate_c, pos_c = _select(aff_c, hc, t_len=SEQ)
        gxl, gate_l, pos_l = _select(aff_l, hl, t_len=DEC_SEQ)
        yc, yl = _ffn(gxc, gxl, gate_c, gate_l, w_gate, w_up, w_down, layer, bf=512)
        xc = _combine(pos_c, yc, xc, mods, t_len=SEQ, tt=SEQ, base=lambda s: 0)
        xl = _combine(pos_l, yl, xl, mods, t_len=DEC_SEQ, tt=512, base=lambda s: 1 + s)

    fn = final_norm.reshape(1, d)
    y_prompt = _final_norm(xc, fn, bm=512).reshape(BATCH, SEQ, d)
    y_sample = _final_norm(xl, fn, bm=512).reshape(DEC_BATCH, DEC_SEQ, d)
    return (y_prompt, y_sample, jnp.stack(ckv_l, axis=1), jnp.stack(kr_l, axis=1),
            jnp.stack(kg_l, axis=1), jnp.stack(vg_l, axis=1), jnp.stack(st_l, axis=1))
```

```python
import functools

import jax
import jax.numpy as jnp
from jax import lax
from jax.experimental import pallas as pl
from jax.experimental.pallas import tpu as pltpu

F32 = jnp.float32
BF16 = jnp.bfloat16

D_MODEL = 1024
BATCH = 32
SEQ = 256
DEPTH = 4
DEC_BATCH = 2
DEC_SEQ = 2048
PAST_LEN = 512
GRID_W = 64
ROPE_BASE = 10000.0
NORM_EPS = 1e-6
NEG_BIG = -1e30
F_MIN = 1e-6
N_ATTN_LAYERS = (DEPTH + 1) // 2
N_REC_LAYERS = DEPTH // 2
MLA_HEADS = 8
MLA_Q_LORA = 384
MLA_KV_LORA = 256
MLA_NOPE = 64
MLA_ROPE = 32
MLA_V = 64
MLA_SCALE = (MLA_NOPE + MLA_ROPE) ** -0.5
GQA_HEADS = 8
GQA_KV_HEADS = 2
GQA_GROUP = GQA_HEADS // GQA_KV_HEADS
GQA_HD = 64
GQA_SCALE = GQA_HD ** -0.5
WINDOW = 128
BAND_BLOCK = 128
HG_HEADS = 8
HG_DK = 128
HG_DV = 128
REC_IN = 3 * HG_HEADS * HG_DK + 2 * HG_HEADS * HG_DV
N_EXPERTS = 16
EC_CAPACITY = 2
D_FF = 2048

LANES = 128
N_MOD_ROWS = 8
N_MOD_GROUPS = 1 + DEC_BATCH

ZC_Q = 0
ZC_KV = ZC_Q + MLA_Q_LORA
ZC_QG = ZC_KV + MLA_KV_LORA
ZC_KG = ZC_QG + GQA_HEADS * GQA_HD
ZC_VG = ZC_KG + GQA_KV_HEADS * GQA_HD
ZC_KR = ZC_VG + GQA_KV_HEADS * GQA_HD
ATTN_Z = ZC_KR + LANES
KR_OFF = MLA_NOPE
QK_W = MLA_HEADS * LANES
VM_W = MLA_HEADS * MLA_V
O_M_W = MLA_HEADS * MLA_V
O_G_W = GQA_HEADS * GQA_HD

SUBLANES = 8
SCAN_CHUNK = 128
SCAN_LEVELS = (64, 32, 16, 8)
SCAN_DIAG = 8
DIAG_CLAMP = 88.0
SELECT_BLOCK = 2048
FFN_ROWS = 256
LAT_QB = 512


def _dot(a, b):
    return jnp.dot(a, b, preferred_element_type=F32)


def _dot_nt(a, b):
    return lax.dot_general(a, b, (((1,), (1,)), ((), ())), preferred_element_type=F32)


def _dot_tn(a, b):
    return lax.dot_general(a, b, (((0,), (0,)), ((), ())), preferred_element_type=F32)


def _sigmoid(x):
    return 1.0 / (1.0 + jnp.exp(-x))


def _rms(x, gain):
    return x * lax.rsqrt(jnp.mean(x * x, axis=-1, keepdims=True) + NORM_EPS) * gain


def _mod_kernel(c_ref, w_ref, b_ref, o_ref):
    c = c_ref[...]
    s = (c * _sigmoid(c)).astype(BF16)
    o_ref[...] = _dot(s, w_ref[...].astype(BF16)) + b_ref[...]


def _mod_vectors(cvecs, w_mod, b_mod):
    bn = 1536
    nd = 6 * D_MODEL
    return pl.pallas_call(
        _mod_kernel,
        grid=(DEPTH, nd // bn),
        in_specs=[pl.BlockSpec((N_MOD_ROWS, D_MODEL), lambda l, j: (0, 0)),
                  pl.BlockSpec((None, D_MODEL, bn), lambda l, j: (l, 0, j)),
                  pl.BlockSpec((None, 1, bn), lambda l, j: (l, 0, j))],
        out_specs=pl.BlockSpec((None, N_MOD_ROWS, bn), lambda l, j: (l, 0, j)),
        out_shape=jax.ShapeDtypeStruct((DEPTH, N_MOD_ROWS, nd), F32),
        name="mod_vectors",
    )(cvecs, w_mod, b_mod.reshape(DEPTH, 1, nd))


def _group_map(rows_per_group, bm, base):
    if rows_per_group is None:
        return lambda i: base
    return lambda i: base + (i * bm) // rows_per_group


def _modproj_kernel(x_ref, g_ref, m_ref, w_ref, o_ref, *, shift_row, chunk):
    h = _rms(x_ref[...], g_ref[...])
    h = (h * (1.0 + m_ref[shift_row + 1:shift_row + 2, :]) + m_ref[shift_row:shift_row + 1, :]).astype(BF16)
    nout = o_ref.shape[1]
    for j in range(0, nout, chunk):
        o_ref[:, j:j + chunk] = _dot(h, w_ref[:, j:j + chunk])


def _modproj(x, gain, mods, w, *, sub, bm, grp):
    n, d = x.shape
    nout = w.shape[1]
    chunk = nout if nout <= 1536 else 1024
    kern = functools.partial(_modproj_kernel, shift_row=3 * sub, chunk=chunk)
    return pl.pallas_call(
        kern,
        grid=(n // bm,),
        in_specs=[pl.BlockSpec((bm, d), lambda i: (i, 0)),
                  pl.BlockSpec((1, d), lambda i: (0, 0)),
                  pl.BlockSpec((None, 6, d), lambda i: (grp(i), 0, 0)),
                  pl.BlockSpec((d, nout), lambda i: (0, 0))],
        out_specs=pl.BlockSpec((bm, nout), lambda i: (i, 0)),
        out_shape=jax.ShapeDtypeStruct((n, nout), F32),
        name="modproj",
    )(x, gain, mods, w)


def _outproj_router_kernel(a_ref, w_ref, x_ref, m_ref, g_ref, wr_ref, x_out, h_out, aff_out):
    x = x_ref[...] + m_ref[2:3, :] * _dot(a_ref[...], w_ref[...])
    x_out[...] = x
    h = (_rms(x, g_ref[...]) * (1.0 + m_ref[4:5, :]) + m_ref[3:4, :]).astype(BF16)
    h_out[...] = h
    logits = _dot_nt(wr_ref[...], h)
    e = jnp.exp(logits - jnp.max(logits, axis=0, keepdims=True))
    aff_out[...] = e / jnp.sum(e, axis=0, keepdims=True)


def _outproj_router(a, w, x, mods, gain2, wr_t, *, bm, grp):
    n, d = x.shape
    k = a.shape[1]
    row = lambda i: (i, 0)
    const = lambda i: (0, 0)
    return pl.pallas_call(
        _outproj_router_kernel,
        grid=(n // bm,),
        in_specs=[pl.BlockSpec((bm, k), row),
                  pl.BlockSpec((k, d), const),
                  pl.BlockSpec((bm, d), row),
                  pl.BlockSpec((None, 6, d), lambda i: (grp(i), 0, 0)),
                  pl.BlockSpec((1, d), const),
                  pl.BlockSpec((N_EXPERTS, d), const)],
        out_specs=[pl.BlockSpec((bm, d), row),
                   pl.BlockSpec((bm, d), row),
                   pl.BlockSpec((N_EXPERTS, bm), lambda i: (0, i))],
        out_shape=[jax.ShapeDtypeStruct((n, d), F32),
                   jax.ShapeDtypeStruct((n, d), BF16),
                   jax.ShapeDtypeStruct((N_EXPERTS, n), F32)],
        name="outproj_router",
    )(a, w, x, mods, gain2, wr_t)


def _softmax_pv(s, v, sink=None):
    m = jnp.max(s, axis=-1, keepdims=True)
    if sink is not None:
        m = jnp.maximum(m, sink)
    p = jnp.exp(s - m)
    l = jnp.sum(p, axis=-1, keepdims=True)
    if sink is not None:
        l = l + jnp.exp(sink - m)
    return _dot(p.astype(BF16), v) / l


def _swap_pairs(x, q):
    w = x.shape[1]
    lane = lax.broadcasted_iota(jnp.int32, x.shape, 1)
    first = (lane % (2 * q)) < q
    return jnp.where(first, pltpu.roll(x, w - q, axis=1), pltpu.roll(x, q, axis=1))


def _rope(x, cos, sin, q):
    return x * cos + _swap_pairs(x, q) * sin


def _attn_project(z, qn_ref, wuq_ref, kvn_ref, wukv_ref):
    cq = _rms(z[:, ZC_Q:ZC_Q + MLA_Q_LORA], qn_ref[...]).astype(BF16)
    qcat = _dot(cq, wuq_ref[...])
    ckv_n = _rms(z[:, ZC_KV:ZC_KV + MLA_KV_LORA], kvn_ref[...])
    kv = _dot(ckv_n.astype(BF16), wukv_ref[...])
    return qcat, ckv_n, kv


def _tile_heads(kr_slot):
    return jnp.concatenate([kr_slot] * MLA_HEADS, axis=1)


def _staggered(n, first, second, depth):
    pending = []
    for i in range(n):
        pending.append((i, first(i)))
        if len(pending) > depth:
            second(*pending.pop(0))
    for item in pending:
        second(*item)


def _gqa_sink_column(sink_ref, g, rows):
    return jnp.concatenate([jnp.full((rows, 1), sink_ref[g * GQA_GROUP + j], F32) for j in range(GQA_GROUP)], axis=0)


def _stack_group(qg, g):
    return jnp.concatenate([qg[:, (g * GQA_GROUP + j) * GQA_HD:(g * GQA_GROUP + j + 1) * GQA_HD]
                            for j in range(GQA_GROUP)], axis=0)


def _attn_ctx_kernel(sink_ref, z_ref, qn_ref, wuq_ref, kvn_ref, wukv_ref,
                     o_ref, ckv_ref, kr_ref, kg_ref, vg_ref):
    z = z_ref[...]
    cq = _rms(z[:, ZC_Q:ZC_Q + MLA_Q_LORA], qn_ref[...]).astype(BF16)
    qm = _dot(cq, wuq_ref[...])
    ckv_n = _rms(z[:, ZC_KV:ZC_KV + MLA_KV_LORA], kvn_ref[...])
    kv = _dot(ckv_n.astype(BF16), wukv_ref[...])
    kr = z[:, ZC_KR + KR_OFF:ZC_KR + KR_OFF + MLA_ROPE]
    kg = z[:, ZC_KG:ZC_KG + GQA_KV_HEADS * GQA_HD]
    vg = z[:, ZC_VG:ZC_VG + GQA_KV_HEADS * GQA_HD]
    ckv_ref[...] = ckv_n
    kr_ref[...] = kr
    kg_ref[...] = kg
    vg_ref[...] = vg
    qn_w = MLA_HEADS * MLA_NOPE
    qn = (qm[:, :qn_w] * MLA_SCALE).astype(BF16)
    qr = (qm[:, qn_w:] * MLA_SCALE).astype(BF16)
    kn = kv[:, :qn_w].astype(BF16)
    vm = kv[:, qn_w:].astype(BF16)
    krb = kr.astype(BF16)
    for h in range(MLA_HEADS):
        s = (_dot_nt(qn[:, h * MLA_NOPE:(h + 1) * MLA_NOPE], kn[:, h * MLA_NOPE:(h + 1) * MLA_NOPE])
             + _dot_nt(qr[:, h * MLA_ROPE:(h + 1) * MLA_ROPE], krb))
        o = _softmax_pv(s, vm[:, h * MLA_V:(h + 1) * MLA_V])
        o_ref[:, h * MLA_V:(h + 1) * MLA_V] = o.astype(o_ref.dtype)
    qg = (z[:, ZC_QG:ZC_QG + O_G_W] * GQA_SCALE).astype(BF16)
    kgb = kg.astype(BF16)
    vgb = vg.astype(BF16)
    for h in range(GQA_HEADS):
        g = h // GQA_GROUP
        s = _dot_nt(qg[:, h * GQA_HD:(h + 1) * GQA_HD], kgb[:, g * GQA_HD:(g + 1) * GQA_HD])
        o = _softmax_pv(s, vgb[:, g * GQA_HD:(g + 1) * GQA_HD], sink_ref[h])
        o_ref[:, O_M_W + h * GQA_HD:O_M_W + (h + 1) * GQA_HD] = o.astype(o_ref.dtype)


def _attn_ctx(z, sink, q_norm, w_uq, kv_norm, w_ukv):
    nb = z.shape[0] // SEQ
    kvw = GQA_KV_HEADS * GQA_HD
    const = lambda b: (0, 0)
    row = lambda b: (b, 0)
    return pl.pallas_call(
        _attn_ctx_kernel,
        grid=(nb,),
        in_specs=[pl.BlockSpec(memory_space=pltpu.SMEM),
                  pl.BlockSpec((SEQ, ATTN_Z), row),
                  pl.BlockSpec((1, MLA_Q_LORA), const),
                  pl.BlockSpec(w_uq.shape, const),
                  pl.BlockSpec((1, MLA_KV_LORA), const),
                  pl.BlockSpec(w_ukv.shape, const)],
        out_specs=[pl.BlockSpec((SEQ, O_M_W + O_G_W), row),
                   pl.BlockSpec((SEQ, MLA_KV_LORA), row),
                   pl.BlockSpec((SEQ, MLA_ROPE), row),
                   pl.BlockSpec((SEQ, kvw), row),
                   pl.BlockSpec((SEQ, kvw), row)],
        out_shape=[jax.ShapeDtypeStruct((nb * SEQ, O_M_W + O_G_W), BF16),
                   jax.ShapeDtypeStruct((nb * SEQ, MLA_KV_LORA), F32),
                   jax.ShapeDtypeStruct((nb * SEQ, MLA_ROPE), F32),
                   jax.ShapeDtypeStruct((nb * SEQ, kvw), F32),
                   jax.ShapeDtypeStruct((nb * SEQ, kvw), F32)],
        name="attn_ctx",
    )(sink, z, q_norm, w_uq, kv_norm, w_ukv)


def _lat_prep_kernel(z_ref, qn_ref, wuq_ref, kvn_ref, wukv_ref, cm_ref, sm_ref, cg_ref, sg_ref,
                     qcat_o, kcat_o, vm_o, qg_o, kg_o, vg_o):
    z = z_ref[...]
    qcat, ckv_n, kv = _attn_project(z, qn_ref, wuq_ref, kvn_ref, wukv_ref)
    cm = cm_ref[...]
    sm = sm_ref[...]
    cg = cg_ref[...]
    sg = sg_ref[...]
    qcat_o[...] = (_rope(qcat, cm, sm, MLA_ROPE // 4) * MLA_SCALE).astype(BF16)
    kr_slot = _rope(z[:, ZC_KR:ZC_KR + LANES], cm[:, :LANES], sm[:, :LANES], MLA_ROPE // 4)
    kcat_o[...] = (kv[:, :QK_W] + _tile_heads(kr_slot)).astype(BF16)
    vm_o[...] = kv[:, QK_W:].astype(BF16)
    qg_o[...] = (_rope(z[:, ZC_QG:ZC_QG + O_G_W], cg, sg, GQA_HD // 4) * GQA_SCALE).astype(BF16)
    kvw = GQA_KV_HEADS * GQA_HD
    kg_o[...] = _rope(z[:, ZC_KG:ZC_KG + kvw], cg[:, :kvw], sg[:, :kvw], GQA_HD // 4).astype(BF16)
    vg_o[...] = z[:, ZC_VG:ZC_VG + kvw].astype(BF16)


def _lat_prep(z, q_norm, w_uq, kv_norm, w_ukv, cm, sm, cg, sg, *, bm):
    n = z.shape[0]
    per = DEC_SEQ // bm
    kvw = GQA_KV_HEADS * GQA_HD
    const = lambda i: (0, 0)
    row = lambda i: (i, 0)
    pos = lambda i: (i % per, 0)
    widths = (QK_W, QK_W, VM_W, O_G_W, kvw, kvw)
    return pl.pallas_call(
        _lat_prep_kernel,
        grid=(n // bm,),
        in_specs=[pl.BlockSpec((bm, ATTN_Z), row),
                  pl.BlockSpec((1, MLA_Q_LORA), const),
                  pl.BlockSpec(w_uq.shape, const),
                  pl.BlockSpec((1, MLA_KV_LORA), const),
                  pl.BlockSpec(w_ukv.shape, const),
                  pl.BlockSpec((bm, QK_W), pos),
                  pl.BlockSpec((bm, QK_W), pos),
                  pl.BlockSpec((bm, O_G_W), pos),
                  pl.BlockSpec((bm, O_G_W), pos)],
        out_specs=[pl.BlockSpec((bm, w), row) for w in widths],
        out_shape=[jax.ShapeDtypeStruct((n, w), BF16) for w in widths],
        name="lat_prep",
    )(z, q_norm, w_uq, kv_norm, w_ukv, cm, sm, cg, sg)


def _cache_kv_kernel(c_ref, kr_ref, w_ref, kcat_o, vm_o):
    kv = _dot(c_ref[...].astype(BF16), w_ref[...])
    kcat_o[...] = (kv[:, :QK_W] + _tile_heads(kr_ref[...])).astype(BF16)
    vm_o[...] = kv[:, QK_W:].astype(BF16)


def _cache_kv(cache_ckv, kr_slot, a, w_ukv):
    nb = cache_ckv.shape[0]
    return pl.pallas_call(
        _cache_kv_kernel,
        grid=(nb,),
        in_specs=[pl.BlockSpec((None, None, PAST_LEN, MLA_KV_LORA), lambda b: (b, a, 0, 0)),
                  pl.BlockSpec((None, PAST_LEN, LANES), lambda b: (b, 0, 0)),
                  pl.BlockSpec(w_ukv.shape, lambda b: (0, 0))],
        out_specs=[pl.BlockSpec((None, PAST_LEN, QK_W), lambda b: (b, 0, 0)),
                   pl.BlockSpec((None, PAST_LEN, VM_W), lambda b: (b, 0, 0))],
        out_shape=[jax.ShapeDtypeStruct((nb, PAST_LEN, QK_W), BF16),
                   jax.ShapeDtypeStruct((nb, PAST_LEN, VM_W), BF16)],
        name="cache_kv",
    )(cache_ckv, kr_slot, w_ukv)


def _lat_attn_kernel(sink_ref, qcat_ref, kcat_ref, vm_ref, qg_ref, kg_ref, vg_ref, kctx_ref, vctx_ref, o_ref):
    n = pl.program_id(1)
    qcat = qcat_ref[...]

    def mla_scores(h):
        return _dot_nt(qcat[:, h * LANES:(h + 1) * LANES], kcat_ref[:, h * LANES:(h + 1) * LANES])

    def mla_out(h, s):
        o_ref[:, h * MLA_V:(h + 1) * MLA_V] = _softmax_pv(s, vm_ref[:, h * MLA_V:(h + 1) * MLA_V]).astype(o_ref.dtype)

    _staggered(MLA_HEADS, mla_scores, mla_out, depth=2)

    bb = BAND_BLOCK
    nsub = qg_ref.shape[0] // bb
    rows = GQA_GROUP * bb
    tq = lax.broadcasted_iota(jnp.int32, (rows, 3 * bb), 0) & (bb - 1)
    kk = lax.broadcasted_iota(jnp.int32, (rows, 3 * bb), 1)
    qg = qg_ref[...]
    kctx = kctx_ref[...]
    vctx = vctx_ref[...]

    def gqa_scores(i):
        j, g = divmod(i, GQA_KV_HEADS)
        hs = slice(g * GQA_HD, (g + 1) * GQA_HD)
        blk = n * nsub + j
        start = pl.multiple_of(blk * bb, bb)
        lo = jnp.maximum(tq, (1 - blk) * bb)
        hi = jnp.minimum(tq + 2 * WINDOW, (DEC_SEQ // bb + 1 - blk) * bb - 1)
        q = _stack_group(qg[j * bb:(j + 1) * bb, :], g)
        s_loc = jnp.where((kk >= lo) & (kk <= hi), _dot_nt(q, kg_ref[pl.ds(start, 3 * bb), hs]), NEG_BIG)
        return s_loc, _dot_nt(q, kctx[:, hs]), start

    def gqa_out(i, res):
        j, g = divmod(i, GQA_KV_HEADS)
        hs = slice(g * GQA_HD, (g + 1) * GQA_HD)
        s_loc, s_ctx, start = res
        sink = _gqa_sink_column(sink_ref, g, bb)
        m = jnp.maximum(jnp.maximum(jnp.max(s_loc, axis=-1, keepdims=True),
                                    jnp.max(s_ctx, axis=-1, keepdims=True)), sink)
        p_loc = jnp.exp(s_loc - m)
        p_ctx = jnp.exp(s_ctx - m)
        l = (jnp.sum(p_loc, axis=-1, keepdims=True) + jnp.sum(p_ctx, axis=-1, keepdims=True)
             + jnp.exp(sink - m))
        o = (_dot(p_loc.astype(BF16), vg_ref[pl.ds(start, 3 * bb), hs]) + _dot(p_ctx.astype(BF16), vctx[:, hs])) / l
        for jj in range(GQA_GROUP):
            h = g * GQA_GROUP + jj
            o_ref[j * bb:(j + 1) * bb, O_M_W + h * GQA_HD:O_M_W + (h + 1) * GQA_HD] = (
                o[jj * bb:(jj + 1) * bb].astype(o_ref.dtype))

    _staggered(nsub * GQA_KV_HEADS, gqa_scores, gqa_out, depth=3)


def _lat_attn(sink, qcat, kcat_all, vm_all, qg, kg_pad, vg_pad, kg_ctx, vg_ctx, *, qb):
    nb = DEC_BATCH
    nq = DEC_SEQ // qb
    tk = kcat_all.shape[1]
    tp = kg_pad.shape[1]
    kvw = GQA_KV_HEADS * GQA_HD
    qrow = lambda b, n: (b * nq + n, 0)
    per_b = lambda b, n: (b, 0, 0)
    return pl.pallas_call(
        _lat_attn_kernel,
        grid=(nb, nq),
        in_specs=[pl.BlockSpec(memory_space=pltpu.SMEM),
                  pl.BlockSpec((qb, QK_W), qrow),
                  pl.BlockSpec((None, tk, QK_W), per_b),
                  pl.BlockSpec((None, tk, VM_W), per_b),
                  pl.BlockSpec((qb, O_G_W), qrow),
                  pl.BlockSpec((None, tp, kvw), per_b),
                  pl.BlockSpec((None, tp, kvw), per_b),
                  pl.BlockSpec((None, PAST_LEN, kvw), per_b),
                  pl.BlockSpec((None, PAST_LEN, kvw), per_b)],
        out_specs=pl.BlockSpec((qb, O_M_W + O_G_W), qrow),
        out_shape=jax.ShapeDtypeStruct((nb * DEC_SEQ, O_M_W + O_G_W), BF16),
        name="lat_attn",
    )(sink, qcat, kcat_all, vm_all, qg, kg_pad, vg_pad, kg_ctx, vg_ctx)


def _row_tiles(x):
    return [x[i * SUBLANES:(i + 1) * SUBLANES, :] for i in range(x.shape[0] // SUBLANES)]


def _scan_masks(mask_ref):
    c = SCAN_CHUNK
    t = lax.broadcasted_iota(jnp.int32, (c, c), 0)
    s = lax.broadcasted_iota(jnp.int32, (c, c), 1)
    for li, m in enumerate(SCAN_LEVELS):
        sh = m.bit_length() - 1
        mask_ref[li] = jnp.where((t >> sh) == (s >> sh), 1.0, 0.0)
    sh = SCAN_DIAG.bit_length() - 1
    same = (t >> sh) == (s >> sh)
    mask_ref[len(SCAN_LEVELS)] = jnp.where(same & (s <= t), 1.0, 0.0)
    mask_ref[len(SCAN_LEVELS) + 1] = jnp.where(same & (s >= t), 1.0, 0.0)


def _level_rows(m, rev):
    mt = m // SUBLANES
    nt = SCAN_CHUNK // SUBLANES
    groups = []
    for g in range(nt // (2 * mt)):
        first = list(range(g * 2 * mt, g * 2 * mt + mt))
        second = list(range(g * 2 * mt + mt, (g + 1) * 2 * mt))
        if rev:
            groups.append((first, second, second[0], 0))
        else:
            groups.append((second, first, first[-1], SUBLANES - 1))
    return groups


def _scan_chunks(chains, mask_ref):
    c = SCAN_CHUNK
    half = c // 2
    nt = c // SUBLANES
    n_lv = len(SCAN_LEVELS)
    work = []
    for q, zf, v, lb, st, tri, rev in chains:
        f = lb + (1.0 - lb) * _sigmoid(zf)
        lf = jnp.log2(jnp.maximum(f, F_MIN))
        hi = lf.astype(BF16)
        mid = (lf - hi.astype(F32)).astype(BF16)
        work.append(dict(q=q, k=1.0 - f, v=v, vb=v.astype(BF16), st=st, tri=tri, rev=rev, hi=hi, mid=mid))
    for w in work:
        w["b"] = _dot(w["tri"], w["hi"]) + _dot(w["tri"], w["mid"])
        w["qt"], w["kt"], w["bt"], w["vt"] = (_row_tiles(w[n]) for n in ("q", "k", "b", "v"))
    for w in work:
        mid_row = SCAN_DIAG // 2 if w["rev"] else SCAN_DIAG // 2 - 1
        e = jnp.concatenate([jnp.clip(t - t[mid_row:mid_row + 1, :], -DIAG_CLAMP, DIAG_CLAMP) for t in w["bt"]],
                            axis=0)
        a = _dot_nt((w["q"] * jnp.exp2(e)).astype(BF16), (w["k"] * jnp.exp2(-e)).astype(BF16))
        w["a"] = jnp.where(mask_ref[n_lv + (1 if w["rev"] else 0)] > 0.5, a, 0.0).astype(BF16)
    for w in work:
        o = _dot(w["a"], w["vb"]) + _dot_nt((w["q"] * jnp.exp2(w["b"])).astype(BF16), w["st"].astype(BF16))
        w["o"] = _row_tiles(o)
    for li, m in enumerate(SCAN_LEVELS):
        for w in work:
            q_parts, k_parts, w["q_idx"], w["k_idx"] = [], [], [], []
            for q_rows, k_rows, rt, rr in _level_rows(m, w["rev"]):
                r = w["bt"][rt][rr:rr + 1, :]
                q_parts += [w["qt"][i] * jnp.exp2(w["bt"][i] - r) for i in q_rows]
                k_parts += [w["kt"][i] * jnp.exp2(r - w["bt"][i]) for i in k_rows]
                w["q_idx"] += q_rows
                w["k_idx"] += k_rows
            al = _dot_nt(jnp.concatenate(q_parts, axis=0).astype(BF16),
                         jnp.concatenate(k_parts, axis=0).astype(BF16))
            if m != half:
                al = al * mask_ref[li, :half, :half]
            w["al"] = al.astype(BF16)
        for w in work:
            vl = jnp.concatenate([w["vt"][i] for i in w["k_idx"]], axis=0).astype(BF16)
            ol = _row_tiles(_dot(w["al"], vl))
            for j, i in enumerate(w["q_idx"]):
                w["o"][i] = w["o"][i] + ol[j]
    outs = []
    for w in work:
        b_tot = w["bt"][0][0:1, :] if w["rev"] else w["bt"][nt - 1][SUBLANES - 1:SUBLANES, :]
        k_end = (w["k"] * jnp.exp2(b_tot - w["b"])).astype(BF16)
        st_new = w["st"] * jnp.exp2(b_tot) + _dot_tn(w["vb"], k_end)
        outs.append((jnp.concatenate(w["o"], axis=0), st_new))
    return outs


def _scan_kernel(*refs, r_layer, has_s0, emit_state, n_heads):
    lbl_ref, zq_ref, zff_ref, zfb_ref, zi_ref, zg_ref, gain_ref = refs[:7]
    pos = 7
    s0_ref = None
    if has_s0:
        s0_ref = refs[pos]
        pos += 1
    o_ref = refs[pos]
    pos += 1
    st_ref = None
    if emit_state:
        st_ref = refs[pos]
        pos += 1
    q_scr, o_scr, st_scr, mask_scr = refs[pos:pos + 4]

    c = SCAN_CHUNK
    w = HG_DK
    t_len = zq_ref.shape[0]
    n_chunks = t_len // c
    lg = lbl_ref[...]
    mx = lg[0]
    for j in range(1, N_REC_LAYERS):
        mx = jnp.maximum(mx, lg[j])
    ex = [jnp.exp(lg[j] - mx) for j in range(N_REC_LAYERS)]
    den = ex[0]
    for j in range(1, N_REC_LAYERS):
        den = den + ex[j]
    lower = jnp.zeros_like(mx)
    for j in range(1, r_layer + 1):
        lower = lower + ex[j] / den

    zq = zq_ref[...]
    q_scr[...] = zq * _sigmoid(zq)
    _scan_masks(mask_scr)
    row = lax.broadcasted_iota(jnp.int32, (c, c), 0)
    col = lax.broadcasted_iota(jnp.int32, (c, c), 1)
    tris = [jnp.where(col <= row, 1.0, 0.0).astype(BF16), jnp.where(col >= row, 1.0, 0.0).astype(BF16)]
    zf_refs = (zff_ref, zfb_ref)
    for d in range(2):
        for hh in range(n_heads):
            st_scr[d, hh] = s0_ref[d, hh].T if has_s0 else jnp.zeros((HG_DV, HG_DK), F32)

    def body(i, carry):
        chains = []
        for hh in range(n_heads):
            lanes = slice(hh * w, (hh + 1) * w)
            for d in range(2):
                ci = (n_chunks - 1 - i) if d == 1 else i
                rows = pl.ds(ci * c, c) if isinstance(ci, int) else pl.ds(pl.multiple_of(ci * c, c), c)
                chains.append((d, hh, rows, lanes, q_scr[rows, lanes], zf_refs[d][rows, lanes],
                               zi_ref[rows, lanes], st_scr[d, hh]))
        outs = _scan_chunks([(q, zf, v, lower[d:d + 1, lanes], st, tris[d], d == 1)
                             for d, hh, rows, lanes, q, zf, v, st in chains], mask_scr)
        for (d, hh, rows, lanes, *_), (o, st_new) in zip(chains, outs):
            st_scr[d, hh] = st_new
            o_scr[d, rows, lanes] = o
        return carry

    if n_chunks <= 2:
        for i in range(n_chunks):
            body(i, 0)
    else:
        lax.fori_loop(0, n_chunks, body, 0)
    if emit_state:
        for d in range(2):
            for hh in range(n_heads):
                st_ref[d, hh] = st_scr[d, hh].T

    zg = zg_ref[...]
    gate = zg * _sigmoid(zg)
    for hh in range(n_heads):
        lanes = slice(hh * w, (hh + 1) * w)
        y = _rms(o_scr[0, :, lanes] + o_scr[1, :, lanes], gain_ref[:, lanes]) * gate[:, lanes]
        o_ref[:, lanes] = y.astype(o_ref.dtype)


def _scan(z, lb_logits, out_gain, r_layer, *, t_len, n_heads, s0=None, emit_state=False):
    nb = z.shape[0] // t_len
    nh = HG_HEADS
    sh = n_heads
    ng = nh // sh
    wb = sh * HG_DK
    col = lambda k: (lambda b, h: (b, k * ng + h))
    in_specs = [pl.BlockSpec((N_REC_LAYERS, 2, wb), lambda b, h: (0, 0, h))]
    in_specs += [pl.BlockSpec((t_len, wb), col(k)) for k in range(5)]
    in_specs += [pl.BlockSpec((1, wb), lambda b, h: (0, h))]
    args = [lb_logits, z, z, z, z, z, out_gain]
    if s0 is not None:
        in_specs.append(pl.BlockSpec((None, None, 2, sh, HG_DK, HG_DV),
                                     lambda b, h: (b, r_layer, 0, h, 0, 0)))
        args.append(s0)
    out_specs = [pl.BlockSpec((t_len, wb), lambda b, h: (b, h))]
    out_shape = [jax.ShapeDtypeStruct((nb * t_len, nh * HG_DV), BF16)]
    if emit_state:
        out_specs.append(pl.BlockSpec((None, 2, sh, HG_DK, HG_DV), lambda b, h: (b, 0, h, 0, 0)))
        out_shape.append(jax.ShapeDtypeStruct((nb, 2, nh, HG_DK, HG_DV), F32))
    kern = functools.partial(_scan_kernel, r_layer=r_layer, has_s0=s0 is not None, emit_state=emit_state,
                             n_heads=n_heads)
    return pl.pallas_call(
        kern,
        grid=(nb, ng),
        in_specs=in_specs,
        out_specs=out_specs,
        out_shape=out_shape,
        scratch_shapes=[pltpu.VMEM((t_len, wb), F32),
                        pltpu.VMEM((2, t_len, wb), F32),
                        pltpu.VMEM((2, sh, HG_DV, HG_DK), F32),
                        pltpu.VMEM((len(SCAN_LEVELS) + 2, SCAN_CHUNK, SCAN_CHUNK), F32)],
        name="hgrn_scan",
    )(*args)


def _one_hot_rows(posm, e0, g, cap):
    t_len = posm.shape[1]
    slot = lax.broadcasted_iota(jnp.int32, (cap, t_len), 0)
    parts = [jnp.where(posm[e:e + 1, :] == slot, 1.0, 0.0).astype(BF16) for e in range(e0, e0 + g)]
    return parts[0] if g == 1 else jnp.concatenate(parts, axis=0)


PREFIX_PIECE = 256


def _prefix_count(mask_f, before):
    pieces = []
    carry = jnp.zeros((mask_f.shape[0], 1), F32)
    for p0 in range(0, mask_f.shape[1], PREFIX_PIECE):
        piece = mask_f[:, p0:p0 + PREFIX_PIECE]
        pieces.append(_dot(piece.astype(BF16), before) + carry)
        carry = carry + jnp.sum(piece, axis=1, keepdims=True)
    return pieces[0] if len(pieces) == 1 else jnp.concatenate(pieces, axis=1)


def _sort_desc_segments(x, seg):
    w = x.shape[1]
    lane = lax.broadcasted_iota(jnp.int32, x.shape, 1)
    k = 2
    while k <= seg:
        j = k // 2
        while j >= 1:
            is_first = (lane & j) == 0
            partner = jnp.where(is_first, pltpu.roll(x, w - j, axis=1), pltpu.roll(x, j, axis=1))
            keep_max = is_first if k == seg else (is_first == ((lane & k) == 0))
            x = jnp.where(keep_max, jnp.maximum(x, partner), jnp.minimum(x, partner))
            j //= 2
        k *= 2
    return x


def _select_kernel(aff_ref, h_ref, g_ref, gate_ref, pos_ref, *, t_len):
    a = aff_ref[...]
    ne, w = a.shape
    cap = EC_CAPACITY * t_len // N_EXPERTS
    capf = jnp.float32(cap)
    srt = jnp.concatenate([_sort_desc_segments(a[r0:r0 + SUBLANES, :], t_len)
                           for r0 in range(0, ne, SUBLANES)], axis=0)
    src = lax.broadcasted_iota(jnp.int32, (PREFIX_PIECE, PREFIX_PIECE), 0)
    dst = lax.broadcasted_iota(jnp.int32, (PREFIX_PIECE, PREFIX_PIECE), 1)
    before = jnp.where(src < dst, 1.0, 0.0).astype(BF16)
    slot = lax.broadcasted_iota(jnp.int32, (cap, t_len), 0)
    g = max(1, min(ne, 512 // cap))
    for s in range(w // t_len):
        t0 = s * t_len
        c0 = s * cap
        seg = a[:, t0:t0 + t_len]
        thr = srt[:, t0 + cap - 1:t0 + cap]
        gt = jnp.where(seg > thr, 1.0, 0.0)
        eq = jnp.where(seg == thr, 1.0, 0.0)
        need = capf - jnp.sum(gt, axis=1, keepdims=True)
        eq_rank = _prefix_count(eq, before)
        sel = gt + eq * jnp.where(eq_rank < need, 1.0, 0.0)
        pos = _prefix_count(sel, before)
        posm = jnp.where(sel > 0.5, pos.astype(jnp.int32), -1)
        pos_ref[:, t0:t0 + t_len] = posm
        h = h_ref[t0:t0 + t_len, :]
        for e0 in range(0, ne, g):
            rows = _dot(_one_hot_rows(posm, e0, g, cap), h)
            g_ref[e0:e0 + g, c0:c0 + cap, :] = rows.reshape(g, cap, h.shape[1]).astype(g_ref.dtype)
        for e in range(ne):
            gate_ref[e, c0:c0 + cap, :] = jnp.sum(jnp.where(posm[e:e + 1, :] == slot, seg[e:e + 1, :], 0.0),
                                                  axis=1, keepdims=True)


def _select(aff_t, h, *, t_len):
    ne, n = aff_t.shape
    d = h.shape[1]
    w = SELECT_BLOCK
    capw = EC_CAPACITY * w // N_EXPERTS
    kern = functools.partial(_select_kernel, t_len=t_len)
    return pl.pallas_call(
        kern,
        grid=(n // w,),
        in_specs=[pl.BlockSpec((ne, w), lambda s: (0, s)),
                  pl.BlockSpec((w, d), lambda s: (s, 0))],
        out_specs=[pl.BlockSpec((ne, capw, d), lambda s: (0, s, 0)),
                   pl.BlockSpec((ne, capw, 1), lambda s: (0, s, 0)),
                   pl.BlockSpec((ne, w), lambda s: (0, s))],
        out_shape=[jax.ShapeDtypeStruct((ne, n // w * capw, d), BF16),
                   jax.ShapeDtypeStruct((ne, n // w * capw, 1), F32),
                   jax.ShapeDtypeStruct((ne, n), jnp.int32)],
        name="ec_select",
    )(aff_t, h)


def _ffn_kernel(xc_ref, xl_ref, gc_ref, gl_ref, wg_ref, wu_ref, wd_ref, yc_ref, yl_ref, accc, accl):
    f = pl.program_id(1)
    chunks = [(x_ref, acc, r0) for x_ref, acc in ((xc_ref, accc), (xl_ref, accl))
              for r0 in range(0, x_ref.shape[0], FFN_ROWS)]

    @pl.when(f == 0)
    def _():
        accc[...] = jnp.zeros_like(accc)
        accl[...] = jnp.zeros_like(accl)

    w = {}

    def weight(name, ref):
        if name not in w:
            w[name] = ref[...].astype(BF16)
        return w[name]

    def up(x_ref, acc, r0):
        x = x_ref[r0:r0 + FFN_ROWS, :]
        hg = _dot(x, weight("g", wg_ref))
        return hg, _dot(x, weight("u", wu_ref))

    def down(x_ref, acc, r0, hg, hu):
        hid = (hg * _sigmoid(hg) * hu).astype(BF16)
        acc[r0:r0 + FFN_ROWS, :] = acc[r0:r0 + FFN_ROWS, :] + _dot(hid, weight("d", wd_ref))

    pending = None
    for ch in chunks:
        cur = up(*ch)
        if pending is not None:
            down(*pending)
        pending = ch + cur
    down(*pending)

    @pl.when(f == pl.num_programs(1) - 1)
    def _():
        yc_ref[...] = (accc[...] * gc_ref[...]).astype(yc_ref.dtype)
        yl_ref[...] = (accl[...] * gl_ref[...]).astype(yl_ref.dtype)


def _ffn(xc, xl, gc, gl, w_gate, w_up, w_down, layer, *, bf):
    ne, nc, d = xc.shape
    nl = xl.shape[1]
    ff = w_gate.shape[3]
    per_e = lambda e, f: (e, 0, 0)
    return pl.pallas_call(
        _ffn_kernel,
        grid=(ne, ff // bf),
        in_specs=[pl.BlockSpec((None, nc, d), per_e),
                  pl.BlockSpec((None, nl, d), per_e),
                  pl.BlockSpec((None, nc, 1), per_e),
                  pl.BlockSpec((None, nl, 1), per_e),
                  pl.BlockSpec((None, None, d, bf), lambda e, f: (layer, e, 0, f)),
                  pl.BlockSpec((None, None, d, bf), lambda e, f: (layer, e, 0, f)),
                  pl.BlockSpec((None, None, bf, d), lambda e, f: (layer, e, f, 0))],
        out_specs=[pl.BlockSpec((None, nc, d), per_e),
                   pl.BlockSpec((None, nl, d), per_e)],
        out_shape=[jax.ShapeDtypeStruct((ne, nc, d), BF16),
                   jax.ShapeDtypeStruct((ne, nl, d), BF16)],
        scratch_shapes=[pltpu.VMEM((nc, d), F32), pltpu.VMEM((nl, d), F32)],
        name="expert_ffn",
    )(xc, xl, gc, gl, w_gate, w_up, w_down)


def _combine_kernel(pos_ref, y_ref, x_ref, m_ref, fn_ref, o_ref, *, cap):
    posm = pos_ref[...]
    ne = posm.shape[0]
    d = x_ref.shape[1]
    g = max(1, min(ne, 512 // cap))
    acc = jnp.zeros(x_ref.shape, F32)
    for e0 in range(0, ne, g):
        acc = acc + _dot_tn(_one_hot_rows(posm, e0, g, cap), y_ref[e0:e0 + g].reshape(g * cap, d))
    x = x_ref[...] + m_ref[5:6, :] * acc
    o_ref[...] = x if fn_ref is None else _rms(x, fn_ref[...])


def _combine(posm, y, x, mods, *, t_len, tt, base, final_gain=None):
    ne, n = posm.shape
    d = x.shape[1]
    ns = n // t_len
    cap = y.shape[1] // ns
    nt = t_len // tt
    in_specs = [pl.BlockSpec((ne, tt), lambda s, j: (0, s * nt + j)),
                pl.BlockSpec((ne, cap, d), lambda s, j: (0, s, 0)),
                pl.BlockSpec((tt, d), lambda s, j: (s * nt + j, 0)),
                pl.BlockSpec((None, 6, d), lambda s, j: (base(s), 0, 0))]
    args = [posm, y, x, mods]
    if final_gain is None:
        kern = lambda p, yr, xr, m, o: _combine_kernel(p, yr, xr, m, None, o, cap=cap)
    else:
        kern = functools.partial(_combine_kernel, cap=cap)
        in_specs.append(pl.BlockSpec((1, d), lambda s, j: (0, 0)))
        args.append(final_gain)
    return pl.pallas_call(
        kern,
        grid=(ns, nt),
        in_specs=in_specs,
        out_specs=pl.BlockSpec((tt, d), lambda s, j: (s * nt + j, 0)),
        out_shape=jax.ShapeDtypeStruct((n, d), F32),
        name="ec_combine",
    )(*args)


def _attn_in_weight(w):
    o_kr = MLA_Q_LORA + MLA_KV_LORA
    o_qg = o_kr + MLA_ROPE
    cols = jnp.concatenate([w[:, :o_kr], w[:, o_qg:], jnp.zeros((w.shape[0], KR_OFF), w.dtype), w[:, o_kr:o_qg]],
                           axis=1)
    return jnp.pad(cols, ((0, 0), (0, ATTN_Z - cols.shape[1]))).astype(BF16)


def _heads_split(w, n_heads, first, second):
    k = w.shape[0]
    w3 = w.reshape(k, n_heads, first + second)
    return jnp.concatenate([w3[:, :, :first].reshape(k, n_heads * first),
                            w3[:, :, first:].reshape(k, n_heads * second)], axis=1).astype(BF16)


def _head_slots(w, n_heads, lo, hi):
    k = w.shape[0]
    w3 = w.reshape(k, n_heads, -1)[:, :, lo:hi]
    return jnp.pad(w3, ((0, 0), (0, 0), (0, LANES - (hi - lo)))).reshape(k, n_heads * LANES)


def _mla_weights(w_uq, w_ukv):
    k = w_ukv.shape[0]
    v_cols = w_ukv.reshape(k, MLA_HEADS, MLA_NOPE + MLA_V)[:, :, MLA_NOPE:].reshape(k, VM_W)
    return (_head_slots(w_uq, MLA_HEADS, 0, MLA_NOPE + MLA_ROPE).astype(BF16),
            jnp.concatenate([_head_slots(w_ukv, MLA_HEADS, 0, MLA_NOPE), v_cols], axis=1).astype(BF16))


def _slot_tables(cos, sin):
    n = cos.shape[0]
    pad = LANES - KR_OFF - MLA_ROPE
    cos = jnp.concatenate([jnp.ones((n, KR_OFF), F32), cos, jnp.ones((n, pad), F32)], axis=1)
    sin = jnp.concatenate([jnp.zeros((n, KR_OFF), F32), sin, jnp.zeros((n, pad), F32)], axis=1)
    return jnp.tile(cos, (1, MLA_HEADS)), jnp.tile(sin, (1, MLA_HEADS))


def _rope_tables(width, n_rep):
    half = width // 2
    t = jnp.arange(DEC_SEQ)
    rows = (t // GRID_W).astype(F32)
    cols = (t % GRID_W).astype(F32)
    inv_freq = ROPE_BASE ** (-jnp.arange(0, half, 2, dtype=F32) / half)
    ar = rows[:, None] * inv_freq
    ac = cols[:, None] * inv_freq
    cos = jnp.concatenate([jnp.cos(ar), jnp.cos(ar), jnp.cos(ac), jnp.cos(ac)], axis=1)
    sin = jnp.concatenate([-jnp.sin(ar), jnp.sin(ar), -jnp.sin(ac), jnp.sin(ac)], axis=1)
    return jnp.tile(cos, (1, n_rep)), jnp.tile(sin, (1, n_rep))


def kernel(x_prompt, x_sample, c, cache_mla_ckv, cache_mla_krope, cache_gqa_k, cache_gqa_v, state_hgrn,
           c_ctx, norm1, norm2, w_mod, b_mod, w_attn_in, mla_q_norm, w_mla_uq, mla_kv_norm, w_mla_ukv,
           gqa_sink, w_attn_out, w_rec_in, rec_lb_logits, rec_out_norm, w_rec_out, w_router, w_gate,
           w_up, w_down, final_norm):
    d = D_MODEL
    xc = x_prompt.reshape(BATCH * SEQ, d)
    xl = x_sample.reshape(DEC_BATCH * DEC_SEQ, d)
    cvecs = jnp.concatenate([c_ctx[None, :], c, jnp.zeros((N_MOD_ROWS - N_MOD_GROUPS, d), F32)], axis=0)
    mods_all = _mod_vectors(cvecs, w_mod, b_mod).reshape(DEPTH, N_MOD_ROWS, 6, d)[:, :N_MOD_GROUPS]
    grp_c = _group_map(None, 0, 0)
    cm, sm = _slot_tables(*_rope_tables(MLA_ROPE, 1))
    cg, sg = _rope_tables(GQA_HD, GQA_HEADS)
    kvw = GQA_KV_HEADS * GQA_HD

    ckv_l, kr_l, kg_l, vg_l, st_l = [], [], [], [], []
    for layer in range(DEPTH):
        mods = mods_all[layer]
        g1 = norm1[layer].reshape(1, d)
        g2 = norm2[layer].reshape(1, d)
        if layer % 2 == 0:
            a = layer // 2
            w_in = _attn_in_weight(w_attn_in[a])
            w_uq, w_ukv = _mla_weights(w_mla_uq[a], w_mla_ukv[a])
            w_out = w_attn_out[a].astype(BF16)
            qn_g = mla_q_norm[a].reshape(1, -1)
            kvn_g = mla_kv_norm[a].reshape(1, -1)
            sink = gqa_sink[a]
            zc = _modproj(xc, g1, mods, w_in, sub=0, bm=512, grp=grp_c)
            zl = _modproj(xl, g1, mods, w_in, sub=0, bm=512, grp=_group_map(DEC_SEQ, 512, 1))
            oc, ckv_n, k_rope, k_g, v_g = _attn_ctx(zc, sink, qn_g,
                                                    _heads_split(w_mla_uq[a], MLA_HEADS, MLA_NOPE, MLA_ROPE), kvn_g,
                                                    _heads_split(w_mla_ukv[a], MLA_HEADS, MLA_NOPE, MLA_V))
            ckv_l.append(ckv_n.reshape(BATCH, SEQ, MLA_KV_LORA))
            kr_l.append(k_rope.reshape(BATCH, SEQ, MLA_ROPE))
            kg_l.append(k_g.reshape(BATCH, SEQ, GQA_KV_HEADS, GQA_HD))
            vg_l.append(v_g.reshape(BATCH, SEQ, GQA_KV_HEADS, GQA_HD))
            qcat, kcat, vm, qg, kg, vg = _lat_prep(zl, qn_g, w_uq, kvn_g, w_ukv, cm, sm, cg, sg, bm=256)
            kr_c = jnp.pad(cache_mla_krope[:, a], ((0, 0), (0, 0), (KR_OFF, LANES - KR_OFF - MLA_ROPE)))
            kcat_c, vm_c = _cache_kv(cache_mla_ckv, kr_c, a, w_ukv)
            per_b = lambda x: x.reshape(DEC_BATCH, DEC_SEQ, -1)
            cat = lambda lat, ctx: jnp.concatenate([per_b(lat), ctx], axis=1)
            band = lambda x: jnp.pad(per_b(x), ((0, 0), (BAND_BLOCK, BAND_BLOCK), (0, 0)))
            ol = _lat_attn(sink, qcat, cat(kcat, kcat_c), cat(vm, vm_c), qg, band(kg), band(vg),
                           cache_gqa_k[:, a].reshape(DEC_BATCH, PAST_LEN, kvw).astype(BF16),
                           cache_gqa_v[:, a].reshape(DEC_BATCH, PAST_LEN, kvw).astype(BF16), qb=LAT_QB)
        else:
            r = layer // 2
            w_in = w_rec_in[r].astype(BF16)
            w_out = w_rec_out[r].astype(BF16)
            og = rec_out_norm[r].reshape(1, -1)
            zc = _modproj(xc, g1, mods, w_in, sub=0, bm=256, grp=grp_c)
            zl = _modproj(xl, g1, mods, w_in, sub=0, bm=256, grp=_group_map(DEC_SEQ, 256, 1))
            oc, st = _scan(zc, rec_lb_logits, og, r, t_len=SEQ, n_heads=4, emit_state=True)
            (ol,) = _scan(zl, rec_lb_logits, og, r, t_len=DEC_SEQ, n_heads=2, s0=state_hgrn)
            st_l.append(st)
        wr_t = w_router[layer].T.astype(BF16)
        xc, hc, aff_c = _outproj_router(oc, w_out, xc, mods, g2, wr_t, bm=512, grp=grp_c)
        xl, hl, aff_l = _outproj_router(ol, w_out, xl, mods, g2, wr_t, bm=512, grp=_group_map(DEC_SEQ, 512, 1))
        gxc, gate_c, pos_c = _select(aff_c, hc, t_len=SEQ)
        gxl, gate_l, pos_l = _select(aff_l, hl, t_len=DEC_SEQ)
        yc, yl = _ffn(gxc, gxl, gate_c, gate_l, w_gate, w_up, w_down, layer, bf=512)
        fn = final_norm.reshape(1, d) if layer == DEPTH - 1 else None
        xc = _combine(pos_c, yc, xc, mods, t_len=SEQ, tt=SEQ, base=lambda s: 0, final_gain=fn)
        xl = _combine(pos_l, yl, xl, mods, t_len=DEC_SEQ, tt=512, base=lambda s: 1 + s, final_gain=fn)

    y_prompt = xc.reshape(BATCH, SEQ, d)
    y_sample = xl.reshape(DEC_BATCH, DEC_SEQ, d)
    return (y_prompt, y_sample, jnp.stack(ckv_l, axis=1), jnp.stack(kr_l, axis=1),
            jnp.stack(kg_l, axis=1), jnp.stack(vg_l, axis=1), jnp.stack(st_l, axis=1))
```

```python
import functools

import jax
import jax.numpy as jnp
from jax import lax
from jax.experimental import pallas as pl
from jax.experimental.pallas import tpu as pltpu

F32 = jnp.float32
BF16 = jnp.bfloat16

D_MODEL = 1024
BATCH = 32
SEQ = 256
DEPTH = 4
DEC_BATCH = 2
DEC_SEQ = 2048
PAST_LEN = 512
GRID_W = 64
ROPE_BASE = 10000.0
NORM_EPS = 1e-6
NEG_BIG = -1e30
F_MIN = 1e-6
N_ATTN_LAYERS = (DEPTH + 1) // 2
N_REC_LAYERS = DEPTH // 2
MLA_HEADS = 8
MLA_Q_LORA = 384
MLA_KV_LORA = 256
MLA_NOPE = 64
MLA_ROPE = 32
MLA_V = 64
MLA_SCALE = (MLA_NOPE + MLA_ROPE) ** -0.5
GQA_HEADS = 8
GQA_KV_HEADS = 2
GQA_GROUP = GQA_HEADS // GQA_KV_HEADS
GQA_HD = 64
GQA_SCALE = GQA_HD ** -0.5
WINDOW = 128
BAND_BLOCK = 128
HG_HEADS = 8
HG_DK = 128
HG_DV = 128
REC_IN = 3 * HG_HEADS * HG_DK + 2 * HG_HEADS * HG_DV
N_EXPERTS = 16
EC_CAPACITY = 2
D_FF = 2048

LANES = 128
N_MOD_ROWS = 8
N_MOD_GROUPS = 1 + DEC_BATCH

ZC_Q = 0
ZC_KV = ZC_Q + MLA_Q_LORA
ZC_QG = ZC_KV + MLA_KV_LORA
ZC_KG = ZC_QG + GQA_HEADS * GQA_HD
ZC_VG = ZC_KG + GQA_KV_HEADS * GQA_HD
ZC_KR = ZC_VG + GQA_KV_HEADS * GQA_HD
ATTN_Z = ZC_KR + LANES
KR_OFF = MLA_NOPE
QK_W = MLA_HEADS * LANES
VM_W = MLA_HEADS * MLA_V
O_M_W = MLA_HEADS * MLA_V
O_G_W = GQA_HEADS * GQA_HD

SUBLANES = 8
SCAN_CHUNK = 128
SCAN_LEVELS = (64, 32, 16, 8)
SCAN_DIAG = 8
DIAG_CLAMP = 88.0
SELECT_BLOCK = 2048
FFN_ROWS = 256
LAT_QB = 512
CTX_REQS = 2
SCAN_SINGLE_BUFFER_BYTES = 4 << 20


def _dot(a, b):
    return jnp.dot(a, b, preferred_element_type=F32)


def _dot_nt(a, b):
    return lax.dot_general(a, b, (((1,), (1,)), ((), ())), preferred_element_type=F32)


def _dot_tn(a, b):
    return lax.dot_general(a, b, (((0,), (0,)), ((), ())), preferred_element_type=F32)


def _sigmoid(x):
    return 1.0 / (1.0 + jnp.exp(-x))


def _rms(x, gain):
    return x * lax.rsqrt(jnp.mean(x * x, axis=-1, keepdims=True) + NORM_EPS) * gain


def _mod_kernel(c_ref, w_ref, b_ref, o_ref):
    c = c_ref[...]
    s = (c * _sigmoid(c)).astype(BF16)
    o_ref[...] = _dot(s, w_ref[...].astype(BF16)) + b_ref[...]


def _mod_vectors(cvecs, w_mod, b_mod):
    bn = 1536
    nd = 6 * D_MODEL
    return pl.pallas_call(
        _mod_kernel,
        grid=(DEPTH, nd // bn),
        in_specs=[pl.BlockSpec((N_MOD_ROWS, D_MODEL), lambda l, j: (0, 0)),
                  pl.BlockSpec((None, D_MODEL, bn), lambda l, j: (l, 0, j)),
                  pl.BlockSpec((None, 1, bn), lambda l, j: (l, 0, j))],
        out_specs=pl.BlockSpec((None, N_MOD_ROWS, bn), lambda l, j: (l, 0, j)),
        out_shape=jax.ShapeDtypeStruct((DEPTH, N_MOD_ROWS, nd), F32),
        name="mod_vectors",
    )(cvecs, w_mod, b_mod.reshape(DEPTH, 1, nd))


def _group_map(rows_per_group, bm, base):
    if rows_per_group is None:
        return lambda i: base
    return lambda i: base + (i * bm) // rows_per_group


def _modproj_kernel(x_ref, g_ref, m_ref, w_ref, o_ref, *, shift_row, chunk):
    h = _rms(x_ref[...], g_ref[...])
    h = (h * (1.0 + m_ref[shift_row + 1:shift_row + 2, :]) + m_ref[shift_row:shift_row + 1, :]).astype(BF16)
    nout = o_ref.shape[1]
    for j in range(0, nout, chunk):
        o_ref[:, j:j + chunk] = _dot(h, w_ref[:, j:j + chunk])


def _modproj(x, gain, mods, w, *, sub, bm, grp):
    n, d = x.shape
    nout = w.shape[1]
    chunk = nout if nout <= 1536 else 1024
    kern = functools.partial(_modproj_kernel, shift_row=3 * sub, chunk=chunk)
    return pl.pallas_call(
        kern,
        grid=(n // bm,),
        in_specs=[pl.BlockSpec((bm, d), lambda i: (i, 0)),
                  pl.BlockSpec((1, d), lambda i: (0, 0)),
                  pl.BlockSpec((None, 6, d), lambda i: (grp(i), 0, 0)),
                  pl.BlockSpec((d, nout), lambda i: (0, 0))],
        out_specs=pl.BlockSpec((bm, nout), lambda i: (i, 0)),
        out_shape=jax.ShapeDtypeStruct((n, nout), F32),
        name="modproj",
    )(x, gain, mods, w)


def _outproj_router_kernel(a_ref, w_ref, x_ref, m_ref, g_ref, wr_ref, x_out, h_out, aff_out):
    x = x_ref[...] + m_ref[2:3, :] * _dot(a_ref[...], w_ref[...])
    x_out[...] = x
    h = (_rms(x, g_ref[...]) * (1.0 + m_ref[4:5, :]) + m_ref[3:4, :]).astype(BF16)
    h_out[...] = h
    logits = _dot_nt(wr_ref[...], h)
    e = jnp.exp(logits - jnp.max(logits, axis=0, keepdims=True))
    aff_out[...] = e / jnp.sum(e, axis=0, keepdims=True)


def _outproj_router(a, w, x, mods, gain2, wr_t, *, bm, grp):
    n, d = x.shape
    k = a.shape[1]
    row = lambda i: (i, 0)
    const = lambda i: (0, 0)
    return pl.pallas_call(
        _outproj_router_kernel,
        grid=(n // bm,),
        in_specs=[pl.BlockSpec((bm, k), row),
                  pl.BlockSpec((k, d), const),
                  pl.BlockSpec((bm, d), row),
                  pl.BlockSpec((None, 6, d), lambda i: (grp(i), 0, 0)),
                  pl.BlockSpec((1, d), const),
                  pl.BlockSpec((N_EXPERTS, d), const)],
        out_specs=[pl.BlockSpec((bm, d), row),
                   pl.BlockSpec((bm, d), row),
                   pl.BlockSpec((N_EXPERTS, bm), lambda i: (0, i))],
        out_shape=[jax.ShapeDtypeStruct((n, d), F32),
                   jax.ShapeDtypeStruct((n, d), BF16),
                   jax.ShapeDtypeStruct((N_EXPERTS, n), F32)],
        name="outproj_router",
    )(a, w, x, mods, gain2, wr_t)


def _softmax_pv(s, v, sink=None):
    m = jnp.max(s, axis=-1, keepdims=True)
    if sink is not None:
        m = jnp.maximum(m, sink)
    p = jnp.exp(s - m)
    l = jnp.sum(p, axis=-1, keepdims=True)
    if sink is not None:
        l = l + jnp.exp(sink - m)
    return _dot(p.astype(BF16), v) / l


def _swap_pairs(x, q):
    w = x.shape[1]
    lane = lax.broadcasted_iota(jnp.int32, x.shape, 1)
    first = (lane % (2 * q)) < q
    return jnp.where(first, pltpu.roll(x, w - q, axis=1), pltpu.roll(x, q, axis=1))


def _rope(x, cos, sin, q):
    return x * cos + _swap_pairs(x, q) * sin


def _attn_project(z, qn_ref, wuq_ref, kvn_ref, wukv_ref):
    cq = _rms(z[:, ZC_Q:ZC_Q + MLA_Q_LORA], qn_ref[...]).astype(BF16)
    qcat = _dot(cq, wuq_ref[...])
    ckv_n = _rms(z[:, ZC_KV:ZC_KV + MLA_KV_LORA], kvn_ref[...])
    kv = _dot(ckv_n.astype(BF16), wukv_ref[...])
    return qcat, ckv_n, kv


def _tile_heads(kr_slot):
    return jnp.concatenate([kr_slot] * MLA_HEADS, axis=1)


def _staggered(n, first, second, depth):
    pending = []
    for i in range(n):
        pending.append((i, first(i)))
        if len(pending) > depth:
            second(*pending.pop(0))
    for item in pending:
        second(*item)


def _gqa_sink_column(sink_ref, g, rows):
    return jnp.concatenate([jnp.full((rows, 1), sink_ref[g * GQA_GROUP + j], F32) for j in range(GQA_GROUP)], axis=0)


def _stack_group(qg, g):
    return jnp.concatenate([qg[:, (g * GQA_GROUP + j) * GQA_HD:(g * GQA_GROUP + j + 1) * GQA_HD]
                            for j in range(GQA_GROUP)], axis=0)


def _attn_ctx_kernel(*refs, n_alias, a):
    sink_ref, z_ref, qn_ref, wuq_ref, kvn_ref, wukv_ref = refs[:6]
    o_ref, ckv_ref, kr_ref, kg_ref, vg_ref = refs[6 + n_alias:]
    t_len = SEQ
    z = z_ref[...]
    cq = _rms(z[:, ZC_Q:ZC_Q + MLA_Q_LORA], qn_ref[...]).astype(BF16)
    qm = _dot(cq, wuq_ref[...])
    ckv_n = _rms(z[:, ZC_KV:ZC_KV + MLA_KV_LORA], kvn_ref[...])
    kv = _dot(ckv_n.astype(BF16), wukv_ref[...])
    kr = z[:, ZC_KR + KR_OFF:ZC_KR + KR_OFF + MLA_ROPE]
    kg = z[:, ZC_KG:ZC_KG + GQA_KV_HEADS * GQA_HD]
    vg = z[:, ZC_VG:ZC_VG + GQA_KV_HEADS * GQA_HD]
    reqs = [slice(r * t_len, (r + 1) * t_len) for r in range(CTX_REQS)]
    for ref, val in ((ckv_ref, ckv_n), (kr_ref, kr), (kg_ref, kg), (vg_ref, vg)):
        for r, rows in enumerate(reqs):
            if n_alias:
                ref[r] = val[rows]
            else:
                for layer in range(N_ATTN_LAYERS):
                    ref[r, layer] = val[rows] if layer == a else jnp.zeros_like(val[rows])
    qn_w = MLA_HEADS * MLA_NOPE
    qn = (qm[:, :qn_w] * MLA_SCALE).astype(BF16)
    qr = (qm[:, qn_w:] * MLA_SCALE).astype(BF16)
    kn = kv[:, :qn_w].astype(BF16)
    vm = kv[:, qn_w:].astype(BF16)
    krb = kr.astype(BF16)
    for h in range(MLA_HEADS):
        nope = slice(h * MLA_NOPE, (h + 1) * MLA_NOPE)
        rope = slice(h * MLA_ROPE, (h + 1) * MLA_ROPE)
        ss = [_dot_nt(qn[rows, nope], kn[rows, nope]) + _dot_nt(qr[rows, rope], krb[rows]) for rows in reqs]
        for rows, s in zip(reqs, ss):
            o = _softmax_pv(s, vm[rows, h * MLA_V:(h + 1) * MLA_V])
            o_ref[rows, h * MLA_V:(h + 1) * MLA_V] = o.astype(o_ref.dtype)
    qg = (z[:, ZC_QG:ZC_QG + O_G_W] * GQA_SCALE).astype(BF16)
    kgb = kg.astype(BF16)
    vgb = vg.astype(BF16)
    for h in range(GQA_HEADS):
        g = h // GQA_GROUP
        kvs = slice(g * GQA_HD, (g + 1) * GQA_HD)
        ss = [_dot_nt(qg[rows, h * GQA_HD:(h + 1) * GQA_HD], kgb[rows, kvs]) for rows in reqs]
        for rows, s in zip(reqs, ss):
            o = _softmax_pv(s, vgb[rows, kvs], sink_ref[h])
            o_ref[rows, O_M_W + h * GQA_HD:O_M_W + (h + 1) * GQA_HD] = o.astype(o_ref.dtype)


def _attn_ctx(z, sink, q_norm, w_uq, kv_norm, w_ukv, a, caches=None):
    nb = z.shape[0] // SEQ
    rq = CTX_REQS
    kvw = GQA_KV_HEADS * GQA_HD
    const = lambda b: (0, 0)
    row = lambda b: (b, 0)
    cache_w = (MLA_KV_LORA, MLA_ROPE, kvw, kvw)
    in_specs = [pl.BlockSpec(memory_space=pltpu.SMEM),
                pl.BlockSpec((rq * SEQ, ATTN_Z), row),
                pl.BlockSpec((1, MLA_Q_LORA), const),
                pl.BlockSpec(w_uq.shape, const),
                pl.BlockSpec((1, MLA_KV_LORA), const),
                pl.BlockSpec(w_ukv.shape, const)]
    args = [sink, z, q_norm, w_uq, kv_norm, w_ukv]
    aliases = {}
    if caches is not None:
        for j, c in enumerate(caches):
            aliases[len(args)] = 1 + j
            in_specs.append(pl.BlockSpec(memory_space=pl.ANY))
            args.append(c)
    kern = functools.partial(_attn_ctx_kernel, n_alias=len(aliases), a=a)
    if aliases:
        cache_specs = [pl.BlockSpec((rq, None, SEQ, w), lambda b: (b, a, 0, 0)) for w in cache_w]
    else:
        cache_specs = [pl.BlockSpec((rq, N_ATTN_LAYERS, SEQ, w), lambda b: (b, 0, 0, 0)) for w in cache_w]
    return pl.pallas_call(
        kern,
        grid=(nb // rq,),
        in_specs=in_specs,
        out_specs=[pl.BlockSpec((rq * SEQ, O_M_W + O_G_W), row)] + cache_specs,
        out_shape=[jax.ShapeDtypeStruct((nb * SEQ, O_M_W + O_G_W), BF16)]
        + [jax.ShapeDtypeStruct((nb, N_ATTN_LAYERS, SEQ, w), F32) for w in cache_w],
        input_output_aliases=aliases,
        name="attn_ctx",
    )(*args)


def _lat_prep_kernel(z_ref, qn_ref, wuq_ref, kvn_ref, wukv_ref, cm_ref, sm_ref, cg_ref, sg_ref,
                     qcat_o, kcat_o, vm_o, qg_o, kg_o, vg_o):
    z = z_ref[...]
    qcat, ckv_n, kv = _attn_project(z, qn_ref, wuq_ref, kvn_ref, wukv_ref)
    cm = cm_ref[...]
    sm = sm_ref[...]
    cg = cg_ref[...]
    sg = sg_ref[...]
    qcat_o[...] = (_rope(qcat, cm, sm, MLA_ROPE // 4) * MLA_SCALE).astype(BF16)
    kr_slot = _rope(z[:, ZC_KR:ZC_KR + LANES], cm[:, :LANES], sm[:, :LANES], MLA_ROPE // 4)
    kcat_o[...] = (kv[:, :QK_W] + _tile_heads(kr_slot)).astype(BF16)
    vm_o[...] = kv[:, QK_W:].astype(BF16)
    qg_o[...] = (_rope(z[:, ZC_QG:ZC_QG + O_G_W], cg, sg, GQA_HD // 4) * GQA_SCALE).astype(BF16)
    kvw = GQA_KV_HEADS * GQA_HD
    kg_o[...] = _rope(z[:, ZC_KG:ZC_KG + kvw], cg[:, :kvw], sg[:, :kvw], GQA_HD // 4).astype(BF16)
    vg_o[...] = z[:, ZC_VG:ZC_VG + kvw].astype(BF16)


def _lat_prep(z, q_norm, w_uq, kv_norm, w_ukv, cm, sm, cg, sg, *, bm):
    n = z.shape[0]
    per = DEC_SEQ // bm
    kvw = GQA_KV_HEADS * GQA_HD
    const = lambda i: (0, 0)
    row = lambda i: (i, 0)
    pos = lambda i: (i % per, 0)
    widths = (QK_W, QK_W, VM_W, O_G_W, kvw, kvw)
    return pl.pallas_call(
        _lat_prep_kernel,
        grid=(n // bm,),
        in_specs=[pl.BlockSpec((bm, ATTN_Z), row),
                  pl.BlockSpec((1, MLA_Q_LORA), const),
                  pl.BlockSpec(w_uq.shape, const),
                  pl.BlockSpec((1, MLA_KV_LORA), const),
                  pl.BlockSpec(w_ukv.shape, const),
                  pl.BlockSpec((bm, QK_W), pos),
                  pl.BlockSpec((bm, QK_W), pos),
                  pl.BlockSpec((bm, O_G_W), pos),
                  pl.BlockSpec((bm, O_G_W), pos)],
        out_specs=[pl.BlockSpec((bm, w), row) for w in widths],
        out_shape=[jax.ShapeDtypeStruct((n, w), BF16) for w in widths],
        name="lat_prep",
    )(z, q_norm, w_uq, kv_norm, w_ukv, cm, sm, cg, sg)


def _cache_kv_kernel(c_ref, kr_ref, w_ref, kcat_o, vm_o):
    kv = _dot(c_ref[...].astype(BF16), w_ref[...])
    kcat_o[...] = (kv[:, :QK_W] + _tile_heads(kr_ref[...])).astype(BF16)
    vm_o[...] = kv[:, QK_W:].astype(BF16)


def _cache_kv(cache_ckv, kr_slot, a, w_ukv):
    nb = cache_ckv.shape[0]
    return pl.pallas_call(
        _cache_kv_kernel,
        grid=(nb,),
        in_specs=[pl.BlockSpec((None, None, PAST_LEN, MLA_KV_LORA), lambda b: (b, a, 0, 0)),
                  pl.BlockSpec((None, PAST_LEN, LANES), lambda b: (b, 0, 0)),
                  pl.BlockSpec(w_ukv.shape, lambda b: (0, 0))],
        out_specs=[pl.BlockSpec((None, PAST_LEN, QK_W), lambda b: (b, 0, 0)),
                   pl.BlockSpec((None, PAST_LEN, VM_W), lambda b: (b, 0, 0))],
        out_shape=[jax.ShapeDtypeStruct((nb, PAST_LEN, QK_W), BF16),
                   jax.ShapeDtypeStruct((nb, PAST_LEN, VM_W), BF16)],
        name="cache_kv",
    )(cache_ckv, kr_slot, w_ukv)


def _lat_attn_kernel(sink_ref, qcat_ref, kcat_ref, vm_ref, qg_ref, kg_ref, vg_ref, kctx_ref, vctx_ref, o_ref):
    n = pl.program_id(1)
    qcat = qcat_ref[...]

    def mla_scores(h):
        return _dot_nt(qcat[:, h * LANES:(h + 1) * LANES], kcat_ref[:, h * LANES:(h + 1) * LANES])

    def mla_out(h, s):
        o_ref[:, h * MLA_V:(h + 1) * MLA_V] = _softmax_pv(s, vm_ref[:, h * MLA_V:(h + 1) * MLA_V]).astype(o_ref.dtype)

    _staggered(MLA_HEADS, mla_scores, mla_out, depth=2)

    bb = BAND_BLOCK
    nsub = qg_ref.shape[0] // bb
    rows = GQA_GROUP * bb
    tq = lax.broadcasted_iota(jnp.int32, (rows, 3 * bb), 0) & (bb - 1)
    kk = lax.broadcasted_iota(jnp.int32, (rows, 3 * bb), 1)
    qg = qg_ref[...]
    kctx = kctx_ref[...]
    vctx = vctx_ref[...]

    def gqa_scores(i):
        j, g = divmod(i, GQA_KV_HEADS)
        hs = slice(g * GQA_HD, (g + 1) * GQA_HD)
        blk = n * nsub + j
        start = pl.multiple_of(blk * bb, bb)
        lo = jnp.maximum(tq, (1 - blk) * bb)
        hi = jnp.minimum(tq + 2 * WINDOW, (DEC_SEQ // bb + 1 - blk) * bb - 1)
        q = _stack_group(qg[j * bb:(j + 1) * bb, :], g)
        s_loc = jnp.where((kk >= lo) & (kk <= hi), _dot_nt(q, kg_ref[pl.ds(start, 3 * bb), hs]), NEG_BIG)
        return s_loc, _dot_nt(q, kctx[:, hs]), start

    def gqa_out(i, res):
        j, g = divmod(i, GQA_KV_HEADS)
        hs = slice(g * GQA_HD, (g + 1) * GQA_HD)
        s_loc, s_ctx, start = res
        sink = _gqa_sink_column(sink_ref, g, bb)
        m = jnp.maximum(jnp.maximum(jnp.max(s_loc, axis=-1, keepdims=True),
                                    jnp.max(s_ctx, axis=-1, keepdims=True)), sink)
        p_loc = jnp.exp(s_loc - m)
        p_ctx = jnp.exp(s_ctx - m)
        l = (jnp.sum(p_loc, axis=-1, keepdims=True) + jnp.sum(p_ctx, axis=-1, keepdims=True)
             + jnp.exp(sink - m))
        o = (_dot(p_loc.astype(BF16), vg_ref[pl.ds(start, 3 * bb), hs]) + _dot(p_ctx.astype(BF16), vctx[:, hs])) / l
        for jj in range(GQA_GROUP):
            h = g * GQA_GROUP + jj
            o_ref[j * bb:(j + 1) * bb, O_M_W + h * GQA_HD:O_M_W + (h + 1) * GQA_HD] = (
                o[jj * bb:(jj + 1) * bb].astype(o_ref.dtype))

    _staggered(nsub * GQA_KV_HEADS, gqa_scores, gqa_out, depth=3)


def _lat_attn(sink, qcat, kcat_all, vm_all, qg, kg_pad, vg_pad, kg_ctx, vg_ctx, *, qb):
    nb = DEC_BATCH
    nq = DEC_SEQ // qb
    tk = kcat_all.shape[1]
    tp = kg_pad.shape[1]
    kvw = GQA_KV_HEADS * GQA_HD
    qrow = lambda b, n: (b * nq + n, 0)
    per_b = lambda b, n: (b, 0, 0)
    return pl.pallas_call(
        _lat_attn_kernel,
        grid=(nb, nq),
        in_specs=[pl.BlockSpec(memory_space=pltpu.SMEM),
                  pl.BlockSpec((qb, QK_W), qrow),
                  pl.BlockSpec((None, tk, QK_W), per_b),
                  pl.BlockSpec((None, tk, VM_W), per_b),
                  pl.BlockSpec((qb, O_G_W), qrow),
                  pl.BlockSpec((None, tp, kvw), per_b),
                  pl.BlockSpec((None, tp, kvw), per_b),
                  pl.BlockSpec((None, PAST_LEN, kvw), per_b),
                  pl.BlockSpec((None, PAST_LEN, kvw), per_b)],
        out_specs=pl.BlockSpec((qb, O_M_W + O_G_W), qrow),
        out_shape=jax.ShapeDtypeStruct((nb * DEC_SEQ, O_M_W + O_G_W), BF16),
        name="lat_attn",
    )(sink, qcat, kcat_all, vm_all, qg, kg_pad, vg_pad, kg_ctx, vg_ctx)


def _row_tiles(x):
    return [x[i * SUBLANES:(i + 1) * SUBLANES, :] for i in range(x.shape[0] // SUBLANES)]


def _scan_masks(mask_ref):
    c = SCAN_CHUNK
    t = lax.broadcasted_iota(jnp.int32, (c, c), 0)
    s = lax.broadcasted_iota(jnp.int32, (c, c), 1)
    for li, m in enumerate(SCAN_LEVELS):
        sh = m.bit_length() - 1
        mask_ref[li] = jnp.where((t >> sh) == (s >> sh), 1.0, 0.0)
    sh = SCAN_DIAG.bit_length() - 1
    same = (t >> sh) == (s >> sh)
    mask_ref[len(SCAN_LEVELS)] = jnp.where(same & (s <= t), 1.0, 0.0)
    mask_ref[len(SCAN_LEVELS) + 1] = jnp.where(same & (s >= t), 1.0, 0.0)


def _level_rows(m, rev):
    mt = m // SUBLANES
    nt = SCAN_CHUNK // SUBLANES
    groups = []
    for g in range(nt // (2 * mt)):
        first = list(range(g * 2 * mt, g * 2 * mt + mt))
        second = list(range(g * 2 * mt + mt, (g + 1) * 2 * mt))
        if rev:
            groups.append((first, second, second[0], 0))
        else:
            groups.append((second, first, first[-1], SUBLANES - 1))
    return groups


def _scan_chunks(chains, mask_ref):
    c = SCAN_CHUNK
    half = c // 2
    nt = c // SUBLANES
    n_lv = len(SCAN_LEVELS)
    work = []
    for q, zf, v, lb, st, tri, rev in chains:
        f = lb + (1.0 - lb) * _sigmoid(zf)
        lf = jnp.log2(jnp.maximum(f, F_MIN))
        hi = lf.astype(BF16)
        mid = (lf - hi.astype(F32)).astype(BF16)
        work.append(dict(q=q, k=1.0 - f, v=v, vb=v.astype(BF16), st=st, tri=tri, rev=rev, hi=hi, mid=mid))
    for w in work:
        w["b"] = _dot(w["tri"], w["hi"]) + _dot(w["tri"], w["mid"])
        w["qt"], w["kt"], w["bt"], w["vt"] = (_row_tiles(w[n]) for n in ("q", "k", "b", "v"))
    for w in work:
        mid_row = SCAN_DIAG // 2 if w["rev"] else SCAN_DIAG // 2 - 1
        e = jnp.concatenate([jnp.clip(t - t[mid_row:mid_row + 1, :], -DIAG_CLAMP, DIAG_CLAMP) for t in w["bt"]],
                            axis=0)
        a = _dot_nt((w["q"] * jnp.exp2(e)).astype(BF16), (w["k"] * jnp.exp2(-e)).astype(BF16))
        w["a"] = jnp.where(mask_ref[n_lv + (1 if w["rev"] else 0)] > 0.5, a, 0.0).astype(BF16)
    for w in work:
        o = _dot(w["a"], w["vb"]) + _dot_nt((w["q"] * jnp.exp2(w["b"])).astype(BF16), w["st"].astype(BF16))
        w["o"] = _row_tiles(o)
    for li, m in enumerate(SCAN_LEVELS):
        for w in work:
            q_parts, k_parts, w["q_idx"], w["k_idx"] = [], [], [], []
            for q_rows, k_rows, rt, rr in _level_rows(m, w["rev"]):
                r = w["bt"][rt][rr:rr + 1, :]
                q_parts += [w["qt"][i] * jnp.exp2(w["bt"][i] - r) for i in q_rows]
                k_parts += [w["kt"][i] * jnp.exp2(r - w["bt"][i]) for i in k_rows]
                w["q_idx"] += q_rows
                w["k_idx"] += k_rows
            al = _dot_nt(jnp.concatenate(q_parts, axis=0).astype(BF16),
                         jnp.concatenate(k_parts, axis=0).astype(BF16))
            if m != half:
                al = al * mask_ref[li, :half, :half]
            w["al"] = al.astype(BF16)
        for w in work:
            vl = jnp.concatenate([w["vt"][i] for i in w["k_idx"]], axis=0).astype(BF16)
            ol = _row_tiles(_dot(w["al"], vl))
            for j, i in enumerate(w["q_idx"]):
                w["o"][i] = w["o"][i] + ol[j]
    outs = []
    for w in work:
        b_tot = w["bt"][0][0:1, :] if w["rev"] else w["bt"][nt - 1][SUBLANES - 1:SUBLANES, :]
        k_end = (w["k"] * jnp.exp2(b_tot - w["b"])).astype(BF16)
        st_new = w["st"] * jnp.exp2(b_tot) + _dot_tn(w["vb"], k_end)
        outs.append((jnp.concatenate(w["o"], axis=0), st_new))
    return outs


def _scan_kernel(*refs, r_layer, has_s0, emit_state, has_alias, n_heads):
    lbl_ref, zq_ref, zff_ref, zfb_ref, zi_ref, zg_ref, gain_ref = refs[:7]
    pos = 7
    s0_ref = None
    if has_s0:
        s0_ref = refs[pos]
        pos += 1
    if has_alias:
        pos += 1
    o_ref = refs[pos]
    pos += 1
    st_ref = None
    if emit_state:
        st_ref = refs[pos]
        pos += 1
    q_scr, o_scr, st_scr, mask_scr = refs[pos:pos + 4]

    c = SCAN_CHUNK
    w = HG_DK
    t_len = zq_ref.shape[0]
    n_chunks = t_len // c
    lg = lbl_ref[...]
    mx = lg[0]
    for j in range(1, N_REC_LAYERS):
        mx = jnp.maximum(mx, lg[j])
    ex = [jnp.exp(lg[j] - mx) for j in range(N_REC_LAYERS)]
    den = ex[0]
    for j in range(1, N_REC_LAYERS):
        den = den + ex[j]
    lower = jnp.zeros_like(mx)
    for j in range(1, r_layer + 1):
        lower = lower + ex[j] / den

    zq = zq_ref[...]
    q_scr[...] = zq * _sigmoid(zq)
    _scan_masks(mask_scr)
    row = lax.broadcasted_iota(jnp.int32, (c, c), 0)
    col = lax.broadcasted_iota(jnp.int32, (c, c), 1)
    tris = [jnp.where(col <= row, 1.0, 0.0).astype(BF16), jnp.where(col >= row, 1.0, 0.0).astype(BF16)]
    zf_refs = (zff_ref, zfb_ref)
    for d in range(2):
        for hh in range(n_heads):
            st_scr[d, hh] = s0_ref[d, hh].T if has_s0 else jnp.zeros((HG_DV, HG_DK), F32)

    def body(i, carry):
        chains = []
        for hh in range(n_heads):
            lanes = slice(hh * w, (hh + 1) * w)
            for d in range(2):
                ci = (n_chunks - 1 - i) if d == 1 else i
                rows = pl.ds(ci * c, c) if isinstance(ci, int) else pl.ds(pl.multiple_of(ci * c, c), c)
                chains.append((d, hh, rows, lanes, q_scr[rows, lanes], zf_refs[d][rows, lanes],
                               zi_ref[rows, lanes], st_scr[d, hh]))
        outs = _scan_chunks([(q, zf, v, lower[d:d + 1, lanes], st, tris[d], d == 1)
                             for d, hh, rows, lanes, q, zf, v, st in chains], mask_scr)
        for (d, hh, rows, lanes, *_), (o, st_new) in zip(chains, outs):
            st_scr[d, hh] = st_new
            o_scr[d, rows, lanes] = o
        return carry

    if n_chunks <= 2:
        for i in range(n_chunks):
            body(i, 0)
    else:
        lax.fori_loop(0, n_chunks, body, 0)
    if emit_state:
        for d in range(2):
            for hh in range(n_heads):
                if has_alias:
                    st_ref[d, hh] = st_scr[d, hh].T
                else:
                    for layer in range(N_REC_LAYERS):
                        st_ref[layer, d, hh] = (st_scr[d, hh].T if layer == r_layer
                                                else jnp.zeros((HG_DK, HG_DV), F32))

    zg = zg_ref[...]
    gate = zg * _sigmoid(zg)
    for hh in range(n_heads):
        lanes = slice(hh * w, (hh + 1) * w)
        y = _rms(o_scr[0, :, lanes] + o_scr[1, :, lanes], gain_ref[:, lanes]) * gate[:, lanes]
        o_ref[:, lanes] = y.astype(o_ref.dtype)


def _scan(z, lb_logits, out_gain, r_layer, *, t_len, n_heads, s0=None, emit_state=False, state_out=None):
    nb = z.shape[0] // t_len
    nh = HG_HEADS
    sh = n_heads
    ng = nh // sh
    wb = sh * HG_DK
    col = lambda k: (lambda b, h: (b, k * ng + h))
    in_specs = [pl.BlockSpec((N_REC_LAYERS, 2, wb), lambda b, h: (0, 0, h))]
    mode = dict(pipeline_mode=pl.Buffered(1)) if t_len * wb * 4 >= SCAN_SINGLE_BUFFER_BYTES else {}
    in_specs += [pl.BlockSpec((t_len, wb), col(k), **mode) for k in range(5)]
    in_specs += [pl.BlockSpec((1, wb), lambda b, h: (0, h))]
    args = [lb_logits, z, z, z, z, z, out_gain]
    if s0 is not None:
        in_specs.append(pl.BlockSpec((None, None, 2, sh, HG_DK, HG_DV),
                                     lambda b, h: (b, r_layer, 0, h, 0, 0)))
        args.append(s0)
    out_specs = [pl.BlockSpec((t_len, wb), lambda b, h: (b, h))]
    out_shape = [jax.ShapeDtypeStruct((nb * t_len, nh * HG_DV), BF16)]
    aliases = {}
    if emit_state:
        if state_out is None:
            out_specs.append(pl.BlockSpec((None, N_REC_LAYERS, 2, sh, HG_DK, HG_DV), lambda b, h: (b, 0, 0, h, 0, 0)))
        else:
            out_specs.append(pl.BlockSpec((None, None, 2, sh, HG_DK, HG_DV), lambda b, h: (b, r_layer, 0, h, 0, 0)))
            aliases[len(args)] = 1
            in_specs.append(pl.BlockSpec(memory_space=pl.ANY))
            args.append(state_out)
        out_shape.append(jax.ShapeDtypeStruct((nb, N_REC_LAYERS, 2, nh, HG_DK, HG_DV), F32))
    kern = functools.partial(_scan_kernel, r_layer=r_layer, has_s0=s0 is not None, emit_state=emit_state,
                             has_alias=bool(aliases), n_heads=n_heads)
    return pl.pallas_call(
        kern,
        grid=(nb, ng),
        in_specs=in_specs,
        out_specs=out_specs,
        out_shape=out_shape,
        input_output_aliases=aliases,
        scratch_shapes=[pltpu.VMEM((t_len, wb), F32),
                        pltpu.VMEM((2, t_len, wb), F32),
                        pltpu.VMEM((2, sh, HG_DV, HG_DK), F32),
                        pltpu.VMEM((len(SCAN_LEVELS) + 2, SCAN_CHUNK, SCAN_CHUNK), F32)],
        name="hgrn_scan",
    )(*args)


def _one_hot_rows(posm, e0, g, cap):
    t_len = posm.shape[1]
    slot = lax.broadcasted_iota(jnp.int32, (cap, t_len), 0)
    parts = [jnp.where(posm[e:e + 1, :] == slot, 1.0, 0.0).astype(BF16) for e in range(e0, e0 + g)]
    return parts[0] if g == 1 else jnp.concatenate(parts, axis=0)


PREFIX_PIECE = 256


def _prefix_count(mask_f, before):
    pieces = []
    carry = jnp.zeros((mask_f.shape[0], 1), F32)
    for p0 in range(0, mask_f.shape[1], PREFIX_PIECE):
        piece = mask_f[:, p0:p0 + PREFIX_PIECE]
        pieces.append(_dot(piece.astype(BF16), before) + carry)
        carry = carry + jnp.sum(piece, axis=1, keepdims=True)
    return pieces[0] if len(pieces) == 1 else jnp.concatenate(pieces, axis=1)


def _sort_desc_segments(x, seg):
    w = x.shape[1]
    lane = lax.broadcasted_iota(jnp.int32, x.shape, 1)
    k = 2
    while k <= seg:
        j = k // 2
        while j >= 1:
            is_first = (lane & j) == 0
            partner = jnp.where(is_first, pltpu.roll(x, w - j, axis=1), pltpu.roll(x, j, axis=1))
            keep_max = is_first if k == seg else (is_first == ((lane & k) == 0))
            x = jnp.where(keep_max, jnp.maximum(x, partner), jnp.minimum(x, partner))
            j //= 2
        k *= 2
    return x


def _select_kernel(aff_ref, h_ref, g_ref, gate_ref, pos_ref, *, t_len):
    a = aff_ref[...]
    ne, w = a.shape
    cap = EC_CAPACITY * t_len // N_EXPERTS
    capf = jnp.float32(cap)
    srt = jnp.concatenate([_sort_desc_segments(a[r0:r0 + SUBLANES, :], t_len)
                           for r0 in range(0, ne, SUBLANES)], axis=0)
    src = lax.broadcasted_iota(jnp.int32, (PREFIX_PIECE, PREFIX_PIECE), 0)
    dst = lax.broadcasted_iota(jnp.int32, (PREFIX_PIECE, PREFIX_PIECE), 1)
    before = jnp.where(src < dst, 1.0, 0.0).astype(BF16)
    slot = lax.broadcasted_iota(jnp.int32, (cap, t_len), 0)
    g = max(1, min(ne, 512 // cap))
    for s in range(w // t_len):
        t0 = s * t_len
        c0 = s * cap
        seg = a[:, t0:t0 + t_len]
        thr = srt[:, t0 + cap - 1:t0 + cap]
        gt = jnp.where(seg > thr, 1.0, 0.0)
        eq = jnp.where(seg == thr, 1.0, 0.0)
        need = capf - jnp.sum(gt, axis=1, keepdims=True)
        eq_rank = _prefix_count(eq, before)
        sel = gt + eq * jnp.where(eq_rank < need, 1.0, 0.0)
        pos = _prefix_count(sel, before)
        posm = jnp.where(sel > 0.5, pos.astype(jnp.int32), -1)
        pos_ref[:, t0:t0 + t_len] = posm
        h = h_ref[t0:t0 + t_len, :]
        for e0 in range(0, ne, g):
            rows = _dot(_one_hot_rows(posm, e0, g, cap), h)
            g_ref[e0:e0 + g, c0:c0 + cap, :] = rows.reshape(g, cap, h.shape[1]).astype(g_ref.dtype)
        for e in range(ne):
            gate_ref[e, c0:c0 + cap, :] = jnp.sum(jnp.where(posm[e:e + 1, :] == slot, seg[e:e + 1, :], 0.0),
                                                  axis=1, keepdims=True)


def _select(aff_t, h, *, t_len):
    ne, n = aff_t.shape
    d = h.shape[1]
    w = SELECT_BLOCK
    capw = EC_CAPACITY * w // N_EXPERTS
    kern = functools.partial(_select_kernel, t_len=t_len)
    return pl.pallas_call(
        kern,
        grid=(n // w,),
        in_specs=[pl.BlockSpec((ne, w), lambda s: (0, s)),
                  pl.BlockSpec((w, d), lambda s: (s, 0))],
        out_specs=[pl.BlockSpec((ne, capw, d), lambda s: (0, s, 0)),
                   pl.BlockSpec((ne, capw, 1), lambda s: (0, s, 0)),
                   pl.BlockSpec((ne, w), lambda s: (0, s))],
        out_shape=[jax.ShapeDtypeStruct((ne, n // w * capw, d), BF16),
                   jax.ShapeDtypeStruct((ne, n // w * capw, 1), F32),
                   jax.ShapeDtypeStruct((ne, n), jnp.int32)],
        name="ec_select",
    )(aff_t, h)


def _ffn_kernel(xc_ref, xl_ref, gc_ref, gl_ref, wg_ref, wu_ref, wd_ref, yc_ref, yl_ref, accc, accl):
    f = pl.program_id(1)
    chunks = [(x_ref, acc, r0) for x_ref, acc in ((xc_ref, accc), (xl_ref, accl))
              for r0 in range(0, x_ref.shape[0], FFN_ROWS)]

    @pl.when(f == 0)
    def _():
        accc[...] = jnp.zeros_like(accc)
        accl[...] = jnp.zeros_like(accl)

    w = {}

    def weight(name, ref):
        if name not in w:
            w[name] = ref[...].astype(BF16)
        return w[name]

    def up(x_ref, acc, r0):
        x = x_ref[r0:r0 + FFN_ROWS, :]
        hg = _dot(x, weight("g", wg_ref))
        return hg, _dot(x, weight("u", wu_ref))

    def down(x_ref, acc, r0, hg, hu):
        hid = (hg * _sigmoid(hg) * hu).astype(BF16)
        acc[r0:r0 + FFN_ROWS, :] = acc[r0:r0 + FFN_ROWS, :] + _dot(hid, weight("d", wd_ref))

    pending = None
    for ch in chunks:
        cur = up(*ch)
        if pending is not None:
            down(*pending)
        pending = ch + cur
    down(*pending)

    @pl.when(f == pl.num_programs(1) - 1)
    def _():
        yc_ref[...] = (accc[...] * gc_ref[...]).astype(yc_ref.dtype)
        yl_ref[...] = (accl[...] * gl_ref[...]).astype(yl_ref.dtype)


def _ffn(xc, xl, gc, gl, w_gate, w_up, w_down, layer, *, bf):
    ne, nc, d = xc.shape
    nl = xl.shape[1]
    ff = w_gate.shape[3]
    per_e = lambda e, f: (e, 0, 0)
    return pl.pallas_call(
        _ffn_kernel,
        grid=(ne, ff // bf),
        in_specs=[pl.BlockSpec((None, nc, d), per_e),
                  pl.BlockSpec((None, nl, d), per_e),
                  pl.BlockSpec((None, nc, 1), per_e),
                  pl.BlockSpec((None, nl, 1), per_e),
                  pl.BlockSpec((None, None, d, bf), lambda e, f: (layer, e, 0, f)),
                  pl.BlockSpec((None, None, d, bf), lambda e, f: (layer, e, 0, f)),
                  pl.BlockSpec((None, None, bf, d), lambda e, f: (layer, e, f, 0))],
        out_specs=[pl.BlockSpec((None, nc, d), per_e),
                   pl.BlockSpec((None, nl, d), per_e)],
        out_shape=[jax.ShapeDtypeStruct((ne, nc, d), BF16),
                   jax.ShapeDtypeStruct((ne, nl, d), BF16)],
        scratch_shapes=[pltpu.VMEM((nc, d), F32), pltpu.VMEM((nl, d), F32)],
        name="expert_ffn",
    )(xc, xl, gc, gl, w_gate, w_up, w_down)


def _combine_kernel(pos_ref, y_ref, x_ref, m_ref, fn_ref, o_ref, *, cap):
    posm = pos_ref[...]
    ne = posm.shape[0]
    d = x_ref.shape[1]
    g = max(1, min(ne, 512 // cap))
    acc = jnp.zeros(x_ref.shape, F32)
    for e0 in range(0, ne, g):
        acc = acc + _dot_tn(_one_hot_rows(posm, e0, g, cap), y_ref[e0:e0 + g].reshape(g * cap, d))
    x = x_ref[...] + m_ref[5:6, :] * acc
    o_ref[...] = x if fn_ref is None else _rms(x, fn_ref[...])


def _combine(posm, y, x, mods, *, t_len, tt, base, final_gain=None):
    ne, n = posm.shape
    d = x.shape[1]
    ns = n // t_len
    cap = y.shape[1] // ns
    nt = t_len // tt
    in_specs = [pl.BlockSpec((ne, tt), lambda s, j: (0, s * nt + j)),
                pl.BlockSpec((ne, cap, d), lambda s, j: (0, s, 0)),
                pl.BlockSpec((tt, d), lambda s, j: (s * nt + j, 0)),
                pl.BlockSpec((None, 6, d), lambda s, j: (base(s), 0, 0))]
    args = [posm, y, x, mods]
    if final_gain is None:
        kern = lambda p, yr, xr, m, o: _combine_kernel(p, yr, xr, m, None, o, cap=cap)
    else:
        kern = functools.partial(_combine_kernel, cap=cap)
        in_specs.append(pl.BlockSpec((1, d), lambda s, j: (0, 0)))
        args.append(final_gain)
    return pl.pallas_call(
        kern,
        grid=(ns, nt),
        in_specs=in_specs,
        out_specs=pl.BlockSpec((tt, d), lambda s, j: (s * nt + j, 0)),
        out_shape=jax.ShapeDtypeStruct((n, d), F32),
        name="ec_combine",
    )(*args)


def _attn_in_weight(w):
    o_kr = MLA_Q_LORA + MLA_KV_LORA
    o_qg = o_kr + MLA_ROPE
    cols = jnp.concatenate([w[:, :o_kr], w[:, o_qg:], jnp.zeros((w.shape[0], KR_OFF), w.dtype), w[:, o_kr:o_qg]],
                           axis=1)
    return jnp.pad(cols, ((0, 0), (0, ATTN_Z - cols.shape[1]))).astype(BF16)


def _heads_split(w, n_heads, first, second):
    k = w.shape[0]
    w3 = w.reshape(k, n_heads, first + second)
    return jnp.concatenate([w3[:, :, :first].reshape(k, n_heads * first),
                            w3[:, :, first:].reshape(k, n_heads * second)], axis=1).astype(BF16)


def _head_slots(w, n_heads, lo, hi):
    k = w.shape[0]
    w3 = w.reshape(k, n_heads, -1)[:, :, lo:hi]
    return jnp.pad(w3, ((0, 0), (0, 0), (0, LANES - (hi - lo)))).reshape(k, n_heads * LANES)


def _mla_weights(w_uq, w_ukv):
    k = w_ukv.shape[0]
    v_cols = w_ukv.reshape(k, MLA_HEADS, MLA_NOPE + MLA_V)[:, :, MLA_NOPE:].reshape(k, VM_W)
    return (_head_slots(w_uq, MLA_HEADS, 0, MLA_NOPE + MLA_ROPE).astype(BF16),
            jnp.concatenate([_head_slots(w_ukv, MLA_HEADS, 0, MLA_NOPE), v_cols], axis=1).astype(BF16))


def _slot_tables(cos, sin):
    n = cos.shape[0]
    pad = LANES - KR_OFF - MLA_ROPE
    cos = jnp.concatenate([jnp.ones((n, KR_OFF), F32), cos, jnp.ones((n, pad), F32)], axis=1)
    sin = jnp.concatenate([jnp.zeros((n, KR_OFF), F32), sin, jnp.zeros((n, pad), F32)], axis=1)
    return jnp.tile(cos, (1, MLA_HEADS)), jnp.tile(sin, (1, MLA_HEADS))


def _rope_tables(width, n_rep):
    half = width // 2
    t = jnp.arange(DEC_SEQ)
    rows = (t // GRID_W).astype(F32)
    cols = (t % GRID_W).astype(F32)
    inv_freq = ROPE_BASE ** (-jnp.arange(0, half, 2, dtype=F32) / half)
    ar = rows[:, None] * inv_freq
    ac = cols[:, None] * inv_freq
    cos = jnp.concatenate([jnp.cos(ar), jnp.cos(ar), jnp.cos(ac), jnp.cos(ac)], axis=1)
    sin = jnp.concatenate([-jnp.sin(ar), jnp.sin(ar), -jnp.sin(ac), jnp.sin(ac)], axis=1)
    return jnp.tile(cos, (1, n_rep)), jnp.tile(sin, (1, n_rep))


def kernel(x_prompt, x_sample, c, cache_mla_ckv, cache_mla_krope, cache_gqa_k, cache_gqa_v, state_hgrn,
           c_ctx, norm1, norm2, w_mod, b_mod, w_attn_in, mla_q_norm, w_mla_uq, mla_kv_norm, w_mla_ukv,
           gqa_sink, w_attn_out, w_rec_in, rec_lb_logits, rec_out_norm, w_rec_out, w_router, w_gate,
           w_up, w_down, final_norm):
    d = D_MODEL
    xc = x_prompt.reshape(BATCH * SEQ, d)
    xl = x_sample.reshape(DEC_BATCH * DEC_SEQ, d)
    cvecs = jnp.concatenate([c_ctx[None, :], c, jnp.zeros((N_MOD_ROWS - N_MOD_GROUPS, d), F32)], axis=0)
    mods_all = _mod_vectors(cvecs, w_mod, b_mod).reshape(DEPTH, N_MOD_ROWS, 6, d)[:, :N_MOD_GROUPS]
    grp_c = _group_map(None, 0, 0)
    cm, sm = _slot_tables(*_rope_tables(MLA_ROPE, 1))
    cg, sg = _rope_tables(GQA_HD, GQA_HEADS)
    kvw = GQA_KV_HEADS * GQA_HD

    caches, st_all = None, None
    for layer in range(DEPTH):
        mods = mods_all[layer]
        g1 = norm1[layer].reshape(1, d)
        g2 = norm2[layer].reshape(1, d)
        if layer % 2 == 0:
            a = layer // 2
            w_in = _attn_in_weight(w_attn_in[a])
            w_uq, w_ukv = _mla_weights(w_mla_uq[a], w_mla_ukv[a])
            w_out = w_attn_out[a].astype(BF16)
            qn_g = mla_q_norm[a].reshape(1, -1)
            kvn_g = mla_kv_norm[a].reshape(1, -1)
            sink = gqa_sink[a]
            zc = _modproj(xc, g1, mods, w_in, sub=0, bm=512, grp=grp_c)
            zl = _modproj(xl, g1, mods, w_in, sub=0, bm=512, grp=_group_map(DEC_SEQ, 512, 1))
            oc, *caches = _attn_ctx(zc, sink, qn_g, _heads_split(w_mla_uq[a], MLA_HEADS, MLA_NOPE, MLA_ROPE), kvn_g,
                                    _heads_split(w_mla_ukv[a], MLA_HEADS, MLA_NOPE, MLA_V), a, caches)
            qcat, kcat, vm, qg, kg, vg = _lat_prep(zl, qn_g, w_uq, kvn_g, w_ukv, cm, sm, cg, sg, bm=256)
            kr_c = jnp.pad(cache_mla_krope[:, a], ((0, 0), (0, 0), (KR_OFF, LANES - KR_OFF - MLA_ROPE)))
            kcat_c, vm_c = _cache_kv(cache_mla_ckv, kr_c, a, w_ukv)
            per_b = lambda x: x.reshape(DEC_BATCH, DEC_SEQ, -1)
            cat = lambda lat, ctx: jnp.concatenate([per_b(lat), ctx], axis=1)
            band = lambda x: jnp.pad(per_b(x), ((0, 0), (BAND_BLOCK, BAND_BLOCK), (0, 0)))
            ol = _lat_attn(sink, qcat, cat(kcat, kcat_c), cat(vm, vm_c), qg, band(kg), band(vg),
                           cache_gqa_k[:, a].reshape(DEC_BATCH, PAST_LEN, kvw).astype(BF16),
                           cache_gqa_v[:, a].reshape(DEC_BATCH, PAST_LEN, kvw).astype(BF16), qb=LAT_QB)
        else:
            r = layer // 2
            w_in = w_rec_in[r].astype(BF16)
            w_out = w_rec_out[r].astype(BF16)
            og = rec_out_norm[r].reshape(1, -1)
            zc = _modproj(xc, g1, mods, w_in, sub=0, bm=256, grp=grp_c)
            zl = _modproj(xl, g1, mods, w_in, sub=0, bm=256, grp=_group_map(DEC_SEQ, 256, 1))
            oc, st_all = _scan(zc, rec_lb_logits, og, r, t_len=SEQ, n_heads=8, emit_state=True, state_out=st_all)
            (ol,) = _scan(zl, rec_lb_logits, og, r, t_len=DEC_SEQ, n_heads=4, s0=state_hgrn)
        wr_t = w_router[layer].T.astype(BF16)
        xc, hc, aff_c = _outproj_router(oc, w_out, xc, mods, g2, wr_t, bm=512, grp=grp_c)
        xl, hl, aff_l = _outproj_router(ol, w_out, xl, mods, g2, wr_t, bm=512, grp=_group_map(DEC_SEQ, 512, 1))
        gxc, gate_c, pos_c = _select(aff_c, hc, t_len=SEQ)
        gxl, gate_l, pos_l = _select(aff_l, hl, t_len=DEC_SEQ)
        yc, yl = _ffn(gxc, gxl, gate_c, gate_l, w_gate, w_up, w_down, layer, bf=512)
        fn = final_norm.reshape(1, d) if layer == DEPTH - 1 else None
        xc = _combine(pos_c, yc, xc, mods, t_len=SEQ, tt=SEQ, base=lambda s: 0, final_gain=fn)
        xl = _combine(pos_l, yl, xl, mods, t_len=DEC_SEQ, tt=512, base=lambda s: 1 + s, final_gain=fn)

    y_prompt = xc.reshape(BATCH, SEQ, d)
    y_sample = xl.reshape(DEC_BATCH, DEC_SEQ, d)
    ckv_all, kr_all, kg_all, vg_all = caches
    heads = lambda x: x.reshape(BATCH, N_ATTN_LAYERS, SEQ, GQA_KV_HEADS, GQA_HD)
    return (y_prompt, y_sample, ckv_all, kr_all, heads(kg_all), heads(vg_all), st_all)
```

```python
import functools

import jax
import jax.numpy as jnp
from jax import lax
from jax.experimental import pallas as pl
from jax.experimental.pallas import tpu as pltpu

F32 = jnp.float32
BF16 = jnp.bfloat16

D_MODEL = 1024
BATCH = 32
SEQ = 256
DEPTH = 4
DEC_BATCH = 2
DEC_SEQ = 2048
PAST_LEN = 512
GRID_W = 64
ROPE_BASE = 10000.0
NORM_EPS = 1e-6
NEG_BIG = -1e30
F_MIN = 1e-6
N_ATTN_LAYERS = (DEPTH + 1) // 2
N_REC_LAYERS = DEPTH // 2
MLA_HEADS = 8
MLA_Q_LORA = 384
MLA_KV_LORA = 256
MLA_NOPE = 64
MLA_ROPE = 32
MLA_V = 64
MLA_SCALE = (MLA_NOPE + MLA_ROPE) ** -0.5
GQA_HEADS = 8
GQA_KV_HEADS = 2
GQA_GROUP = GQA_HEADS // GQA_KV_HEADS
GQA_HD = 64
GQA_SCALE = GQA_HD ** -0.5
WINDOW = 128
BAND_BLOCK = 128
HG_HEADS = 8
HG_DK = 128
HG_DV = 128
REC_IN = 3 * HG_HEADS * HG_DK + 2 * HG_HEADS * HG_DV
N_EXPERTS = 16
EC_CAPACITY = 2
D_FF = 2048

LANES = 128
N_MOD_ROWS = 8
N_MOD_GROUPS = 1 + DEC_BATCH

ZC_Q = 0
ZC_KV = ZC_Q + MLA_Q_LORA
ZC_QG = ZC_KV + MLA_KV_LORA
ZC_KG = ZC_QG + GQA_HEADS * GQA_HD
ZC_VG = ZC_KG + GQA_KV_HEADS * GQA_HD
ZC_KR = ZC_VG + GQA_KV_HEADS * GQA_HD
ATTN_Z = ZC_KR + LANES
KR_OFF = MLA_NOPE
QK_W = MLA_HEADS * LANES
VM_W = MLA_HEADS * MLA_V
O_M_W = MLA_HEADS * MLA_V
O_G_W = GQA_HEADS * GQA_HD

SUBLANES = 8
SCAN_CHUNK = 128
SCAN_LEVELS = (64, 32, 16, 8)
SCAN_DIAG = 8
SELECT_BLOCK = 2048
FFN_ROWS = 256
LAT_QB = 512
CTX_REQS = 4
COMBINE_SETS = 4
SCAN_SINGLE_BUFFER_BYTES = 4 << 20


def _dot(a, b):
    return jnp.dot(a, b, preferred_element_type=F32)


def _dot_nt(a, b):
    return lax.dot_general(a, b, (((1,), (1,)), ((), ())), preferred_element_type=F32)


def _dot_tn(a, b):
    return lax.dot_general(a, b, (((0,), (0,)), ((), ())), preferred_element_type=F32)


def _sigmoid(x):
    return 1.0 / (1.0 + jnp.exp(-x))


def _rms(x, gain):
    return x * lax.rsqrt(_row_sum(x * x) * (1.0 / x.shape[-1]) + NORM_EPS) * gain


def _mod_kernel(c_ref, w_ref, b_ref, o_ref):
    c = c_ref[...]
    s = (c * _sigmoid(c)).astype(BF16)
    o_ref[...] = _dot(s, w_ref[...].astype(BF16)) + b_ref[...]


def _mod_vectors(cvecs, w_mod, b_mod):
    bn = 1536
    nd = 6 * D_MODEL
    return pl.pallas_call(
        _mod_kernel,
        grid=(DEPTH, nd // bn),
        in_specs=[pl.BlockSpec((N_MOD_ROWS, D_MODEL), lambda l, j: (0, 0)),
                  pl.BlockSpec((None, D_MODEL, bn), lambda l, j: (l, 0, j)),
                  pl.BlockSpec((None, 1, bn), lambda l, j: (l, 0, j))],
        out_specs=pl.BlockSpec((None, N_MOD_ROWS, bn), lambda l, j: (l, 0, j)),
        out_shape=jax.ShapeDtypeStruct((DEPTH, N_MOD_ROWS, nd), F32),
        name="mod_vectors",
    )(cvecs, w_mod, b_mod.reshape(DEPTH, 1, nd))


def _group_map(rows_per_group, bm, base):
    if rows_per_group is None:
        return lambda i: base
    return lambda i: base + (i * bm) // rows_per_group


def _modproj_kernel(x_ref, g_ref, m_ref, w_ref, o_ref, *, shift_row, chunk):
    h = _rms(x_ref[...], g_ref[...])
    h = (h * (1.0 + m_ref[shift_row + 1:shift_row + 2, :]) + m_ref[shift_row:shift_row + 1, :]).astype(BF16)
    nout = o_ref.shape[1]
    for j in range(0, nout, chunk):
        o_ref[:, j:j + chunk] = _dot(h, w_ref[:, j:j + chunk])


def _modproj(x, gain, mods, w, *, sub, bm, grp):
    n, d = x.shape
    nout = w.shape[1]
    chunk = nout if nout <= 1536 else 1024
    kern = functools.partial(_modproj_kernel, shift_row=3 * sub, chunk=chunk)
    return pl.pallas_call(
        kern,
        grid=(n // bm,),
        in_specs=[pl.BlockSpec((bm, d), lambda i: (i, 0)),
                  pl.BlockSpec((1, d), lambda i: (0, 0)),
                  pl.BlockSpec((None, 6, d), lambda i: (grp(i), 0, 0)),
                  pl.BlockSpec((d, nout), lambda i: (0, 0), pipeline_mode=pl.Buffered(1))],
        out_specs=pl.BlockSpec((bm, nout), lambda i: (i, 0)),
        out_shape=jax.ShapeDtypeStruct((n, nout), F32),
        name="modproj",
    )(x, gain, mods, w)


def _outproj_router_kernel(a_ref, w_ref, x_ref, m_ref, g_ref, wr_ref, x_out, h_out, aff_out):
    x = x_ref[...] + m_ref[2:3, :] * _dot(a_ref[...], w_ref[...])
    x_out[...] = x
    h = (_rms(x, g_ref[...]) * (1.0 + m_ref[4:5, :]) + m_ref[3:4, :]).astype(BF16)
    h_out[...] = h
    logits = _dot_nt(wr_ref[...], h)
    e = jnp.exp(logits - jnp.max(logits, axis=0, keepdims=True))
    aff_out[...] = e / jnp.sum(e, axis=0, keepdims=True)


def _outproj_router(a, w, x, mods, gain2, wr_t, *, bm, grp):
    n, d = x.shape
    k = a.shape[1]
    row = lambda i: (i, 0)
    const = lambda i: (0, 0)
    return pl.pallas_call(
        _outproj_router_kernel,
        grid=(n // bm,),
        in_specs=[pl.BlockSpec((bm, k), row),
                  pl.BlockSpec((k, d), const),
                  pl.BlockSpec((bm, d), row),
                  pl.BlockSpec((None, 6, d), lambda i: (grp(i), 0, 0)),
                  pl.BlockSpec((1, d), const),
                  pl.BlockSpec((N_EXPERTS, d), const)],
        out_specs=[pl.BlockSpec((bm, d), row),
                   pl.BlockSpec((bm, d), row),
                   pl.BlockSpec((N_EXPERTS, bm), lambda i: (0, i))],
        out_shape=[jax.ShapeDtypeStruct((n, d), F32),
                   jax.ShapeDtypeStruct((n, d), BF16),
                   jax.ShapeDtypeStruct((N_EXPERTS, n), F32)],
        name="outproj_router",
    )(a, w, x, mods, gain2, wr_t)


def _lane_fold(x, op):
    w = x.shape[1]
    if w <= LANES or w % LANES:
        return x
    acc = x[:, :LANES]
    for j in range(1, w // LANES):
        acc = op(acc, x[:, j * LANES:(j + 1) * LANES])
    return acc


def _row_max(s):
    return jnp.max(_lane_fold(s, jnp.maximum), axis=-1, keepdims=True)


def _row_sum(p):
    return jnp.sum(_lane_fold(p, jnp.add), axis=-1, keepdims=True)


def _softmax_pv(s, v, sink=None):
    m = _row_max(s)
    if sink is not None:
        m = jnp.maximum(m, sink)
    p = jnp.exp(s - m)
    l = _row_sum(p)
    if sink is not None:
        l = l + jnp.exp(sink - m)
    return _dot(p.astype(BF16), v) / l


def _swap_pairs(x, q):
    w = x.shape[1]
    lane = lax.broadcasted_iota(jnp.int32, x.shape, 1)
    first = (lane % (2 * q)) < q
    return jnp.where(first, pltpu.roll(x, w - q, axis=1), pltpu.roll(x, q, axis=1))


def _rope(x, cos, sin, q):
    return x * cos + _swap_pairs(x, q) * sin


def _attn_project(z, qn_ref, wuq_ref, kvn_ref, wukv_ref):
    cq = _rms(z[:, ZC_Q:ZC_Q + MLA_Q_LORA], qn_ref[...]).astype(BF16)
    qcat = _dot(cq, wuq_ref[...])
    ckv_n = _rms(z[:, ZC_KV:ZC_KV + MLA_KV_LORA], kvn_ref[...])
    kv = _dot(ckv_n.astype(BF16), wukv_ref[...])
    return qcat, ckv_n, kv


def _tile_heads(kr_slot):
    return jnp.concatenate([kr_slot] * MLA_HEADS, axis=1)


def _staggered(n, first, second, depth):
    pending = []
    for i in range(n):
        pending.append((i, first(i)))
        if len(pending) > depth:
            second(*pending.pop(0))
    for item in pending:
        second(*item)


def _gqa_sink_column(sink_ref, g, rows):
    return jnp.concatenate([jnp.full((rows, 1), sink_ref[g * GQA_GROUP + j], F32) for j in range(GQA_GROUP)], axis=0)


def _stack_group(qg, g):
    return jnp.concatenate([qg[:, (g * GQA_GROUP + j) * GQA_HD:(g * GQA_GROUP + j + 1) * GQA_HD]
                            for j in range(GQA_GROUP)], axis=0)


def _attn_ctx_kernel(*refs, n_alias, a):
    sink_ref, z_ref, qn_ref, wuq_ref, kvn_ref, wukv_ref = refs[:6]
    o_ref, ckv_ref, kr_ref, kg_ref, vg_ref = refs[6 + n_alias:]
    t_len = SEQ
    z = z_ref[...]
    cq = _rms(z[:, ZC_Q:ZC_Q + MLA_Q_LORA], qn_ref[...]).astype(BF16)
    qm = _dot(cq, wuq_ref[...])
    ckv_n = _rms(z[:, ZC_KV:ZC_KV + MLA_KV_LORA], kvn_ref[...])
    kv = _dot(ckv_n.astype(BF16), wukv_ref[...])
    kr = z[:, ZC_KR + KR_OFF:ZC_KR + KR_OFF + MLA_ROPE]
    kg = z[:, ZC_KG:ZC_KG + GQA_KV_HEADS * GQA_HD]
    vg = z[:, ZC_VG:ZC_VG + GQA_KV_HEADS * GQA_HD]
    reqs = [slice(r * t_len, (r + 1) * t_len) for r in range(CTX_REQS)]
    for ref, val in ((ckv_ref, ckv_n), (kr_ref, kr), (kg_ref, kg), (vg_ref, vg)):
        for r, rows in enumerate(reqs):
            if n_alias:
                ref[r] = val[rows]
            else:
                for layer in range(N_ATTN_LAYERS):
                    ref[r, layer] = val[rows] if layer == a else jnp.zeros_like(val[rows])
    qn_w = MLA_HEADS * MLA_NOPE
    qn = (qm[:, :qn_w] * MLA_SCALE).astype(BF16)
    qr = (qm[:, qn_w:] * MLA_SCALE).astype(BF16)
    kn = kv[:, :qn_w].astype(BF16)
    vm = kv[:, qn_w:].astype(BF16)
    krb = kr.astype(BF16)
    for h in range(MLA_HEADS):
        nope = slice(h * MLA_NOPE, (h + 1) * MLA_NOPE)
        rope = slice(h * MLA_ROPE, (h + 1) * MLA_ROPE)
        ss = [_dot_nt(qn[rows, nope], kn[rows, nope]) + _dot_nt(qr[rows, rope], krb[rows]) for rows in reqs]
        for rows, s in zip(reqs, ss):
            o = _softmax_pv(s, vm[rows, h * MLA_V:(h + 1) * MLA_V])
            o_ref[rows, h * MLA_V:(h + 1) * MLA_V] = o.astype(o_ref.dtype)
    qg = (z[:, ZC_QG:ZC_QG + O_G_W] * GQA_SCALE).astype(BF16)
    kgb = kg.astype(BF16)
    vgb = vg.astype(BF16)
    for h in range(GQA_HEADS):
        g = h // GQA_GROUP
        kvs = slice(g * GQA_HD, (g + 1) * GQA_HD)
        ss = [_dot_nt(qg[rows, h * GQA_HD:(h + 1) * GQA_HD], kgb[rows, kvs]) for rows in reqs]
        for rows, s in zip(reqs, ss):
            o = _softmax_pv(s, vgb[rows, kvs], sink_ref[h])
            o_ref[rows, O_M_W + h * GQA_HD:O_M_W + (h + 1) * GQA_HD] = o.astype(o_ref.dtype)


def _attn_ctx(z, sink, q_norm, w_uq, kv_norm, w_ukv, a, caches=None):
    nb = z.shape[0] // SEQ
    rq = CTX_REQS
    kvw = GQA_KV_HEADS * GQA_HD
    const = lambda b: (0, 0)
    row = lambda b: (b, 0)
    cache_w = (MLA_KV_LORA, MLA_ROPE, kvw, kvw)
    in_specs = [pl.BlockSpec(memory_space=pltpu.SMEM),
                pl.BlockSpec((rq * SEQ, ATTN_Z), row),
                pl.BlockSpec((1, MLA_Q_LORA), const),
                pl.BlockSpec(w_uq.shape, const),
                pl.BlockSpec((1, MLA_KV_LORA), const),
                pl.BlockSpec(w_ukv.shape, const)]
    args = [sink, z, q_norm, w_uq, kv_norm, w_ukv]
    aliases = {}
    if caches is not None:
        for j, c in enumerate(caches):
            aliases[len(args)] = 1 + j
            in_specs.append(pl.BlockSpec(memory_space=pl.ANY))
            args.append(c)
    kern = functools.partial(_attn_ctx_kernel, n_alias=len(aliases), a=a)
    if aliases:
        cache_specs = [pl.BlockSpec((rq, None, SEQ, w), lambda b: (b, a, 0, 0)) for w in cache_w]
    else:
        cache_specs = [pl.BlockSpec((rq, N_ATTN_LAYERS, SEQ, w), lambda b: (b, 0, 0, 0)) for w in cache_w]
    return pl.pallas_call(
        kern,
        grid=(nb // rq,),
        in_specs=in_specs,
        out_specs=[pl.BlockSpec((rq * SEQ, O_M_W + O_G_W), row)] + cache_specs,
        out_shape=[jax.ShapeDtypeStruct((nb * SEQ, O_M_W + O_G_W), BF16)]
        + [jax.ShapeDtypeStruct((nb, N_ATTN_LAYERS, SEQ, w), F32) for w in cache_w],
        input_output_aliases=aliases,
        name="attn_ctx",
    )(*args)


def _lat_prep_kernel(z_ref, qn_ref, wuq_ref, kvn_ref, wukv_ref, cm_ref, sm_ref, cg_ref, sg_ref,
                     qcat_o, kcat_o, vm_o, qg_o, kg_o, vg_o):
    z = z_ref[...]
    qcat, ckv_n, kv = _attn_project(z, qn_ref, wuq_ref, kvn_ref, wukv_ref)
    cm = cm_ref[...]
    sm = sm_ref[...]
    cg = cg_ref[...]
    sg = sg_ref[...]
    qcat_o[...] = (_rope(qcat, cm, sm, MLA_ROPE // 4) * MLA_SCALE).astype(BF16)
    kr_slot = _rope(z[:, ZC_KR:ZC_KR + LANES], cm[:, :LANES], sm[:, :LANES], MLA_ROPE // 4)
    kcat_o[...] = (kv[:, :QK_W] + _tile_heads(kr_slot)).astype(BF16)
    vm_o[...] = kv[:, QK_W:].astype(BF16)
    qg_o[...] = (_rope(z[:, ZC_QG:ZC_QG + O_G_W], cg, sg, GQA_HD // 4) * GQA_SCALE).astype(BF16)
    kvw = GQA_KV_HEADS * GQA_HD
    kg_o[...] = _rope(z[:, ZC_KG:ZC_KG + kvw], cg[:, :kvw], sg[:, :kvw], GQA_HD // 4).astype(BF16)
    vg_o[...] = z[:, ZC_VG:ZC_VG + kvw].astype(BF16)


def _lat_prep(z, q_norm, w_uq, kv_norm, w_ukv, cm, sm, cg, sg, *, bm):
    n = z.shape[0]
    per = DEC_SEQ // bm
    kvw = GQA_KV_HEADS * GQA_HD
    const = lambda i: (0, 0)
    row = lambda i: (i, 0)
    pos = lambda i: (i % per, 0)
    widths = (QK_W, QK_W, VM_W, O_G_W, kvw, kvw)
    return pl.pallas_call(
        _lat_prep_kernel,
        grid=(n // bm,),
        in_specs=[pl.BlockSpec((bm, ATTN_Z), row),
                  pl.BlockSpec((1, MLA_Q_LORA), const),
                  pl.BlockSpec(w_uq.shape, const),
                  pl.BlockSpec((1, MLA_KV_LORA), const),
                  pl.BlockSpec(w_ukv.shape, const),
                  pl.BlockSpec((bm, QK_W), pos),
                  pl.BlockSpec((bm, QK_W), pos),
                  pl.BlockSpec((bm, O_G_W), pos),
                  pl.BlockSpec((bm, O_G_W), pos)],
        out_specs=[pl.BlockSpec((bm, w), row) for w in widths],
        out_shape=[jax.ShapeDtypeStruct((n, w), BF16) for w in widths],
        name="lat_prep",
    )(z, q_norm, w_uq, kv_norm, w_ukv, cm, sm, cg, sg)


def _cache_kv_kernel(c_ref, kr_ref, w_ref, kcat_o, vm_o):
    kv = _dot(c_ref[...].astype(BF16), w_ref[...])
    kcat_o[...] = (kv[:, :QK_W] + _tile_heads(kr_ref[...])).astype(BF16)
    vm_o[...] = kv[:, QK_W:].astype(BF16)


def _cache_kv(cache_ckv, kr_slot, a, w_ukv):
    nb = cache_ckv.shape[0]
    return pl.pallas_call(
        _cache_kv_kernel,
        grid=(nb,),
        in_specs=[pl.BlockSpec((None, None, PAST_LEN, MLA_KV_LORA), lambda b: (b, a, 0, 0)),
                  pl.BlockSpec((None, PAST_LEN, LANES), lambda b: (b, 0, 0)),
                  pl.BlockSpec(w_ukv.shape, lambda b: (0, 0))],
        out_specs=[pl.BlockSpec((None, PAST_LEN, QK_W), lambda b: (b, 0, 0)),
                   pl.BlockSpec((None, PAST_LEN, VM_W), lambda b: (b, 0, 0))],
        out_shape=[jax.ShapeDtypeStruct((nb, PAST_LEN, QK_W), BF16),
                   jax.ShapeDtypeStruct((nb, PAST_LEN, VM_W), BF16)],
        name="cache_kv",
    )(cache_ckv, kr_slot, w_ukv)


def _lat_attn_kernel(sink_ref, qcat_ref, kcat_ref, vm_ref, qg_ref, kg_ref, vg_ref, kctx_ref, vctx_ref, o_ref):
    n = pl.program_id(1)
    qcat = qcat_ref[...]

    def mla_scores(h):
        return _dot_nt(qcat[:, h * LANES:(h + 1) * LANES], kcat_ref[:, h * LANES:(h + 1) * LANES])

    def mla_out(h, s):
        o_ref[:, h * MLA_V:(h + 1) * MLA_V] = _softmax_pv(s, vm_ref[:, h * MLA_V:(h + 1) * MLA_V]).astype(o_ref.dtype)

    _staggered(MLA_HEADS, mla_scores, mla_out, depth=2)

    bb = BAND_BLOCK
    nsub = qg_ref.shape[0] // bb
    rows = GQA_GROUP * bb
    tq = lax.broadcasted_iota(jnp.int32, (rows, 3 * bb), 0) & (bb - 1)
    kk = lax.broadcasted_iota(jnp.int32, (rows, 3 * bb), 1)
    qg = qg_ref[...]
    kctx = kctx_ref[...]
    vctx = vctx_ref[...]

    def gqa_scores(i):
        j, g = divmod(i, GQA_KV_HEADS)
        hs = slice(g * GQA_HD, (g + 1) * GQA_HD)
        blk = n * nsub + j
        start = pl.multiple_of(blk * bb, bb)
        lo = jnp.maximum(tq, (1 - blk) * bb)
        hi = jnp.minimum(tq + 2 * WINDOW, (DEC_SEQ // bb + 1 - blk) * bb - 1)
        q = _stack_group(qg[j * bb:(j + 1) * bb, :], g)
        s_loc = jnp.where((kk >= lo) & (kk <= hi), _dot_nt(q, kg_ref[pl.ds(start, 3 * bb), hs]), NEG_BIG)
        return s_loc, _dot_nt(q, kctx[:, hs]), start

    def gqa_out(i, res):
        j, g = divmod(i, GQA_KV_HEADS)
        hs = slice(g * GQA_HD, (g + 1) * GQA_HD)
        s_loc, s_ctx, start = res
        sink = _gqa_sink_column(sink_ref, g, bb)
        m = jnp.maximum(_row_max(jnp.maximum(_lane_fold(s_loc, jnp.maximum), _lane_fold(s_ctx, jnp.maximum))), sink)
        p_loc = jnp.exp(s_loc - m)
        p_ctx = jnp.exp(s_ctx - m)
        l = _row_sum(_lane_fold(p_loc, jnp.add) + _lane_fold(p_ctx, jnp.add)) + jnp.exp(sink - m)
        o = (_dot(p_loc.astype(BF16), vg_ref[pl.ds(start, 3 * bb), hs]) + _dot(p_ctx.astype(BF16), vctx[:, hs])) / l
        for jj in range(GQA_GROUP):
            h = g * GQA_GROUP + jj
            o_ref[j * bb:(j + 1) * bb, O_M_W + h * GQA_HD:O_M_W + (h + 1) * GQA_HD] = (
                o[jj * bb:(jj + 1) * bb].astype(o_ref.dtype))

    _staggered(nsub * GQA_KV_HEADS, gqa_scores, gqa_out, depth=3)


def _lat_attn(sink, qcat, kcat_all, vm_all, qg, kg_pad, vg_pad, kg_ctx, vg_ctx, *, qb):
    nb = DEC_BATCH
    nq = DEC_SEQ // qb
    tk = kcat_all.shape[1]
    tp = kg_pad.shape[1]
    kvw = GQA_KV_HEADS * GQA_HD
    qrow = lambda b, n: (b * nq + n, 0)
    per_b = lambda b, n: (b, 0, 0)
    return pl.pallas_call(
        _lat_attn_kernel,
        grid=(nb, nq),
        in_specs=[pl.BlockSpec(memory_space=pltpu.SMEM),
                  pl.BlockSpec((qb, QK_W), qrow),
                  pl.BlockSpec((None, tk, QK_W), per_b),
                  pl.BlockSpec((None, tk, VM_W), per_b),
                  pl.BlockSpec((qb, O_G_W), qrow),
                  pl.BlockSpec((None, tp, kvw), per_b),
                  pl.BlockSpec((None, tp, kvw), per_b),
                  pl.BlockSpec((None, PAST_LEN, kvw), per_b),
                  pl.BlockSpec((None, PAST_LEN, kvw), per_b)],
        out_specs=pl.BlockSpec((qb, O_M_W + O_G_W), qrow),
        out_shape=jax.ShapeDtypeStruct((nb * DEC_SEQ, O_M_W + O_G_W), BF16),
        name="lat_attn",
    )(sink, qcat, kcat_all, vm_all, qg, kg_pad, vg_pad, kg_ctx, vg_ctx)


def _row_tiles(x):
    return [x[i * SUBLANES:(i + 1) * SUBLANES, :] for i in range(x.shape[0] // SUBLANES)]


def _scan_masks(mask_ref):
    c = SCAN_CHUNK
    t = lax.broadcasted_iota(jnp.int32, (c, c), 0)
    s = lax.broadcasted_iota(jnp.int32, (c, c), 1)
    for li, m in enumerate(SCAN_LEVELS):
        sh = m.bit_length() - 1
        mask_ref[li] = jnp.where((t >> sh) == (s >> sh), 1.0, 0.0)
    sh = SCAN_DIAG.bit_length() - 1
    same = (t >> sh) == (s >> sh)
    mask_ref[len(SCAN_LEVELS)] = jnp.where(same & (s <= t), 1.0, 0.0)
    mask_ref[len(SCAN_LEVELS) + 1] = jnp.where(same & (s >= t), 1.0, 0.0)


def _level_rows(m, rev):
    mt = m // SUBLANES
    nt = SCAN_CHUNK // SUBLANES
    groups = []
    for g in range(nt // (2 * mt)):
        first = list(range(g * 2 * mt, g * 2 * mt + mt))
        second = list(range(g * 2 * mt + mt, (g + 1) * 2 * mt))
        if rev:
            groups.append((first, second, second[0], 0))
        else:
            groups.append((second, first, first[-1], SUBLANES - 1))
    return groups


def _scan_chunks(chains, mask_ref):
    c = SCAN_CHUNK
    half = c // 2
    nt = c // SUBLANES
    n_lv = len(SCAN_LEVELS)
    work = []
    for q, zf, v, lb, st, tri, rev in chains:
        f = lb + (1.0 - lb) * _sigmoid(zf)
        lf = jnp.log2(jnp.maximum(f, F_MIN))
        hi = lf.astype(BF16)
        mid = (lf - hi.astype(F32)).astype(BF16)
        work.append(dict(q=q, k=1.0 - f, v=v, vb=v.astype(BF16), st=st, tri=tri, rev=rev, hi=hi, mid=mid))
    for w in work:
        w["b"] = _dot(w["tri"], w["hi"]) + _dot(w["tri"], w["mid"])
        w["qt"], w["kt"], w["bt"], w["vt"] = (_row_tiles(w[n]) for n in ("q", "k", "b", "v"))
    for w in work:
        mid_row = SCAN_DIAG // 2 if w["rev"] else SCAN_DIAG // 2 - 1
        e = jnp.concatenate([t - t[mid_row:mid_row + 1, :] for t in w["bt"]], axis=0)
        a = _dot_nt((w["q"] * jnp.exp2(e)).astype(BF16), (w["k"] * jnp.exp2(-e)).astype(BF16))
        w["a"] = jnp.where(mask_ref[n_lv + (1 if w["rev"] else 0)] > 0.5, a, 0.0).astype(BF16)
    for w in work:
        o = _dot(w["a"], w["vb"]) + _dot_nt((w["q"] * jnp.exp2(w["b"])).astype(BF16), w["st"].astype(BF16))
        w["o"] = _row_tiles(o)
    for li, m in enumerate(SCAN_LEVELS):
        for w in work:
            q_parts, k_parts, w["q_idx"], w["k_idx"] = [], [], [], []
            for q_rows, k_rows, rt, rr in _level_rows(m, w["rev"]):
                r = w["bt"][rt][rr:rr + 1, :]
                q_parts += [w["qt"][i] * jnp.exp2(w["bt"][i] - r) for i in q_rows]
                k_parts += [w["kt"][i] * jnp.exp2(r - w["bt"][i]) for i in k_rows]
                w["q_idx"] += q_rows
                w["k_idx"] += k_rows
            al = _dot_nt(jnp.concatenate(q_parts, axis=0).astype(BF16),
                         jnp.concatenate(k_parts, axis=0).astype(BF16))
            if m != half:
                al = al * mask_ref[li, :half, :half]
            w["al"] = al.astype(BF16)
        for w in work:
            vl = jnp.concatenate([w["vt"][i] for i in w["k_idx"]], axis=0).astype(BF16)
            ol = _row_tiles(_dot(w["al"], vl))
            for j, i in enumerate(w["q_idx"]):
                w["o"][i] = w["o"][i] + ol[j]
    outs = []
    for w in work:
        b_tot = w["bt"][0][0:1, :] if w["rev"] else w["bt"][nt - 1][SUBLANES - 1:SUBLANES, :]
        k_end = (w["k"] * jnp.exp2(b_tot - w["b"])).astype(BF16)
        st_new = w["st"] * jnp.exp2(b_tot) + _dot_tn(w["vb"], k_end)
        outs.append((jnp.concatenate(w["o"], axis=0), st_new))
    return outs


def _scan_kernel(*refs, r_layer, has_s0, emit_state, has_alias, n_heads):
    lbl_ref, zq_ref, zff_ref, zfb_ref, zi_ref, zg_ref, gain_ref = refs[:7]
    pos = 7
    s0_ref = None
    if has_s0:
        s0_ref = refs[pos]
        pos += 1
    if has_alias:
        pos += 1
    o_ref = refs[pos]
    pos += 1
    st_ref = None
    if emit_state:
        st_ref = refs[pos]
        pos += 1
    q_scr, o_scr, st_scr, mask_scr = refs[pos:pos + 4]

    c = SCAN_CHUNK
    w = HG_DK
    t_len = zq_ref.shape[0]
    n_chunks = t_len // c
    lg = lbl_ref[...]
    mx = lg[0]
    for j in range(1, N_REC_LAYERS):
        mx = jnp.maximum(mx, lg[j])
    ex = [jnp.exp(lg[j] - mx) for j in range(N_REC_LAYERS)]
    den = ex[0]
    for j in range(1, N_REC_LAYERS):
        den = den + ex[j]
    lower = jnp.zeros_like(mx)
    for j in range(1, r_layer + 1):
        lower = lower + ex[j] / den

    zq = zq_ref[...]
    q_scr[...] = zq * _sigmoid(zq)
    _scan_masks(mask_scr)
    row = lax.broadcasted_iota(jnp.int32, (c, c), 0)
    col = lax.broadcasted_iota(jnp.int32, (c, c), 1)
    tris = [jnp.where(col <= row, 1.0, 0.0).astype(BF16), jnp.where(col >= row, 1.0, 0.0).astype(BF16)]
    zf_refs = (zff_ref, zfb_ref)
    for d in range(2):
        for hh in range(n_heads):
            st_scr[d, hh] = s0_ref[d, hh].T if has_s0 else jnp.zeros((HG_DV, HG_DK), F32)

    def body(i, carry):
        chains = []
        for hh in range(n_heads):
            lanes = slice(hh * w, (hh + 1) * w)
            for d in range(2):
                ci = (n_chunks - 1 - i) if d == 1 else i
                rows = pl.ds(ci * c, c) if isinstance(ci, int) else pl.ds(pl.multiple_of(ci * c, c), c)
                chains.append((d, hh, rows, lanes, q_scr[rows, lanes], zf_refs[d][rows, lanes],
                               zi_ref[rows, lanes], st_scr[d, hh]))
        outs = _scan_chunks([(q, zf, v, lower[d:d + 1, lanes], st, tris[d], d == 1)
                             for d, hh, rows, lanes, q, zf, v, st in chains], mask_scr)
        for (d, hh, rows, lanes, *_), (o, st_new) in zip(chains, outs):
            st_scr[d, hh] = st_new
            o_scr[d, rows, lanes] = o
        return carry

    if n_chunks <= 2:
        for i in range(n_chunks):
            body(i, 0)
    else:
        lax.fori_loop(0, n_chunks, body, 0)
    if emit_state:
        for d in range(2):
            for hh in range(n_heads):
                if has_alias:
                    st_ref[d, hh] = st_scr[d, hh].T
                else:
                    for layer in range(N_REC_LAYERS):
                        st_ref[layer, d, hh] = (st_scr[d, hh].T if layer == r_layer
                                                else jnp.zeros((HG_DK, HG_DV), F32))

    zg = zg_ref[...]
    gate = zg * _sigmoid(zg)
    for hh in range(n_heads):
        lanes = slice(hh * w, (hh + 1) * w)
        y = _rms(o_scr[0, :, lanes] + o_scr[1, :, lanes], gain_ref[:, lanes]) * gate[:, lanes]
        o_ref[:, lanes] = y.astype(o_ref.dtype)


def _scan(z, lb_logits, out_gain, r_layer, *, t_len, n_heads, s0=None, emit_state=False, state_out=None):
    nb = z.shape[0] // t_len
    nh = HG_HEADS
    sh = n_heads
    ng = nh // sh
    wb = sh * HG_DK
    col = lambda k: (lambda b, h: (b, k * ng + h))
    in_specs = [pl.BlockSpec((N_REC_LAYERS, 2, wb), lambda b, h: (0, 0, h))]
    mode = dict(pipeline_mode=pl.Buffered(1)) if t_len * wb * 4 >= SCAN_SINGLE_BUFFER_BYTES else {}
    in_specs += [pl.BlockSpec((t_len, wb), col(k), **mode) for k in range(5)]
    in_specs += [pl.BlockSpec((1, wb), lambda b, h: (0, h))]
    args = [lb_logits, z, z, z, z, z, out_gain]
    if s0 is not None:
        in_specs.append(pl.BlockSpec((None, None, 2, sh, HG_DK, HG_DV),
                                     lambda b, h: (b, r_layer, 0, h, 0, 0)))
        args.append(s0)
    out_specs = [pl.BlockSpec((t_len, wb), lambda b, h: (b, h))]
    out_shape = [jax.ShapeDtypeStruct((nb * t_len, nh * HG_DV), BF16)]
    aliases = {}
    if emit_state:
        if state_out is None:
            out_specs.append(pl.BlockSpec((None, N_REC_LAYERS, 2, sh, HG_DK, HG_DV), lambda b, h: (b, 0, 0, h, 0, 0)))
        else:
            out_specs.append(pl.BlockSpec((None, None, 2, sh, HG_DK, HG_DV), lambda b, h: (b, r_layer, 0, h, 0, 0)))
            aliases[len(args)] = 1
            in_specs.append(pl.BlockSpec(memory_space=pl.ANY))
            args.append(state_out)
        out_shape.append(jax.ShapeDtypeStruct((nb, N_REC_LAYERS, 2, nh, HG_DK, HG_DV), F32))
    kern = functools.partial(_scan_kernel, r_layer=r_layer, has_s0=s0 is not None, emit_state=emit_state,
                             has_alias=bool(aliases), n_heads=n_heads)
    return pl.pallas_call(
        kern,
        grid=(nb, ng),
        in_specs=in_specs,
        out_specs=out_specs,
        out_shape=out_shape,
        input_output_aliases=aliases,
        scratch_shapes=[pltpu.VMEM((t_len, wb), F32),
                        pltpu.VMEM((2, t_len, wb), F32),
                        pltpu.VMEM((2, sh, HG_DV, HG_DK), F32),
                        pltpu.VMEM((len(SCAN_LEVELS) + 2, SCAN_CHUNK, SCAN_CHUNK), F32)],
        name="hgrn_scan",
    )(*args)


def _one_hot_rows(posm, e0, g, cap):
    t_len = posm.shape[1]
    slot = lax.broadcasted_iota(jnp.int32, (cap, t_len), 0)
    parts = [jnp.where(posm[e:e + 1, :] == slot, 1.0, 0.0).astype(BF16) for e in range(e0, e0 + g)]
    return parts[0] if g == 1 else jnp.concatenate(parts, axis=0)


PREFIX_PIECE = 256


def _prefix_count(mask_f, before):
    pieces = []
    carry = jnp.zeros((mask_f.shape[0], 1), F32)
    for p0 in range(0, mask_f.shape[1], PREFIX_PIECE):
        piece = mask_f[:, p0:p0 + PREFIX_PIECE]
        pieces.append(_dot(piece.astype(BF16), before) + carry)
        carry = carry + jnp.sum(piece, axis=1, keepdims=True)
    return pieces[0] if len(pieces) == 1 else jnp.concatenate(pieces, axis=1)


def _sort_desc_segments(x, seg):
    w = x.shape[1]
    lane = lax.broadcasted_iota(jnp.int32, x.shape, 1)
    k = 2
    while k <= seg:
        j = k // 2
        while j >= 1:
            is_first = (lane & j) == 0
            partner = jnp.where(is_first, pltpu.roll(x, w - j, axis=1), pltpu.roll(x, j, axis=1))
            keep_max = is_first if k == seg else (is_first == ((lane & k) == 0))
            x = jnp.where(keep_max, jnp.maximum(x, partner), jnp.minimum(x, partner))
            j //= 2
        k *= 2
    return x


def _select_kernel(aff_ref, h_ref, g_ref, gate_ref, pos_ref, *, t_len):
    a = aff_ref[...]
    ne, w = a.shape
    cap = EC_CAPACITY * t_len // N_EXPERTS
    capf = jnp.float32(cap)
    srt = jnp.concatenate([_sort_desc_segments(a[r0:r0 + SUBLANES, :], t_len)
                           for r0 in range(0, ne, SUBLANES)], axis=0)
    src = lax.broadcasted_iota(jnp.int32, (PREFIX_PIECE, PREFIX_PIECE), 0)
    dst = lax.broadcasted_iota(jnp.int32, (PREFIX_PIECE, PREFIX_PIECE), 1)
    before = jnp.where(src < dst, 1.0, 0.0).astype(BF16)
    slot = lax.broadcasted_iota(jnp.int32, (cap, t_len), 0)
    g = max(1, min(ne, 512 // cap))
    for s in range(w // t_len):
        t0 = s * t_len
        c0 = s * cap
        seg = a[:, t0:t0 + t_len]
        thr = srt[:, t0 + cap - 1:t0 + cap]
        gt = jnp.where(seg > thr, 1.0, 0.0)
        eq = jnp.where(seg == thr, 1.0, 0.0)
        need = capf - jnp.sum(gt, axis=1, keepdims=True)
        eq_rank = _prefix_count(eq, before)
        sel = gt + eq * jnp.where(eq_rank < need, 1.0, 0.0)
        pos = _prefix_count(sel, before)
        posm = jnp.where(sel > 0.5, pos.astype(jnp.int32), -1)
        pos_ref[:, t0:t0 + t_len] = posm
        h = h_ref[t0:t0 + t_len, :]
        for e0 in range(0, ne, g):
            rows = _dot(_one_hot_rows(posm, e0, g, cap), h)
            g_ref[e0:e0 + g, c0:c0 + cap, :] = rows.reshape(g, cap, h.shape[1]).astype(g_ref.dtype)
        for e in range(ne):
            gate_ref[e, c0:c0 + cap, :] = jnp.sum(jnp.where(posm[e:e + 1, :] == slot, seg[e:e + 1, :], 0.0),
                                                  axis=1, keepdims=True)


def _select(aff_t, h, *, t_len):
    ne, n = aff_t.shape
    d = h.shape[1]
    w = SELECT_BLOCK
    capw = EC_CAPACITY * w // N_EXPERTS
    kern = functools.partial(_select_kernel, t_len=t_len)
    return pl.pallas_call(
        kern,
        grid=(n // w,),
        in_specs=[pl.BlockSpec((ne, w), lambda s: (0, s)),
                  pl.BlockSpec((w, d), lambda s: (s, 0))],
        out_specs=[pl.BlockSpec((ne, capw, d), lambda s: (0, s, 0)),
                   pl.BlockSpec((ne, capw, 1), lambda s: (0, s, 0)),
                   pl.BlockSpec((ne, w), lambda s: (0, s))],
        out_shape=[jax.ShapeDtypeStruct((ne, n // w * capw, d), BF16),
                   jax.ShapeDtypeStruct((ne, n // w * capw, 1), F32),
                   jax.ShapeDtypeStruct((ne, n), jnp.int32)],
        name="ec_select",
    )(aff_t, h)


def _ffn_kernel(xc_ref, xl_ref, gc_ref, gl_ref, wg_ref, wu_ref, wd_ref, yc_ref, yl_ref, accc, accl):
    f = pl.program_id(1)
    chunks = [(x_ref, acc, r0) for x_ref, acc in ((xc_ref, accc), (xl_ref, accl))
              for r0 in range(0, x_ref.shape[0], FFN_ROWS)]

    @pl.when(f == 0)
    def _():
        accc[...] = jnp.zeros_like(accc)
        accl[...] = jnp.zeros_like(accl)

    w = {}

    def weight(name, ref):
        if name not in w:
            w[name] = ref[...].astype(BF16)
        return w[name]

    def up(x_ref, acc, r0):
        x = x_ref[r0:r0 + FFN_ROWS, :]
        hg = _dot(x, weight("g", wg_ref))
        return hg, _dot(x, weight("u", wu_ref))

    def down(x_ref, acc, r0, hg, hu):
        hid = (hg * _sigmoid(hg) * hu).astype(BF16)
        acc[r0:r0 + FFN_ROWS, :] = acc[r0:r0 + FFN_ROWS, :] + _dot(hid, weight("d", wd_ref))

    pending = None
    for ch in chunks:
        cur = up(*ch)
        if pending is not None:
            down(*pending)
        pending = ch + cur
    down(*pending)

    @pl.when(f == pl.num_programs(1) - 1)
    def _():
        yc_ref[...] = (accc[...] * gc_ref[...]).astype(yc_ref.dtype)
        yl_ref[...] = (accl[...] * gl_ref[...]).astype(yl_ref.dtype)


def _ffn(xc, xl, gc, gl, w_gate, w_up, w_down, layer, *, bf):
    ne, nc, d = xc.shape
    nl = xl.shape[1]
    ff = w_gate.shape[3]
    per_e = lambda e, f: (e, 0, 0)
    return pl.pallas_call(
        _ffn_kernel,
        grid=(ne, ff // bf),
        in_specs=[pl.BlockSpec((None, nc, d), per_e),
                  pl.BlockSpec((None, nl, d), per_e),
                  pl.BlockSpec((None, nc, 1), per_e),
                  pl.BlockSpec((None, nl, 1), per_e),
                  pl.BlockSpec((None, None, d, bf), lambda e, f: (layer, e, 0, f)),
                  pl.BlockSpec((None, None, d, bf), lambda e, f: (layer, e, 0, f)),
                  pl.BlockSpec((None, None, bf, d), lambda e, f: (layer, e, f, 0))],
        out_specs=[pl.BlockSpec((None, nc, d), per_e),
                   pl.BlockSpec((None, nl, d), per_e)],
        out_shape=[jax.ShapeDtypeStruct((ne, nc, d), BF16),
                   jax.ShapeDtypeStruct((ne, nl, d), BF16)],
        scratch_shapes=[pltpu.VMEM((nc, d), F32), pltpu.VMEM((nl, d), F32)],
        name="expert_ffn",
    )(xc, xl, gc, gl, w_gate, w_up, w_down)


def _combine_kernel(pos_ref, y_ref, x_ref, m_ref, fn_ref, o_ref, *, cap, sets):
    ne = pos_ref.shape[0]
    d = x_ref.shape[1]
    tt = x_ref.shape[0] // sets
    g = max(1, min(ne, 512 // cap))
    for s in range(sets):
        posm = pos_ref[:, s * tt:(s + 1) * tt]
        acc = jnp.zeros((tt, d), F32)
        for e0 in range(0, ne, g):
            acc = acc + _dot_tn(_one_hot_rows(posm, e0, g, cap),
                                y_ref[e0:e0 + g, s * cap:(s + 1) * cap, :].reshape(g * cap, d))
        x = x_ref[s * tt:(s + 1) * tt, :] + m_ref[5:6, :] * acc
        o_ref[s * tt:(s + 1) * tt, :] = x if fn_ref is None else _rms(x, fn_ref[...])


def _combine(posm, y, x, mods, *, t_len, tt, base, sets=1, final_gain=None):
    ne, n = posm.shape
    d = x.shape[1]
    ns = n // t_len
    cap = y.shape[1] // ns
    nt = t_len // tt
    assert sets == 1 or nt == 1
    in_specs = [pl.BlockSpec((ne, sets * tt), lambda s, j: (0, s * nt + j)),
                pl.BlockSpec((ne, sets * cap, d), lambda s, j: (0, s, 0)),
                pl.BlockSpec((sets * tt, d), lambda s, j: (s * nt + j, 0)),
                pl.BlockSpec((None, 6, d), lambda s, j: (base(s), 0, 0))]
    args = [posm, y, x, mods]
    if final_gain is None:
        kern = lambda p, yr, xr, m, o: _combine_kernel(p, yr, xr, m, None, o, cap=cap, sets=sets)
    else:
        kern = functools.partial(_combine_kernel, cap=cap, sets=sets)
        in_specs.append(pl.BlockSpec((1, d), lambda s, j: (0, 0)))
        args.append(final_gain)
    return pl.pallas_call(
        kern,
        grid=(ns // sets, nt),
        in_specs=in_specs,
        out_specs=pl.BlockSpec((sets * tt, d), lambda s, j: (s * nt + j, 0)),
        out_shape=jax.ShapeDtypeStruct((n, d), F32),
        name="ec_combine",
    )(*args)


def _attn_in_weight(w):
    o_kr = MLA_Q_LORA + MLA_KV_LORA
    o_qg = o_kr + MLA_ROPE
    cols = jnp.concatenate([w[:, :o_kr], w[:, o_qg:], jnp.zeros((w.shape[0], KR_OFF), w.dtype), w[:, o_kr:o_qg]],
                           axis=1)
    return jnp.pad(cols, ((0, 0), (0, ATTN_Z - cols.shape[1]))).astype(BF16)


def _heads_split(w, n_heads, first, second):
    k = w.shape[0]
    w3 = w.reshape(k, n_heads, first + second)
    return jnp.concatenate([w3[:, :, :first].reshape(k, n_heads * first),
                            w3[:, :, first:].reshape(k, n_heads * second)], axis=1).astype(BF16)


def _head_slots(w, n_heads, lo, hi):
    k = w.shape[0]
    w3 = w.reshape(k, n_heads, -1)[:, :, lo:hi]
    return jnp.pad(w3, ((0, 0), (0, 0), (0, LANES - (hi - lo)))).reshape(k, n_heads * LANES)


def _mla_weights(w_uq, w_ukv):
    k = w_ukv.shape[0]
    v_cols = w_ukv.reshape(k, MLA_HEADS, MLA_NOPE + MLA_V)[:, :, MLA_NOPE:].reshape(k, VM_W)
    return (_head_slots(w_uq, MLA_HEADS, 0, MLA_NOPE + MLA_ROPE).astype(BF16),
            jnp.concatenate([_head_slots(w_ukv, MLA_HEADS, 0, MLA_NOPE), v_cols], axis=1).astype(BF16))


def _slot_tables(cos, sin):
    n = cos.shape[0]
    pad = LANES - KR_OFF - MLA_ROPE
    cos = jnp.concatenate([jnp.ones((n, KR_OFF), F32), cos, jnp.ones((n, pad), F32)], axis=1)
    sin = jnp.concatenate([jnp.zeros((n, KR_OFF), F32), sin, jnp.zeros((n, pad), F32)], axis=1)
    return jnp.tile(cos, (1, MLA_HEADS)), jnp.tile(sin, (1, MLA_HEADS))


def _rope_tables(width, n_rep):
    half = width // 2
    t = jnp.arange(DEC_SEQ)
    rows = (t // GRID_W).astype(F32)
    cols = (t % GRID_W).astype(F32)
    inv_freq = ROPE_BASE ** (-jnp.arange(0, half, 2, dtype=F32) / half)
    ar = rows[:, None] * inv_freq
    ac = cols[:, None] * inv_freq
    cos = jnp.concatenate([jnp.cos(ar), jnp.cos(ar), jnp.cos(ac), jnp.cos(ac)], axis=1)
    sin = jnp.concatenate([-jnp.sin(ar), jnp.sin(ar), -jnp.sin(ac), jnp.sin(ac)], axis=1)
    return jnp.tile(cos, (1, n_rep)), jnp.tile(sin, (1, n_rep))


def kernel(x_prompt, x_sample, c, cache_mla_ckv, cache_mla_krope, cache_gqa_k, cache_gqa_v, state_hgrn,
           c_ctx, norm1, norm2, w_mod, b_mod, w_attn_in, mla_q_norm, w_mla_uq, mla_kv_norm, w_mla_ukv,
           gqa_sink, w_attn_out, w_rec_in, rec_lb_logits, rec_out_norm, w_rec_out, w_router, w_gate,
           w_up, w_down, final_norm):
    d = D_MODEL
    xc = x_prompt.reshape(BATCH * SEQ, d)
    xl = x_sample.reshape(DEC_BATCH * DEC_SEQ, d)
    cvecs = jnp.concatenate([c_ctx[None, :], c, jnp.zeros((N_MOD_ROWS - N_MOD_GROUPS, d), F32)], axis=0)
    mods_all = _mod_vectors(cvecs, w_mod, b_mod).reshape(DEPTH, N_MOD_ROWS, 6, d)[:, :N_MOD_GROUPS]
    grp_c = _group_map(None, 0, 0)
    cm, sm = _slot_tables(*_rope_tables(MLA_ROPE, 1))
    cg, sg = _rope_tables(GQA_HD, GQA_HEADS)
    kvw = GQA_KV_HEADS * GQA_HD

    caches, st_all = None, None
    for layer in range(DEPTH):
        mods = mods_all[layer]
        g1 = norm1[layer].reshape(1, d)
        g2 = norm2[layer].reshape(1, d)
        if layer % 2 == 0:
            a = layer // 2
            w_in = _attn_in_weight(w_attn_in[a])
            w_uq, w_ukv = _mla_weights(w_mla_uq[a], w_mla_ukv[a])
            w_out = w_attn_out[a].astype(BF16)
            qn_g = mla_q_norm[a].reshape(1, -1)
            kvn_g = mla_kv_norm[a].reshape(1, -1)
            sink = gqa_sink[a]
            zc = _modproj(xc, g1, mods, w_in, sub=0, bm=512, grp=grp_c)
            zl = _modproj(xl, g1, mods, w_in, sub=0, bm=512, grp=_group_map(DEC_SEQ, 512, 1))
            oc, *caches = _attn_ctx(zc, sink, qn_g, _heads_split(w_mla_uq[a], MLA_HEADS, MLA_NOPE, MLA_ROPE), kvn_g,
                                    _heads_split(w_mla_ukv[a], MLA_HEADS, MLA_NOPE, MLA_V), a, caches)
            qcat, kcat, vm, qg, kg, vg = _lat_prep(zl, qn_g, w_uq, kvn_g, w_ukv, cm, sm, cg, sg, bm=256)
            kr_c = jnp.pad(cache_mla_krope[:, a], ((0, 0), (0, 0), (KR_OFF, LANES - KR_OFF - MLA_ROPE)))
            kcat_c, vm_c = _cache_kv(cache_mla_ckv, kr_c, a, w_ukv)
            per_b = lambda x: x.reshape(DEC_BATCH, DEC_SEQ, -1)
            cat = lambda lat, ctx: jnp.concatenate([per_b(lat), ctx], axis=1)
            band = lambda x: jnp.pad(per_b(x), ((0, 0), (BAND_BLOCK, BAND_BLOCK), (0, 0)))
            ol = _lat_attn(sink, qcat, cat(kcat, kcat_c), cat(vm, vm_c), qg, band(kg), band(vg),
                           cache_gqa_k[:, a].reshape(DEC_BATCH, PAST_LEN, kvw).astype(BF16),
                           cache_gqa_v[:, a].reshape(DEC_BATCH, PAST_LEN, kvw).astype(BF16), qb=LAT_QB)
        else:
            r = layer // 2
            w_in = w_rec_in[r].astype(BF16)
            w_out = w_rec_out[r].astype(BF16)
            og = rec_out_norm[r].reshape(1, -1)
            zc = _modproj(xc, g1, mods, w_in, sub=0, bm=512, grp=grp_c)
            zl = _modproj(xl, g1, mods, w_in, sub=0, bm=512, grp=_group_map(DEC_SEQ, 512, 1))
            oc, st_all = _scan(zc, rec_lb_logits, og, r, t_len=SEQ, n_heads=8, emit_state=True, state_out=st_all)
            (ol,) = _scan(zl, rec_lb_logits, og, r, t_len=DEC_SEQ, n_heads=4, s0=state_hgrn)
        wr_t = w_router[layer].T.astype(BF16)
        xc, hc, aff_c = _outproj_router(oc, w_out, xc, mods, g2, wr_t, bm=512, grp=grp_c)
        xl, hl, aff_l = _outproj_router(ol, w_out, xl, mods, g2, wr_t, bm=512, grp=_group_map(DEC_SEQ, 512, 1))
        gxc, gate_c, pos_c = _select(aff_c, hc, t_len=SEQ)
        gxl, gate_l, pos_l = _select(aff_l, hl, t_len=DEC_SEQ)
        yc, yl = _ffn(gxc, gxl, gate_c, gate_l, w_gate, w_up, w_down, layer, bf=512)
        fn = final_norm.reshape(1, d) if layer == DEPTH - 1 else None
        xc = _combine(pos_c, yc, xc, mods, t_len=SEQ, tt=SEQ, base=lambda s: 0, sets=COMBINE_SETS, final_gain=fn)
        xl = _combine(pos_l, yl, xl, mods, t_len=DEC_SEQ, tt=512, base=lambda s: 1 + s, final_gain=fn)

    y_prompt = xc.reshape(BATCH, SEQ, d)
    y_sample = xl.reshape(DEC_BATCH, DEC_SEQ, d)
    ckv_all, kr_all, kg_all, vg_all = caches
    heads = lambda x: x.reshape(BATCH, N_ATTN_LAYERS, SEQ, GQA_KV_HEADS, GQA_HD)
    return (y_prompt, y_sample, ckv_all, kr_all, heads(kg_all), heads(vg_all), st_all)
```

```python
import functools

import jax
import jax.numpy as jnp
from jax import lax
from jax.experimental import pallas as pl
from jax.experimental.pallas import tpu as pltpu

F32 = jnp.float32
BF16 = jnp.bfloat16

D_MODEL = 1024
BATCH = 32
SEQ = 256
DEPTH = 4
DEC_BATCH = 2
DEC_SEQ = 2048
PAST_LEN = 512
GRID_W = 64
ROPE_BASE = 10000.0
NORM_EPS = 1e-6
NEG_BIG = -1e30
F_MIN = 1e-6
N_ATTN_LAYERS = (DEPTH + 1) // 2
N_REC_LAYERS = DEPTH // 2
MLA_HEADS = 8
MLA_Q_LORA = 384
MLA_KV_LORA = 256
MLA_NOPE = 64
MLA_ROPE = 32
MLA_V = 64
MLA_SCALE = (MLA_NOPE + MLA_ROPE) ** -0.5
GQA_HEADS = 8
GQA_KV_HEADS = 2
GQA_GROUP = GQA_HEADS // GQA_KV_HEADS
GQA_HD = 64
GQA_SCALE = GQA_HD ** -0.5
WINDOW = 128
BAND_BLOCK = 128
HG_HEADS = 8
HG_DK = 128
HG_DV = 128
REC_IN = 3 * HG_HEADS * HG_DK + 2 * HG_HEADS * HG_DV
N_EXPERTS = 16
EC_CAPACITY = 2
D_FF = 2048

LANES = 128
N_MOD_ROWS = 8
N_MOD_GROUPS = 1 + DEC_BATCH

ZC_Q = 0
ZC_KV = ZC_Q + MLA_Q_LORA
ZC_QG = ZC_KV + MLA_KV_LORA
ZC_KG = ZC_QG + GQA_HEADS * GQA_HD
ZC_VG = ZC_KG + GQA_KV_HEADS * GQA_HD
ZC_KR = ZC_VG + GQA_KV_HEADS * GQA_HD
ATTN_Z = ZC_KR + LANES
KR_OFF = MLA_NOPE
QK_W = MLA_HEADS * LANES
VM_W = MLA_HEADS * MLA_V
O_M_W = MLA_HEADS * MLA_V
O_G_W = GQA_HEADS * GQA_HD

SUBLANES = 8
SCAN_CHUNK = 128
SCAN_LEVELS = (64, 32, 16, 8)
SCAN_DIAG = 8
SELECT_BLOCK = 2048
FFN_ROWS = 256
LAT_QB = 512
CTX_REQS = 4
COMBINE_SETS = 4
SCAN_SINGLE_BUFFER_BYTES = 4 << 20


def _dot(a, b):
    return jnp.dot(a, b, preferred_element_type=F32)


def _dot_nt(a, b):
    return lax.dot_general(a, b, (((1,), (1,)), ((), ())), preferred_element_type=F32)


def _dot_tn(a, b):
    return lax.dot_general(a, b, (((0,), (0,)), ((), ())), preferred_element_type=F32)


def _sigmoid(x):
    return 1.0 / (1.0 + jnp.exp(-x))


def _rms(x, gain):
    return x * lax.rsqrt(_row_sum(x * x) * (1.0 / x.shape[-1]) + NORM_EPS) * gain


def _mod_kernel(c_ref, w_ref, b_ref, o_ref):
    c = c_ref[...]
    s = (c * _sigmoid(c)).astype(BF16)
    o_ref[...] = _dot(s, w_ref[...].astype(BF16)) + b_ref[...]


def _mod_vectors(cvecs, w_mod, b_mod):
    bn = 1536
    nd = 6 * D_MODEL
    return pl.pallas_call(
        _mod_kernel,
        grid=(DEPTH, nd // bn),
        in_specs=[pl.BlockSpec((N_MOD_ROWS, D_MODEL), lambda l, j: (0, 0)),
                  pl.BlockSpec((None, D_MODEL, bn), lambda l, j: (l, 0, j)),
                  pl.BlockSpec((None, 1, bn), lambda l, j: (l, 0, j))],
        out_specs=pl.BlockSpec((None, N_MOD_ROWS, bn), lambda l, j: (l, 0, j)),
        out_shape=jax.ShapeDtypeStruct((DEPTH, N_MOD_ROWS, nd), F32),
        name="mod_vectors",
    )(cvecs, w_mod, b_mod.reshape(DEPTH, 1, nd))


def _group_map(rows_per_group, bm, base):
    if rows_per_group is None:
        return lambda i: base
    return lambda i: base + (i * bm) // rows_per_group


def _modproj_kernel(x_ref, g_ref, m_ref, w_ref, o_ref, *, shift_row, chunk):
    h = _rms(x_ref[...], g_ref[...])
    h = (h * (1.0 + m_ref[shift_row + 1:shift_row + 2, :]) + m_ref[shift_row:shift_row + 1, :]).astype(BF16)
    nout = o_ref.shape[1]
    for j in range(0, nout, chunk):
        o_ref[:, j:j + chunk] = _dot(h, w_ref[:, j:j + chunk])


def _modproj(x, gain, mods, w, *, sub, bm, grp):
    n, d = x.shape
    nout = w.shape[1]
    chunk = nout if nout <= 1536 else 1024
    kern = functools.partial(_modproj_kernel, shift_row=3 * sub, chunk=chunk)
    return pl.pallas_call(
        kern,
        grid=(n // bm,),
        in_specs=[pl.BlockSpec((bm, d), lambda i: (i, 0)),
                  pl.BlockSpec((1, d), lambda i: (0, 0)),
                  pl.BlockSpec((None, 6, d), lambda i: (grp(i), 0, 0)),
                  pl.BlockSpec((d, nout), lambda i: (0, 0), pipeline_mode=pl.Buffered(1))],
        out_specs=pl.BlockSpec((bm, nout), lambda i: (i, 0)),
        out_shape=jax.ShapeDtypeStruct((n, nout), F32),
        name="modproj",
    )(x, gain, mods, w)


def _outproj_router_kernel(a_ref, w_ref, x_ref, m_ref, g_ref, wr_ref, x_out, h_out, aff_out):
    x = x_ref[...] + m_ref[2:3, :] * _dot(a_ref[...], w_ref[...])
    x_out[...] = x
    h = (_rms(x, g_ref[...]) * (1.0 + m_ref[4:5, :]) + m_ref[3:4, :]).astype(BF16)
    h_out[...] = h
    logits = _dot_nt(wr_ref[...], h)
    e = jnp.exp(logits - jnp.max(logits, axis=0, keepdims=True))
    aff_out[...] = e / jnp.sum(e, axis=0, keepdims=True)


def _outproj_router(a, w, x, mods, gain2, wr_t, *, bm, grp):
    n, d = x.shape
    k = a.shape[1]
    row = lambda i: (i, 0)
    const = lambda i: (0, 0)
    return pl.pallas_call(
        _outproj_router_kernel,
        grid=(n // bm,),
        in_specs=[pl.BlockSpec((bm, k), row),
                  pl.BlockSpec((k, d), const),
                  pl.BlockSpec((bm, d), row),
                  pl.BlockSpec((None, 6, d), lambda i: (grp(i), 0, 0)),
                  pl.BlockSpec((1, d), const),
                  pl.BlockSpec((N_EXPERTS, d), const)],
        out_specs=[pl.BlockSpec((bm, d), row),
                   pl.BlockSpec((bm, d), row),
                   pl.BlockSpec((N_EXPERTS, bm), lambda i: (0, i))],
        out_shape=[jax.ShapeDtypeStruct((n, d), F32),
                   jax.ShapeDtypeStruct((n, d), BF16),
                   jax.ShapeDtypeStruct((N_EXPERTS, n), F32)],
        name="outproj_router",
    )(a, w, x, mods, gain2, wr_t)


def _lane_fold(x, op):
    w = x.shape[1]
    if w <= LANES or w % LANES:
        return x
    acc = x[:, :LANES]
    for j in range(1, w // LANES):
        acc = op(acc, x[:, j * LANES:(j + 1) * LANES])
    return acc


def _row_max(s):
    return jnp.max(_lane_fold(s, jnp.maximum), axis=-1, keepdims=True)


def _row_sum(p):
    return jnp.sum(_lane_fold(p, jnp.add), axis=-1, keepdims=True)


def _softmax_pv(s, v, sink=None):
    m = _row_max(s)
    if sink is not None:
        m = jnp.maximum(m, sink)
    p = jnp.exp(s - m)
    l = _row_sum(p)
    if sink is not None:
        l = l + jnp.exp(sink - m)
    return _dot(p.astype(BF16), v) / l


def _swap_pairs(x, q):
    w = x.shape[1]
    lane = lax.broadcasted_iota(jnp.int32, x.shape, 1)
    first = (lane % (2 * q)) < q
    return jnp.where(first, pltpu.roll(x, w - q, axis=1), pltpu.roll(x, q, axis=1))


def _rope(x, cos, sin, q):
    return x * cos + _swap_pairs(x, q) * sin


def _attn_project(z, qn_ref, wuq_ref, kvn_ref, wukv_ref):
    cq = _rms(z[:, ZC_Q:ZC_Q + MLA_Q_LORA], qn_ref[...]).astype(BF16)
    qcat = _dot(cq, wuq_ref[...])
    ckv_n = _rms(z[:, ZC_KV:ZC_KV + MLA_KV_LORA], kvn_ref[...])
    kv = _dot(ckv_n.astype(BF16), wukv_ref[...])
    return qcat, ckv_n, kv


def _tile_heads(kr_slot):
    return jnp.concatenate([kr_slot] * MLA_HEADS, axis=1)


def _staggered(n, first, second, depth):
    pending = []
    for i in range(n):
        pending.append((i, first(i)))
        if len(pending) > depth:
            second(*pending.pop(0))
    for item in pending:
        second(*item)


def _gqa_sink_column(sink_ref, g, rows):
    return jnp.concatenate([jnp.full((rows, 1), sink_ref[g * GQA_GROUP + j], F32) for j in range(GQA_GROUP)], axis=0)


def _stack_group(qg, g):
    return jnp.concatenate([qg[:, (g * GQA_GROUP + j) * GQA_HD:(g * GQA_GROUP + j + 1) * GQA_HD]
                            for j in range(GQA_GROUP)], axis=0)


def _attn_ctx_kernel(*refs, n_alias, a):
    sink_ref, z_ref, qn_ref, wuq_ref, kvn_ref, wukv_ref = refs[:6]
    o_ref, ckv_ref, kr_ref, kg_ref, vg_ref = refs[6 + n_alias:]
    t_len = SEQ
    z = z_ref[...]
    cq = _rms(z[:, ZC_Q:ZC_Q + MLA_Q_LORA], qn_ref[...]).astype(BF16)
    qm = _dot(cq, wuq_ref[...])
    ckv_n = _rms(z[:, ZC_KV:ZC_KV + MLA_KV_LORA], kvn_ref[...])
    kv = _dot(ckv_n.astype(BF16), wukv_ref[...])
    kr = z[:, ZC_KR + KR_OFF:ZC_KR + KR_OFF + MLA_ROPE]
    kg = z[:, ZC_KG:ZC_KG + GQA_KV_HEADS * GQA_HD]
    vg = z[:, ZC_VG:ZC_VG + GQA_KV_HEADS * GQA_HD]
    reqs = [slice(r * t_len, (r + 1) * t_len) for r in range(CTX_REQS)]
    for ref, val in ((ckv_ref, ckv_n), (kr_ref, kr), (kg_ref, kg), (vg_ref, vg)):
        for r, rows in enumerate(reqs):
            if n_alias:
                ref[r] = val[rows]
            else:
                for layer in range(N_ATTN_LAYERS):
                    ref[r, layer] = val[rows] if layer == a else jnp.zeros_like(val[rows])
    qn_w = MLA_HEADS * MLA_NOPE
    n_rows = z.shape[0]
    lane = lax.broadcasted_iota(jnp.int32, (n_rows, LANES), 1)
    low_half = lax.broadcasted_iota(jnp.int32, (t_len, LANES), 1) < MLA_V
    qn = qm[:, :qn_w] * MLA_SCALE
    qr = qm[:, qn_w:] * MLA_SCALE
    kn = kv[:, :qn_w].astype(BF16)
    vm = kv[:, qn_w:].astype(BF16)
    kr_slot = z[:, ZC_KR:ZC_KR + LANES]
    kr_rep = kr_slot
    for i in range(1, LANES // MLA_ROPE):
        kr_rep = kr_rep + pltpu.roll(kr_slot, i * MLA_ROPE, axis=1)
    kr_rep = kr_rep.astype(BF16)
    nope_per, rope_per = LANES // MLA_NOPE, LANES // MLA_ROPE
    held = {}

    def finish(h, r, rows, o, col0):
        tile = slice(col0 + h // 2 * LANES, col0 + (h // 2 + 1) * LANES)
        if h % 2 == 0:
            held[r] = o
        else:
            o_ref[rows, tile] = jnp.where(low_half, held.pop(r), o).astype(o_ref.dtype)

    for h in range(MLA_HEADS):
        nt = slice(h // nope_per * LANES, (h // nope_per + 1) * LANES)
        rt = slice(h // rope_per * LANES, (h // rope_per + 1) * LANES)
        q_n = jnp.where(lane // MLA_NOPE == h % nope_per, qn[:, nt], 0.0)
        q_r = jnp.where(lane // MLA_ROPE == h % rope_per, qr[:, rt], 0.0)
        q2 = jnp.concatenate([q_n, q_r], axis=1).astype(BF16)
        k2 = jnp.concatenate([kn[:, nt], kr_rep], axis=1)
        ss = [_dot_nt(q2[rows], k2[rows]) for rows in reqs]
        for r, (rows, s) in enumerate(zip(reqs, ss)):
            finish(h, r, rows, _softmax_pv(s, vm[rows, nt]), 0)
    qg = z[:, ZC_QG:ZC_QG + O_G_W] * GQA_SCALE
    kgs = [kg.astype(BF16), pltpu.roll(kg, GQA_HD, axis=1).astype(BF16)]
    vgs = [vg.astype(BF16), pltpu.roll(vg, GQA_HD, axis=1).astype(BF16)]
    for h in range(GQA_HEADS):
        g = h // GQA_GROUP
        par = h % 2
        tile = slice(h // 2 * LANES, (h // 2 + 1) * LANES)
        q_h = jnp.where(lane // GQA_HD == par, qg[:, tile], 0.0).astype(BF16)
        k_h = kgs[0 if par == g else 1]
        v_h = vgs[0 if par == g else 1]
        ss = [_dot_nt(q_h[rows], k_h[rows]) for rows in reqs]
        for r, (rows, s) in enumerate(zip(reqs, ss)):
            finish(h, r, rows, _softmax_pv(s, v_h[rows], sink_ref[h]), O_M_W)


def _attn_ctx(z, sink, q_norm, w_uq, kv_norm, w_ukv, a, caches=None):
    nb = z.shape[0] // SEQ
    rq = CTX_REQS
    kvw = GQA_KV_HEADS * GQA_HD
    const = lambda b: (0, 0)
    row = lambda b: (b, 0)
    cache_w = (MLA_KV_LORA, MLA_ROPE, kvw, kvw)
    in_specs = [pl.BlockSpec(memory_space=pltpu.SMEM),
                pl.BlockSpec((rq * SEQ, ATTN_Z), row),
                pl.BlockSpec((1, MLA_Q_LORA), const),
                pl.BlockSpec(w_uq.shape, const),
                pl.BlockSpec((1, MLA_KV_LORA), const),
                pl.BlockSpec(w_ukv.shape, const)]
    args = [sink, z, q_norm, w_uq, kv_norm, w_ukv]
    aliases = {}
    if caches is not None:
        for j, c in enumerate(caches):
            aliases[len(args)] = 1 + j
            in_specs.append(pl.BlockSpec(memory_space=pl.ANY))
            args.append(c)
    kern = functools.partial(_attn_ctx_kernel, n_alias=len(aliases), a=a)
    if aliases:
        cache_specs = [pl.BlockSpec((rq, None, SEQ, w), lambda b: (b, a, 0, 0)) for w in cache_w]
    else:
        cache_specs = [pl.BlockSpec((rq, N_ATTN_LAYERS, SEQ, w), lambda b: (b, 0, 0, 0)) for w in cache_w]
    return pl.pallas_call(
        kern,
        grid=(nb // rq,),
        in_specs=in_specs,
        out_specs=[pl.BlockSpec((rq * SEQ, O_M_W + O_G_W), row)] + cache_specs,
        out_shape=[jax.ShapeDtypeStruct((nb * SEQ, O_M_W + O_G_W), BF16)]
        + [jax.ShapeDtypeStruct((nb, N_ATTN_LAYERS, SEQ, w), F32) for w in cache_w],
        input_output_aliases=aliases,
        name="attn_ctx",
    )(*args)


def _lat_prep_kernel(z_ref, qn_ref, wuq_ref, kvn_ref, wukv_ref, cm_ref, sm_ref, cg_ref, sg_ref,
                     qcat_o, kcat_o, vm_o, qg_o, kg_o, vg_o):
    z = z_ref[...]
    qcat, ckv_n, kv = _attn_project(z, qn_ref, wuq_ref, kvn_ref, wukv_ref)
    cm = _tile_heads(cm_ref[...])
    sm = _tile_heads(sm_ref[...])
    cg = jnp.concatenate([cg_ref[...]] * (O_G_W // LANES), axis=1)
    sg = jnp.concatenate([sg_ref[...]] * (O_G_W // LANES), axis=1)
    qcat_o[...] = (_rope(qcat, cm, sm, MLA_ROPE // 4) * MLA_SCALE).astype(BF16)
    kr_slot = _rope(z[:, ZC_KR:ZC_KR + LANES], cm[:, :LANES], sm[:, :LANES], MLA_ROPE // 4)
    kcat_o[...] = (kv[:, :QK_W] + _tile_heads(kr_slot)).astype(BF16)
    vm_o[...] = kv[:, QK_W:].astype(BF16)
    qg_o[...] = (_rope(z[:, ZC_QG:ZC_QG + O_G_W], cg, sg, GQA_HD // 4) * GQA_SCALE).astype(BF16)
    kvw = GQA_KV_HEADS * GQA_HD
    kg_o[...] = _rope(z[:, ZC_KG:ZC_KG + kvw], cg[:, :kvw], sg[:, :kvw], GQA_HD // 4).astype(BF16)
    vg_o[...] = z[:, ZC_VG:ZC_VG + kvw].astype(BF16)


def _lat_prep(z, q_norm, w_uq, kv_norm, w_ukv, cm, sm, cg, sg, *, bm):
    n = z.shape[0]
    per = DEC_SEQ // bm
    kvw = GQA_KV_HEADS * GQA_HD
    const = lambda i: (0, 0)
    row = lambda i: (i, 0)
    pos = lambda i: (i % per, 0)
    widths = (QK_W, QK_W, VM_W, O_G_W, kvw, kvw)
    return pl.pallas_call(
        _lat_prep_kernel,
        grid=(n // bm,),
        in_specs=[pl.BlockSpec((bm, ATTN_Z), row),
                  pl.BlockSpec((1, MLA_Q_LORA), const),
                  pl.BlockSpec(w_uq.shape, const),
                  pl.BlockSpec((1, MLA_KV_LORA), const),
                  pl.BlockSpec(w_ukv.shape, const),
                  pl.BlockSpec((bm, LANES), pos),
                  pl.BlockSpec((bm, LANES), pos),
                  pl.BlockSpec((bm, LANES), pos),
                  pl.BlockSpec((bm, LANES), pos)],
        out_specs=[pl.BlockSpec((bm, w), row) for w in widths],
        out_shape=[jax.ShapeDtypeStruct((n, w), BF16) for w in widths],
        name="lat_prep",
    )(z, q_norm, w_uq, kv_norm, w_ukv, cm, sm, cg, sg)


def _cache_kv_kernel(c_ref, kr_ref, w_ref, kcat_o, vm_o):
    kv = _dot(c_ref[...].astype(BF16), w_ref[...])
    kcat_o[...] = (kv[:, :QK_W] + _tile_heads(kr_ref[...])).astype(BF16)
    vm_o[...] = kv[:, QK_W:].astype(BF16)


def _cache_kv(cache_ckv, kr_slot, a, w_ukv):
    nb = cache_ckv.shape[0]
    return pl.pallas_call(
        _cache_kv_kernel,
        grid=(nb,),
        in_specs=[pl.BlockSpec((None, None, PAST_LEN, MLA_KV_LORA), lambda b: (b, a, 0, 0)),
                  pl.BlockSpec((None, PAST_LEN, LANES), lambda b: (b, 0, 0)),
                  pl.BlockSpec(w_ukv.shape, lambda b: (0, 0))],
        out_specs=[pl.BlockSpec((None, PAST_LEN, QK_W), lambda b: (b, 0, 0)),
                   pl.BlockSpec((None, PAST_LEN, VM_W), lambda b: (b, 0, 0))],
        out_shape=[jax.ShapeDtypeStruct((nb, PAST_LEN, QK_W), BF16),
                   jax.ShapeDtypeStruct((nb, PAST_LEN, VM_W), BF16)],
        name="cache_kv",
    )(cache_ckv, kr_slot, w_ukv)


def _lat_attn_kernel(sink_ref, qcat_ref, kcat_ref, vm_ref, qg_ref, kg_ref, vg_ref, kctx_ref, vctx_ref, o_ref):
    n = pl.program_id(1)
    qcat = qcat_ref[...]

    def mla_scores(h):
        return _dot_nt(qcat[:, h * LANES:(h + 1) * LANES], kcat_ref[:, h * LANES:(h + 1) * LANES])

    per_tile = LANES // MLA_V
    held = []
    low_half = lax.broadcasted_iota(jnp.int32, (qcat.shape[0], LANES), 1) < MLA_V

    def mla_out(h, s):
        j = h // per_tile
        held.append(_softmax_pv(s, vm_ref[:, j * LANES:(j + 1) * LANES]))
        if len(held) == per_tile:
            o_ref[:, j * LANES:(j + 1) * LANES] = jnp.where(low_half, held[0], held[1]).astype(o_ref.dtype)
            held.clear()

    _staggered(MLA_HEADS, mla_scores, mla_out, depth=2)

    bb = BAND_BLOCK
    nsub = qg_ref.shape[0] // bb
    rows = GQA_GROUP * bb
    tq = lax.broadcasted_iota(jnp.int32, (rows, 3 * bb), 0) & (bb - 1)
    kk = lax.broadcasted_iota(jnp.int32, (rows, 3 * bb), 1)
    qg = qg_ref[...]
    kctx = kctx_ref[...]
    vctx = vctx_ref[...]

    def gqa_scores(i):
        j, g = divmod(i, GQA_KV_HEADS)
        hs = slice(g * GQA_HD, (g + 1) * GQA_HD)
        blk = n * nsub + j
        start = pl.multiple_of(blk * bb, bb)
        lo = jnp.maximum(tq, (1 - blk) * bb)
        hi = jnp.minimum(tq + 2 * WINDOW, (DEC_SEQ // bb + 1 - blk) * bb - 1)
        q = _stack_group(qg[j * bb:(j + 1) * bb, :], g)
        s_loc = jnp.where((kk >= lo) & (kk <= hi), _dot_nt(q, kg_ref[pl.ds(start, 3 * bb), hs]), NEG_BIG)
        return s_loc, _dot_nt(q, kctx[:, hs]), start

    def gqa_out(i, res):
        j, g = divmod(i, GQA_KV_HEADS)
        hs = slice(g * GQA_HD, (g + 1) * GQA_HD)
        s_loc, s_ctx, start = res
        sink = _gqa_sink_column(sink_ref, g, bb)
        m = jnp.maximum(_row_max(jnp.maximum(_lane_fold(s_loc, jnp.maximum), _lane_fold(s_ctx, jnp.maximum))), sink)
        p_loc = jnp.exp(s_loc - m)
        p_ctx = jnp.exp(s_ctx - m)
        l = _row_sum(_lane_fold(p_loc, jnp.add) + _lane_fold(p_ctx, jnp.add)) + jnp.exp(sink - m)
        o = (_dot(p_loc.astype(BF16), vg_ref[pl.ds(start, 3 * bb), hs]) + _dot(p_ctx.astype(BF16), vctx[:, hs])) / l
        for jj in range(GQA_GROUP):
            h = g * GQA_GROUP + jj
            o_ref[j * bb:(j + 1) * bb, O_M_W + h * GQA_HD:O_M_W + (h + 1) * GQA_HD] = (
                o[jj * bb:(jj + 1) * bb].astype(o_ref.dtype))

    _staggered(nsub * GQA_KV_HEADS, gqa_scores, gqa_out, depth=3)


def _lat_attn(sink, qcat, kcat_all, vm_all, qg, kg_pad, vg_pad, kg_ctx, vg_ctx, *, qb):
    nb = DEC_BATCH
    nq = DEC_SEQ // qb
    tk = kcat_all.shape[1]
    tp = kg_pad.shape[1]
    kvw = GQA_KV_HEADS * GQA_HD
    qrow = lambda b, n: (b * nq + n, 0)
    per_b = lambda b, n: (b, 0, 0)
    return pl.pallas_call(
        _lat_attn_kernel,
        grid=(nb, nq),
        in_specs=[pl.BlockSpec(memory_space=pltpu.SMEM),
                  pl.BlockSpec((qb, QK_W), qrow),
                  pl.BlockSpec((None, tk, QK_W), per_b),
                  pl.BlockSpec((None, tk, VM_W), per_b),
                  pl.BlockSpec((qb, O_G_W), qrow),
                  pl.BlockSpec((None, tp, kvw), per_b),
                  pl.BlockSpec((None, tp, kvw), per_b),
                  pl.BlockSpec((None, PAST_LEN, kvw), per_b),
                  pl.BlockSpec((None, PAST_LEN, kvw), per_b)],
        out_specs=pl.BlockSpec((qb, O_M_W + O_G_W), qrow),
        out_shape=jax.ShapeDtypeStruct((nb * DEC_SEQ, O_M_W + O_G_W), BF16),
        name="lat_attn",
    )(sink, qcat, kcat_all, vm_all, qg, kg_pad, vg_pad, kg_ctx, vg_ctx)


def _row_tiles(x):
    return [x[i * SUBLANES:(i + 1) * SUBLANES, :] for i in range(x.shape[0] // SUBLANES)]


def _scan_masks(mask_ref):
    c = SCAN_CHUNK
    t = lax.broadcasted_iota(jnp.int32, (c, c), 0)
    s = lax.broadcasted_iota(jnp.int32, (c, c), 1)
    for li, m in enumerate(SCAN_LEVELS):
        sh = m.bit_length() - 1
        mask_ref[li] = jnp.where((t >> sh) == (s >> sh), 1.0, 0.0)
    sh = SCAN_DIAG.bit_length() - 1
    same = (t >> sh) == (s >> sh)
    mask_ref[len(SCAN_LEVELS)] = jnp.where(same & (s <= t), 1.0, 0.0)
    mask_ref[len(SCAN_LEVELS) + 1] = jnp.where(same & (s >= t), 1.0, 0.0)


def _level_rows(m, rev):
    mt = m // SUBLANES
    nt = SCAN_CHUNK // SUBLANES
    groups = []
    for g in range(nt // (2 * mt)):
        first = list(range(g * 2 * mt, g * 2 * mt + mt))
        second = list(range(g * 2 * mt + mt, (g + 1) * 2 * mt))
        if rev:
            groups.append((first, second, second[0], 0))
        else:
            groups.append((second, first, first[-1], SUBLANES - 1))
    return groups


def _scan_chunks(chains, mask_ref):
    c = SCAN_CHUNK
    half = c // 2
    nt = c // SUBLANES
    n_lv = len(SCAN_LEVELS)
    work = []
    for q, zf, v, lb, st, tri, rev in chains:
        f = lb + (1.0 - lb) * _sigmoid(zf)
        lf = jnp.log2(jnp.maximum(f, F_MIN))
        hi = lf.astype(BF16)
        mid = (lf - hi.astype(F32)).astype(BF16)
        work.append(dict(q=q, k=1.0 - f, v=v, vb=v.astype(BF16), st=st, tri=tri, rev=rev, hi=hi, mid=mid))
    for w in work:
        w["b"] = _dot(w["tri"], w["hi"]) + _dot(w["tri"], w["mid"])
        w["qt"], w["kt"], w["bt"], w["vt"] = (_row_tiles(w[n]) for n in ("q", "k", "b", "v"))
    for w in work:
        mid_row = SCAN_DIAG // 2 if w["rev"] else SCAN_DIAG // 2 - 1
        e = jnp.concatenate([t - t[mid_row:mid_row + 1, :] for t in w["bt"]], axis=0)
        a = _dot_nt((w["q"] * jnp.exp2(e)).astype(BF16), (w["k"] * jnp.exp2(-e)).astype(BF16))
        w["a"] = jnp.where(mask_ref[n_lv + (1 if w["rev"] else 0)] > 0.5, a, 0.0).astype(BF16)
    for w in work:
        o = _dot(w["a"], w["vb"]) + _dot_nt((w["q"] * jnp.exp2(w["b"])).astype(BF16), w["st"].astype(BF16))
        w["o"] = _row_tiles(o)
    for li, m in enumerate(SCAN_LEVELS):
        for w in work:
            q_parts, k_parts, w["q_idx"], w["k_idx"] = [], [], [], []
            for q_rows, k_rows, rt, rr in _level_rows(m, w["rev"]):
                r = w["bt"][rt][rr:rr + 1, :]
                q_parts += [w["qt"][i] * jnp.exp2(w["bt"][i] - r) for i in q_rows]
                k_parts += [w["kt"][i] * jnp.exp2(r - w["bt"][i]) for i in k_rows]
                w["q_idx"] += q_rows
                w["k_idx"] += k_rows
            al = _dot_nt(jnp.concatenate(q_parts, axis=0).astype(BF16),
                         jnp.concatenate(k_parts, axis=0).astype(BF16))
            if m != half:
                al = al * mask_ref[li, :half, :half]
            w["al"] = al.astype(BF16)
        for w in work:
            vl = jnp.concatenate([w["vt"][i] for i in w["k_idx"]], axis=0).astype(BF16)
            ol = _row_tiles(_dot(w["al"], vl))
            for j, i in enumerate(w["q_idx"]):
                w["o"][i] = w["o"][i] + ol[j]
    outs = []
    for w in work:
        b_tot = w["bt"][0][0:1, :] if w["rev"] else w["bt"][nt - 1][SUBLANES - 1:SUBLANES, :]
        k_end = (w["k"] * jnp.exp2(b_tot - w["b"])).astype(BF16)
        st_new = w["st"] * jnp.exp2(b_tot) + _dot_tn(w["vb"], k_end)
        outs.append((jnp.concatenate(w["o"], axis=0), st_new))
    return outs


def _scan_kernel(*refs, r_layer, has_s0, emit_state, has_alias, n_heads):
    lbl_ref, zq_ref, zff_ref, zfb_ref, zi_ref, zg_ref, gain_ref = refs[:7]
    pos = 7
    s0_ref = None
    if has_s0:
        s0_ref = refs[pos]
        pos += 1
    if has_alias:
        pos += 1
    o_ref = refs[pos]
    pos += 1
    st_ref = None
    if emit_state:
        st_ref = refs[pos]
        pos += 1
    q_scr, o_scr, st_scr, mask_scr = refs[pos:pos + 4]

    c = SCAN_CHUNK
    w = HG_DK
    t_len = zq_ref.shape[0]
    n_chunks = t_len // c
    lg = lbl_ref[...]
    mx = lg[0]
    for j in range(1, N_REC_LAYERS):
        mx = jnp.maximum(mx, lg[j])
    ex = [jnp.exp(lg[j] - mx) for j in range(N_REC_LAYERS)]
    den = ex[0]
    for j in range(1, N_REC_LAYERS):
        den = den + ex[j]
    lower = jnp.zeros_like(mx)
    for j in range(1, r_layer + 1):
        lower = lower + ex[j] / den

    zq = zq_ref[...]
    q_scr[...] = zq * _sigmoid(zq)
    _scan_masks(mask_scr)
    row = lax.broadcasted_iota(jnp.int32, (c, c), 0)
    col = lax.broadcasted_iota(jnp.int32, (c, c), 1)
    tris = [jnp.where(col <= row, 1.0, 0.0).astype(BF16), jnp.where(col >= row, 1.0, 0.0).astype(BF16)]
    zf_refs = (zff_ref, zfb_ref)
    for d in range(2):
        for hh in range(n_heads):
            st_scr[d, hh] = s0_ref[d, hh].T if has_s0 else jnp.zeros((HG_DV, HG_DK), F32)

    def body(i, carry):
        chains = []
        for hh in range(n_heads):
            lanes = slice(hh * w, (hh + 1) * w)
            for d in range(2):
                ci = (n_chunks - 1 - i) if d == 1 else i
                rows = pl.ds(ci * c, c) if isinstance(ci, int) else pl.ds(pl.multiple_of(ci * c, c), c)
                chains.append((d, hh, rows, lanes, q_scr[rows, lanes], zf_refs[d][rows, lanes],
                               zi_ref[rows, lanes], st_scr[d, hh]))
        outs = _scan_chunks([(q, zf, v, lower[d:d + 1, lanes], st, tris[d], d == 1)
                             for d, hh, rows, lanes, q, zf, v, st in chains], mask_scr)
        for (d, hh, rows, lanes, *_), (o, st_new) in zip(chains, outs):
            st_scr[d, hh] = st_new
            o_scr[d, rows, lanes] = o
        return carry

    if n_chunks <= 2:
        for i in range(n_chunks):
            body(i, 0)
    else:
        lax.fori_loop(0, n_chunks, body, 0)
    if emit_state:
        for d in range(2):
            for hh in range(n_heads):
                if has_alias:
                    st_ref[d, hh] = st_scr[d, hh].T
                else:
                    for layer in range(N_REC_LAYERS):
                        st_ref[layer, d, hh] = (st_scr[d, hh].T if layer == r_layer
                                                else jnp.zeros((HG_DK, HG_DV), F32))

    zg = zg_ref[...]
    gate = zg * _sigmoid(zg)
    for hh in range(n_heads):
        lanes = slice(hh * w, (hh + 1) * w)
        y = _rms(o_scr[0, :, lanes] + o_scr[1, :, lanes], gain_ref[:, lanes]) * gate[:, lanes]
        o_ref[:, lanes] = y.astype(o_ref.dtype)


def _scan(z, lb_logits, out_gain, r_layer, *, t_len, n_heads, s0=None, emit_state=False, state_out=None):
    nb = z.shape[0] // t_len
    nh = HG_HEADS
    sh = n_heads
    ng = nh // sh
    wb = sh * HG_DK
    col = lambda k: (lambda b, h: (b, k * ng + h))
    in_specs = [pl.BlockSpec((N_REC_LAYERS, 2, wb), lambda b, h: (0, 0, h))]
    mode = dict(pipeline_mode=pl.Buffered(1)) if t_len * wb * 4 >= SCAN_SINGLE_BUFFER_BYTES else {}
    in_specs += [pl.BlockSpec((t_len, wb), col(k), **mode) for k in range(5)]
    in_specs += [pl.BlockSpec((1, wb), lambda b, h: (0, h))]
    args = [lb_logits, z, z, z, z, z, out_gain]
    if s0 is not None:
        in_specs.append(pl.BlockSpec((None, None, 2, sh, HG_DK, HG_DV),
                                     lambda b, h: (b, r_layer, 0, h, 0, 0)))
        args.append(s0)
    out_specs = [pl.BlockSpec((t_len, wb), lambda b, h: (b, h))]
    out_shape = [jax.ShapeDtypeStruct((nb * t_len, nh * HG_DV), BF16)]
    aliases = {}
    if emit_state:
        if state_out is None:
            out_specs.append(pl.BlockSpec((None, N_REC_LAYERS, 2, sh, HG_DK, HG_DV), lambda b, h: (b, 0, 0, h, 0, 0)))
        else:
            out_specs.append(pl.BlockSpec((None, None, 2, sh, HG_DK, HG_DV), lambda b, h: (b, r_layer, 0, h, 0, 0)))
            aliases[len(args)] = 1
            in_specs.append(pl.BlockSpec(memory_space=pl.ANY))
            args.append(state_out)
        out_shape.append(jax.ShapeDtypeStruct((nb, N_REC_LAYERS, 2, nh, HG_DK, HG_DV), F32))
    kern = functools.partial(_scan_kernel, r_layer=r_layer, has_s0=s0 is not None, emit_state=emit_state,
                             has_alias=bool(aliases), n_heads=n_heads)
    return pl.pallas_call(
        kern,
        grid=(nb, ng),
        in_specs=in_specs,
        out_specs=out_specs,
        out_shape=out_shape,
        input_output_aliases=aliases,
        scratch_shapes=[pltpu.VMEM((t_len, wb), F32),
                        pltpu.VMEM((2, t_len, wb), F32),
                        pltpu.VMEM((2, sh, HG_DV, HG_DK), F32),
                        pltpu.VMEM((len(SCAN_LEVELS) + 2, SCAN_CHUNK, SCAN_CHUNK), F32)],
        name="hgrn_scan",
    )(*args)


def _one_hot_rows(posm, e0, g, cap):
    t_len = posm.shape[1]
    slot = lax.broadcasted_iota(jnp.int32, (cap, t_len), 0)
    parts = [jnp.where(posm[e:e + 1, :] == slot, 1.0, 0.0).astype(BF16) for e in range(e0, e0 + g)]
    return parts[0] if g == 1 else jnp.concatenate(parts, axis=0)


PREFIX_PIECE = 256


def _prefix_count(mask_f, before):
    pieces = []
    carry = jnp.zeros((mask_f.shape[0], 1), F32)
    for p0 in range(0, mask_f.shape[1], PREFIX_PIECE):
        piece = mask_f[:, p0:p0 + PREFIX_PIECE]
        pieces.append(_dot(piece.astype(BF16), before) + carry)
        carry = carry + jnp.sum(piece, axis=1, keepdims=True)
    return pieces[0] if len(pieces) == 1 else jnp.concatenate(pieces, axis=1)


def _sort_desc_segments(x, seg):
    w = x.shape[1]
    lane = lax.broadcasted_iota(jnp.int32, x.shape, 1)
    k = 2
    while k <= seg:
        j = k // 2
        while j >= 1:
            is_first = (lane & j) == 0
            partner = jnp.where(is_first, pltpu.roll(x, w - j, axis=1), pltpu.roll(x, j, axis=1))
            keep_max = is_first if k == seg else (is_first == ((lane & k) == 0))
            x = jnp.where(keep_max, jnp.maximum(x, partner), jnp.minimum(x, partner))
            j //= 2
        k *= 2
    return x


def _select_kernel(aff_ref, h_ref, g_ref, gate_ref, pos_ref, *, t_len):
    a = aff_ref[...]
    ne, w = a.shape
    cap = EC_CAPACITY * t_len // N_EXPERTS
    capf = jnp.float32(cap)
    srt = jnp.concatenate([_sort_desc_segments(a[r0:r0 + SUBLANES, :], t_len)
                           for r0 in range(0, ne, SUBLANES)], axis=0)
    src = lax.broadcasted_iota(jnp.int32, (PREFIX_PIECE, PREFIX_PIECE), 0)
    dst = lax.broadcasted_iota(jnp.int32, (PREFIX_PIECE, PREFIX_PIECE), 1)
    before = jnp.where(src < dst, 1.0, 0.0).astype(BF16)
    slot = lax.broadcasted_iota(jnp.int32, (cap, t_len), 0)
    g = max(1, min(ne, 512 // cap))
    for s in range(w // t_len):
        t0 = s * t_len
        c0 = s * cap
        seg = a[:, t0:t0 + t_len]
        thr = srt[:, t0 + cap - 1:t0 + cap]
        gt = jnp.where(seg > thr, 1.0, 0.0)
        eq = jnp.where(seg == thr, 1.0, 0.0)
        need = capf - jnp.sum(gt, axis=1, keepdims=True)
        eq_rank = _prefix_count(eq, before)
        sel = gt + eq * jnp.where(eq_rank < need, 1.0, 0.0)
        pos = _prefix_count(sel, before)
        posm = jnp.where(sel > 0.5, pos.astype(jnp.int32), -1)
        pos_ref[:, t0:t0 + t_len] = posm
        h = h_ref[t0:t0 + t_len, :]
        for e0 in range(0, ne, g):
            rows = _dot(_one_hot_rows(posm, e0, g, cap), h)
            g_ref[e0:e0 + g, c0:c0 + cap, :] = rows.reshape(g, cap, h.shape[1]).astype(g_ref.dtype)
        for e in range(ne):
            gate_ref[e, c0:c0 + cap, :] = jnp.sum(jnp.where(posm[e:e + 1, :] == slot, seg[e:e + 1, :], 0.0),
                                                  axis=1, keepdims=True)


def _select(aff_t, h, *, t_len):
    ne, n = aff_t.shape
    d = h.shape[1]
    w = SELECT_BLOCK
    capw = EC_CAPACITY * w // N_EXPERTS
    kern = functools.partial(_select_kernel, t_len=t_len)
    return pl.pallas_call(
        kern,
        grid=(n // w,),
        in_specs=[pl.BlockSpec((ne, w), lambda s: (0, s)),
                  pl.BlockSpec((w, d), lambda s: (s, 0))],
        out_specs=[pl.BlockSpec((ne, capw, d), lambda s: (0, s, 0)),
                   pl.BlockSpec((ne, capw, 1), lambda s: (0, s, 0)),
                   pl.BlockSpec((ne, w), lambda s: (0, s))],
        out_shape=[jax.ShapeDtypeStruct((ne, n // w * capw, d), BF16),
                   jax.ShapeDtypeStruct((ne, n // w * capw, 1), F32),
                   jax.ShapeDtypeStruct((ne, n), jnp.int32)],
        name="ec_select",
    )(aff_t, h)


def _ffn_kernel(xc_ref, xl_ref, gc_ref, gl_ref, wg_ref, wu_ref, wd_ref, yc_ref, yl_ref, accc, accl):
    f = pl.program_id(1)
    chunks = [(x_ref, acc, r0) for x_ref, acc in ((xc_ref, accc), (xl_ref, accl))
              for r0 in range(0, x_ref.shape[0], FFN_ROWS)]

    @pl.when(f == 0)
    def _():
        accc[...] = jnp.zeros_like(accc)
        accl[...] = jnp.zeros_like(accl)

    w = {}

    def weight(name, ref):
        if name not in w:
            w[name] = ref[...].astype(BF16)
        return w[name]

    def up(x_ref, acc, r0):
        x = x_ref[r0:r0 + FFN_ROWS, :]
        hg = _dot(x, weight("g", wg_ref))
        return hg, _dot(x, weight("u", wu_ref))

    def down(x_ref, acc, r0, hg, hu):
        hid = (hg * _sigmoid(hg) * hu).astype(BF16)
        acc[r0:r0 + FFN_ROWS, :] = acc[r0:r0 + FFN_ROWS, :] + _dot(hid, weight("d", wd_ref))

    pending = None
    for ch in chunks:
        cur = up(*ch)
        if pending is not None:
            down(*pending)
        pending = ch + cur
    down(*pending)

    @pl.when(f == pl.num_programs(1) - 1)
    def _():
        yc_ref[...] = (accc[...] * gc_ref[...]).astype(yc_ref.dtype)
        yl_ref[...] = (accl[...] * gl_ref[...]).astype(yl_ref.dtype)


def _ffn(xc, xl, gc, gl, w_gate, w_up, w_down, layer, *, bf):
    ne, nc, d = xc.shape
    nl = xl.shape[1]
    ff = w_gate.shape[3]
    per_e = lambda e, f: (e, 0, 0)
    return pl.pallas_call(
        _ffn_kernel,
        grid=(ne, ff // bf),
        in_specs=[pl.BlockSpec((None, nc, d), per_e),
                  pl.BlockSpec((None, nl, d), per_e),
                  pl.BlockSpec((None, nc, 1), per_e),
                  pl.BlockSpec((None, nl, 1), per_e),
                  pl.BlockSpec((None, None, d, bf), lambda e, f: (layer, e, 0, f)),
                  pl.BlockSpec((None, None, d, bf), lambda e, f: (layer, e, 0, f)),
                  pl.BlockSpec((None, None, bf, d), lambda e, f: (layer, e, f, 0))],
        out_specs=[pl.BlockSpec((None, nc, d), per_e),
                   pl.BlockSpec((None, nl, d), per_e)],
        out_shape=[jax.ShapeDtypeStruct((ne, nc, d), BF16),
                   jax.ShapeDtypeStruct((ne, nl, d), BF16)],
        scratch_shapes=[pltpu.VMEM((nc, d), F32), pltpu.VMEM((nl, d), F32)],
        name="expert_ffn",
    )(xc, xl, gc, gl, w_gate, w_up, w_down)


def _combine_kernel(pos_ref, y_ref, x_ref, m_ref, fn_ref, o_ref, *, cap, sets):
    ne = pos_ref.shape[0]
    d = x_ref.shape[1]
    tt = x_ref.shape[0] // sets
    g = max(1, min(ne, 512 // cap))
    for s in range(sets):
        posm = pos_ref[:, s * tt:(s + 1) * tt]
        acc = jnp.zeros((tt, d), F32)
        for e0 in range(0, ne, g):
            acc = acc + _dot_tn(_one_hot_rows(posm, e0, g, cap),
                                y_ref[e0:e0 + g, s * cap:(s + 1) * cap, :].reshape(g * cap, d))
        x = x_ref[s * tt:(s + 1) * tt, :] + m_ref[5:6, :] * acc
        o_ref[s * tt:(s + 1) * tt, :] = x if fn_ref is None else _rms(x, fn_ref[...])


def _combine(posm, y, x, mods, *, t_len, tt, base, sets=1, final_gain=None):
    ne, n = posm.shape
    d = x.shape[1]
    ns = n // t_len
    cap = y.shape[1] // ns
    nt = t_len // tt
    assert sets == 1 or nt == 1
    in_specs = [pl.BlockSpec((ne, sets * tt), lambda s, j: (0, s * nt + j)),
                pl.BlockSpec((ne, sets * cap, d), lambda s, j: (0, s, 0)),
                pl.BlockSpec((sets * tt, d), lambda s, j: (s * nt + j, 0)),
                pl.BlockSpec((None, 6, d), lambda s, j: (base(s), 0, 0))]
    args = [posm, y, x, mods]
    if final_gain is None:
        kern = lambda p, yr, xr, m, o: _combine_kernel(p, yr, xr, m, None, o, cap=cap, sets=sets)
    else:
        kern = functools.partial(_combine_kernel, cap=cap, sets=sets)
        in_specs.append(pl.BlockSpec((1, d), lambda s, j: (0, 0)))
        args.append(final_gain)
    return pl.pallas_call(
        kern,
        grid=(ns // sets, nt),
        in_specs=in_specs,
        out_specs=pl.BlockSpec((sets * tt, d), lambda s, j: (s * nt + j, 0)),
        out_shape=jax.ShapeDtypeStruct((n, d), F32),
        name="ec_combine",
    )(*args)


def _attn_in_weight(w):
    o_kr = MLA_Q_LORA + MLA_KV_LORA
    o_qg = o_kr + MLA_ROPE
    cols = jnp.concatenate([w[:, :o_kr], w[:, o_qg:], jnp.zeros((w.shape[0], KR_OFF), w.dtype), w[:, o_kr:o_qg]],
                           axis=1)
    return jnp.pad(cols, ((0, 0), (0, ATTN_Z - cols.shape[1]))).astype(BF16)


def _heads_split(w, n_heads, first, second):
    k = w.shape[0]
    w3 = w.reshape(k, n_heads, first + second)
    return jnp.concatenate([w3[:, :, :first].reshape(k, n_heads * first),
                            w3[:, :, first:].reshape(k, n_heads * second)], axis=1).astype(BF16)


def _head_slots(w, n_heads, lo, hi):
    k = w.shape[0]
    w3 = w.reshape(k, n_heads, -1)[:, :, lo:hi]
    return jnp.pad(w3, ((0, 0), (0, 0), (0, LANES - (hi - lo)))).reshape(k, n_heads * LANES)


def _mla_weights(w_uq, w_ukv):
    k = w_ukv.shape[0]
    v_cols = w_ukv.reshape(k, MLA_HEADS, MLA_NOPE + MLA_V)[:, :, MLA_NOPE:].reshape(k, VM_W)
    return (_head_slots(w_uq, MLA_HEADS, 0, MLA_NOPE + MLA_ROPE).astype(BF16),
            jnp.concatenate([_head_slots(w_ukv, MLA_HEADS, 0, MLA_NOPE), v_cols], axis=1).astype(BF16))


def _slot_tables(cos, sin):
    n = cos.shape[0]
    pad = LANES - KR_OFF - MLA_ROPE
    cos = jnp.concatenate([jnp.ones((n, KR_OFF), F32), cos, jnp.ones((n, pad), F32)], axis=1)
    sin = jnp.concatenate([jnp.zeros((n, KR_OFF), F32), sin, jnp.zeros((n, pad), F32)], axis=1)
    return cos, sin


def _rope_tables(width, n_rep):
    half = width // 2
    t = jnp.arange(DEC_SEQ)
    rows = (t // GRID_W).astype(F32)
    cols = (t % GRID_W).astype(F32)
    inv_freq = ROPE_BASE ** (-jnp.arange(0, half, 2, dtype=F32) / half)
    ar = rows[:, None] * inv_freq
    ac = cols[:, None] * inv_freq
    cos = jnp.concatenate([jnp.cos(ar), jnp.cos(ar), jnp.cos(ac), jnp.cos(ac)], axis=1)
    sin = jnp.concatenate([-jnp.sin(ar), jnp.sin(ar), -jnp.sin(ac), jnp.sin(ac)], axis=1)
    return jnp.tile(cos, (1, n_rep)), jnp.tile(sin, (1, n_rep))


def kernel(x_prompt, x_sample, c, cache_mla_ckv, cache_mla_krope, cache_gqa_k, cache_gqa_v, state_hgrn,
           c_ctx, norm1, norm2, w_mod, b_mod, w_attn_in, mla_q_norm, w_mla_uq, mla_kv_norm, w_mla_ukv,
           gqa_sink, w_attn_out, w_rec_in, rec_lb_logits, rec_out_norm, w_rec_out, w_router, w_gate,
           w_up, w_down, final_norm):
    d = D_MODEL
    xc = x_prompt.reshape(BATCH * SEQ, d)
    xl = x_sample.reshape(DEC_BATCH * DEC_SEQ, d)
    cvecs = jnp.concatenate([c_ctx[None, :], c, jnp.zeros((N_MOD_ROWS - N_MOD_GROUPS, d), F32)], axis=0)
    mods_all = _mod_vectors(cvecs, w_mod, b_mod).reshape(DEPTH, N_MOD_ROWS, 6, d)[:, :N_MOD_GROUPS]
    grp_c = _group_map(None, 0, 0)
    cm, sm = _slot_tables(*_rope_tables(MLA_ROPE, 1))
    cg, sg = _rope_tables(GQA_HD, LANES // GQA_HD)
    kvw = GQA_KV_HEADS * GQA_HD

    caches, st_all = None, None
    for layer in range(DEPTH):
        mods = mods_all[layer]
        g1 = norm1[layer].reshape(1, d)
        g2 = norm2[layer].reshape(1, d)
        if layer % 2 == 0:
            a = layer // 2
            w_in = _attn_in_weight(w_attn_in[a])
            w_uq, w_ukv = _mla_weights(w_mla_uq[a], w_mla_ukv[a])
            w_out = w_attn_out[a].astype(BF16)
            qn_g = mla_q_norm[a].reshape(1, -1)
            kvn_g = mla_kv_norm[a].reshape(1, -1)
            sink = gqa_sink[a]
            zc = _modproj(xc, g1, mods, w_in, sub=0, bm=512, grp=grp_c)
            zl = _modproj(xl, g1, mods, w_in, sub=0, bm=512, grp=_group_map(DEC_SEQ, 512, 1))
            oc, *caches = _attn_ctx(zc, sink, qn_g, _heads_split(w_mla_uq[a], MLA_HEADS, MLA_NOPE, MLA_ROPE), kvn_g,
                                    _heads_split(w_mla_ukv[a], MLA_HEADS, MLA_NOPE, MLA_V), a, caches)
            qcat, kcat, vm, qg, kg, vg = _lat_prep(zl, qn_g, w_uq, kvn_g, w_ukv, cm, sm, cg, sg, bm=256)
            kr_c = jnp.pad(cache_mla_krope[:, a], ((0, 0), (0, 0), (KR_OFF, LANES - KR_OFF - MLA_ROPE)))
            kcat_c, vm_c = _cache_kv(cache_mla_ckv, kr_c, a, w_ukv)
            per_b = lambda x: x.reshape(DEC_BATCH, DEC_SEQ, -1)
            cat = lambda lat, ctx: jnp.concatenate([per_b(lat), ctx], axis=1)
            band = lambda x: jnp.pad(per_b(x), ((0, 0), (BAND_BLOCK, BAND_BLOCK), (0, 0)))
            ol = _lat_attn(sink, qcat, cat(kcat, kcat_c), cat(vm, vm_c), qg, band(kg), band(vg),
                           cache_gqa_k[:, a].reshape(DEC_BATCH, PAST_LEN, kvw).astype(BF16),
                           cache_gqa_v[:, a].reshape(DEC_BATCH, PAST_LEN, kvw).astype(BF16), qb=LAT_QB)
        else:
            r = layer // 2
            w_in = w_rec_in[r].astype(BF16)
            w_out = w_rec_out[r].astype(BF16)
            og = rec_out_norm[r].reshape(1, -1)
            zc = _modproj(xc, g1, mods, w_in, sub=0, bm=512, grp=grp_c)
            zl = _modproj(xl, g1, mods, w_in, sub=0, bm=512, grp=_group_map(DEC_SEQ, 512, 1))
            oc, st_all = _scan(zc, rec_lb_logits, og, r, t_len=SEQ, n_heads=8, emit_state=True, state_out=st_all)
            (ol,) = _scan(zl, rec_lb_logits, og, r, t_len=DEC_SEQ, n_heads=4, s0=state_hgrn)
        wr_t = w_router[layer].T.astype(BF16)
        xc, hc, aff_c = _outproj_router(oc, w_out, xc, mods, g2, wr_t, bm=1024, grp=grp_c)
        xl, hl, aff_l = _outproj_router(ol, w_out, xl, mods, g2, wr_t, bm=1024, grp=_group_map(DEC_SEQ, 1024, 1))
        gxc, gate_c, pos_c = _select(aff_c, hc, t_len=SEQ)
        gxl, gate_l, pos_l = _select(aff_l, hl, t_len=DEC_SEQ)
        yc, yl = _ffn(gxc, gxl, gate_c, gate_l, w_gate, w_up, w_down, layer, bf=512)
        fn = final_norm.reshape(1, d) if layer == DEPTH - 1 else None
        xc = _combine(pos_c, yc, xc, mods, t_len=SEQ, tt=SEQ, base=lambda s: 0, sets=COMBINE_SETS, final_gain=fn)
        xl = _combine(pos_l, yl, xl, mods, t_len=DEC_SEQ, tt=512, base=lambda s: 1 + s, final_gain=fn)

    y_prompt = xc.reshape(BATCH, SEQ, d)
    y_sample = xl.reshape(DEC_BATCH, DEC_SEQ, d)
    ckv_all, kr_all, kg_all, vg_all = caches
    heads = lambda x: x.reshape(BATCH, N_ATTN_LAYERS, SEQ, GQA_KV_HEADS, GQA_HD)
    return (y_prompt, y_sample, ckv_all, kr_all, heads(kg_all), heads(vg_all), st_all)
```

```python
import functools

import jax
import jax.numpy as jnp
from jax import lax
from jax.experimental import pallas as pl
from jax.experimental.pallas import tpu as pltpu

F32 = jnp.float32
BF16 = jnp.bfloat16

D_MODEL = 1024
BATCH = 32
SEQ = 256
DEPTH = 4
DEC_BATCH = 2
DEC_SEQ = 2048
PAST_LEN = 512
GRID_W = 64
ROPE_BASE = 10000.0
NORM_EPS = 1e-6
NEG_BIG = -1e30
F_MIN = 1e-6
N_ATTN_LAYERS = (DEPTH + 1) // 2
N_REC_LAYERS = DEPTH // 2
MLA_HEADS = 8
MLA_Q_LORA = 384
MLA_KV_LORA = 256
MLA_NOPE = 64
MLA_ROPE = 32
MLA_V = 64
MLA_SCALE = (MLA_NOPE + MLA_ROPE) ** -0.5
GQA_HEADS = 8
GQA_KV_HEADS = 2
GQA_GROUP = GQA_HEADS // GQA_KV_HEADS
GQA_HD = 64
GQA_SCALE = GQA_HD ** -0.5
WINDOW = 128
BAND_BLOCK = 128
HG_HEADS = 8
HG_DK = 128
HG_DV = 128
REC_IN = 3 * HG_HEADS * HG_DK + 2 * HG_HEADS * HG_DV
N_EXPERTS = 16
EC_CAPACITY = 2
D_FF = 2048

LANES = 128
N_MOD_ROWS = 8
N_MOD_GROUPS = 1 + DEC_BATCH

ZC_Q = 0
ZC_KV = ZC_Q + MLA_Q_LORA
ZC_QG = ZC_KV + MLA_KV_LORA
ZC_KG = ZC_QG + GQA_HEADS * GQA_HD
ZC_VG = ZC_KG + GQA_KV_HEADS * GQA_HD
ZC_KR = ZC_VG + GQA_KV_HEADS * GQA_HD
ATTN_Z = ZC_KR + LANES
KR_OFF = MLA_NOPE
QK_W = MLA_HEADS * LANES
VM_W = MLA_HEADS * MLA_V
O_M_W = MLA_HEADS * MLA_V
O_G_W = GQA_HEADS * GQA_HD

SUBLANES = 8
SCAN_CHUNK = 128
SCAN_LEVELS = (64, 32, 16, 8)
SCAN_DIAG = 8
SELECT_BLOCK = 2048
FFN_ROWS = 256
LAT_QB = 512
CTX_REQS = 4
COMBINE_SETS = 4
SCAN_SINGLE_BUFFER_BYTES = 4 << 20


def _dot(a, b):
    return jnp.dot(a, b, preferred_element_type=F32)


def _dot_nt(a, b):
    return lax.dot_general(a, b, (((1,), (1,)), ((), ())), preferred_element_type=F32)


def _dot_tn(a, b):
    return lax.dot_general(a, b, (((0,), (0,)), ((), ())), preferred_element_type=F32)


def _sigmoid(x):
    return 1.0 / (1.0 + jnp.exp(-x))


def _rms(x, gain):
    return x * lax.rsqrt(_row_sum(x * x) * (1.0 / x.shape[-1]) + NORM_EPS) * gain


def _mod_kernel(c_ref, w_ref, b_ref, o_ref):
    c = c_ref[...]
    s = (c * _sigmoid(c)).astype(BF16)
    o_ref[...] = _dot(s, w_ref[...].astype(BF16)) + b_ref[...]


def _mod_vectors(cvecs, w_mod, b_mod):
    bn = 1536
    nd = 6 * D_MODEL
    return pl.pallas_call(
        _mod_kernel,
        grid=(DEPTH, nd // bn),
        in_specs=[pl.BlockSpec((N_MOD_ROWS, D_MODEL), lambda l, j: (0, 0)),
                  pl.BlockSpec((None, D_MODEL, bn), lambda l, j: (l, 0, j)),
                  pl.BlockSpec((None, 1, bn), lambda l, j: (l, 0, j))],
        out_specs=pl.BlockSpec((None, N_MOD_ROWS, bn), lambda l, j: (l, 0, j)),
        out_shape=jax.ShapeDtypeStruct((DEPTH, N_MOD_ROWS, nd), F32),
        name="mod_vectors",
    )(cvecs, w_mod, b_mod.reshape(DEPTH, 1, nd))


def _group_map(rows_per_group, bm, base):
    if rows_per_group is None:
        return lambda i: base
    return lambda i: base + (i * bm) // rows_per_group


def _modproj_kernel(x_ref, g_ref, m_ref, w_ref, o_ref, *, shift_row, chunk):
    h = _rms(x_ref[...], g_ref[...])
    h = (h * (1.0 + m_ref[shift_row + 1:shift_row + 2, :]) + m_ref[shift_row:shift_row + 1, :]).astype(BF16)
    nout = o_ref.shape[1]
    for j in range(0, nout, chunk):
        o_ref[:, j:j + chunk] = _dot(h, w_ref[:, j:j + chunk])


def _modproj(x, gain, mods, w, *, sub, bm, grp):
    n, d = x.shape
    nout = w.shape[1]
    chunk = nout if nout <= 1536 else 1024
    kern = functools.partial(_modproj_kernel, shift_row=3 * sub, chunk=chunk)
    return pl.pallas_call(
        kern,
        grid=(n // bm,),
        in_specs=[pl.BlockSpec((bm, d), lambda i: (i, 0)),
                  pl.BlockSpec((1, d), lambda i: (0, 0)),
                  pl.BlockSpec((None, 6, d), lambda i: (grp(i), 0, 0)),
                  pl.BlockSpec((d, nout), lambda i: (0, 0), pipeline_mode=pl.Buffered(1))],
        out_specs=pl.BlockSpec((bm, nout), lambda i: (i, 0)),
        out_shape=jax.ShapeDtypeStruct((n, nout), F32),
        name="modproj",
    )(x, gain, mods, w)


def _outproj_router_kernel(a_ref, w_ref, x_ref, m_ref, g_ref, wr_ref, x_out, h_out, aff_out):
    x = x_ref[...] + m_ref[2:3, :] * _dot(a_ref[...], w_ref[...])
    x_out[...] = x
    h = (_rms(x, g_ref[...]) * (1.0 + m_ref[4:5, :]) + m_ref[3:4, :]).astype(BF16)
    h_out[...] = h
    logits = _dot_nt(wr_ref[...], h)
    e = jnp.exp(logits - jnp.max(logits, axis=0, keepdims=True))
    aff_out[...] = e / jnp.sum(e, axis=0, keepdims=True)


def _outproj_router(a, w, x, mods, gain2, wr_t, *, bm, grp):
    n, d = x.shape
    k = a.shape[1]
    row = lambda i: (i, 0)
    const = lambda i: (0, 0)
    return pl.pallas_call(
        _outproj_router_kernel,
        grid=(n // bm,),
        in_specs=[pl.BlockSpec((bm, k), row),
                  pl.BlockSpec((k, d), const),
                  pl.BlockSpec((bm, d), row),
                  pl.BlockSpec((None, 6, d), lambda i: (grp(i), 0, 0)),
                  pl.BlockSpec((1, d), const),
                  pl.BlockSpec((N_EXPERTS, d), const)],
        out_specs=[pl.BlockSpec((bm, d), row),
                   pl.BlockSpec((bm, d), row),
                   pl.BlockSpec((N_EXPERTS, bm), lambda i: (0, i))],
        out_shape=[jax.ShapeDtypeStruct((n, d), F32),
                   jax.ShapeDtypeStruct((n, d), BF16),
                   jax.ShapeDtypeStruct((N_EXPERTS, n), F32)],
        name="outproj_router",
    )(a, w, x, mods, gain2, wr_t)


def _lane_fold(x, op):
    w = x.shape[1]
    if w <= LANES or w % LANES:
        return x
    acc = x[:, :LANES]
    for j in range(1, w // LANES):
        acc = op(acc, x[:, j * LANES:(j + 1) * LANES])
    return acc


def _row_max(s):
    return jnp.max(_lane_fold(s, jnp.maximum), axis=-1, keepdims=True)


def _row_sum(p):
    return jnp.sum(_lane_fold(p, jnp.add), axis=-1, keepdims=True)


def _softmax_pv(s, v, sink=None):
    m = _row_max(s)
    if sink is not None:
        m = jnp.maximum(m, sink)
    p = jnp.exp(s - m)
    l = _row_sum(p)
    if sink is not None:
        l = l + jnp.exp(sink - m)
    return _dot(p.astype(BF16), v) / l


def _swap_pairs(x, q):
    w = x.shape[1]
    lane = lax.broadcasted_iota(jnp.int32, x.shape, 1)
    first = (lane % (2 * q)) < q
    return jnp.where(first, pltpu.roll(x, w - q, axis=1), pltpu.roll(x, q, axis=1))


def _rope(x, cos, sin, q):
    return x * cos + _swap_pairs(x, q) * sin


def _attn_project(z, qn_ref, wuq_ref, kvn_ref, wukv_ref):
    cq = _rms(z[:, ZC_Q:ZC_Q + MLA_Q_LORA], qn_ref[...]).astype(BF16)
    qcat = _dot(cq, wuq_ref[...])
    ckv_n = _rms(z[:, ZC_KV:ZC_KV + MLA_KV_LORA], kvn_ref[...])
    kv = _dot(ckv_n.astype(BF16), wukv_ref[...])
    return qcat, ckv_n, kv


def _tile_heads(kr_slot):
    return jnp.concatenate([kr_slot] * MLA_HEADS, axis=1)


def _staggered(n, first, second, depth):
    pending = []
    for i in range(n):
        pending.append((i, first(i)))
        if len(pending) > depth:
            second(*pending.pop(0))
    for item in pending:
        second(*item)


def _gqa_sink_column(sink_ref, g, rows):
    return jnp.concatenate([jnp.full((rows, 1), sink_ref[g * GQA_GROUP + j], F32) for j in range(GQA_GROUP)], axis=0)


def _stack_group(qg, g):
    return jnp.concatenate([qg[:, (g * GQA_GROUP + j) * GQA_HD:(g * GQA_GROUP + j + 1) * GQA_HD]
                            for j in range(GQA_GROUP)], axis=0)


def _attn_ctx_kernel(*refs, n_alias, a):
    sink_ref, z_ref, qn_ref, wuq_ref, kvn_ref, wukv_ref = refs[:6]
    o_ref, ckv_ref, kr_ref, kg_ref, vg_ref = refs[6 + n_alias:]
    t_len = SEQ
    z = z_ref[...]
    cq = _rms(z[:, ZC_Q:ZC_Q + MLA_Q_LORA], qn_ref[...]).astype(BF16)
    qm = _dot(cq, wuq_ref[...])
    ckv_n = _rms(z[:, ZC_KV:ZC_KV + MLA_KV_LORA], kvn_ref[...])
    kv = _dot(ckv_n.astype(BF16), wukv_ref[...])
    kr = z[:, ZC_KR + KR_OFF:ZC_KR + KR_OFF + MLA_ROPE]
    kg = z[:, ZC_KG:ZC_KG + GQA_KV_HEADS * GQA_HD]
    vg = z[:, ZC_VG:ZC_VG + GQA_KV_HEADS * GQA_HD]
    reqs = [slice(r * t_len, (r + 1) * t_len) for r in range(CTX_REQS)]
    for ref, val in ((ckv_ref, ckv_n), (kr_ref, kr), (kg_ref, kg), (vg_ref, vg)):
        for r, rows in enumerate(reqs):
            if n_alias:
                ref[r] = val[rows]
            else:
                for layer in range(N_ATTN_LAYERS):
                    ref[r, layer] = val[rows] if layer == a else jnp.zeros_like(val[rows])
    qn_w = MLA_HEADS * MLA_NOPE
    n_rows = z.shape[0]
    lane = lax.broadcasted_iota(jnp.int32, (n_rows, LANES), 1)
    low_half = lax.broadcasted_iota(jnp.int32, (t_len, LANES), 1) < MLA_V
    qn = qm[:, :qn_w] * MLA_SCALE
    qr = qm[:, qn_w:] * MLA_SCALE
    kn = kv[:, :qn_w].astype(BF16)
    vm = kv[:, qn_w:].astype(BF16)
    kr_slot = z[:, ZC_KR:ZC_KR + LANES]
    kr_rep = kr_slot
    for i in range(1, LANES // MLA_ROPE):
        kr_rep = kr_rep + pltpu.roll(kr_slot, i * MLA_ROPE, axis=1)
    kr_rep = kr_rep.astype(BF16)
    nope_per, rope_per = LANES // MLA_NOPE, LANES // MLA_ROPE
    held = {}

    def finish(h, r, rows, o, col0):
        tile = slice(col0 + h // 2 * LANES, col0 + (h // 2 + 1) * LANES)
        if h % 2 == 0:
            held[r] = o
        else:
            o_ref[rows, tile] = jnp.where(low_half, held.pop(r), o).astype(o_ref.dtype)

    for h in range(MLA_HEADS):
        nt = slice(h // nope_per * LANES, (h // nope_per + 1) * LANES)
        rt = slice(h // rope_per * LANES, (h // rope_per + 1) * LANES)
        q_n = jnp.where(lane // MLA_NOPE == h % nope_per, qn[:, nt], 0.0)
        q_r = jnp.where(lane // MLA_ROPE == h % rope_per, qr[:, rt], 0.0)
        q2 = jnp.concatenate([q_n, q_r], axis=1).astype(BF16)
        k2 = jnp.concatenate([kn[:, nt], kr_rep], axis=1)
        ss = [_dot_nt(q2[rows], k2[rows]) for rows in reqs]
        for r, (rows, s) in enumerate(zip(reqs, ss)):
            finish(h, r, rows, _softmax_pv(s, vm[rows, nt]), 0)
    qg = z[:, ZC_QG:ZC_QG + O_G_W] * GQA_SCALE
    kgs = [kg.astype(BF16), pltpu.roll(kg, GQA_HD, axis=1).astype(BF16)]
    vgs = [vg.astype(BF16), pltpu.roll(vg, GQA_HD, axis=1).astype(BF16)]
    for h in range(GQA_HEADS):
        g = h // GQA_GROUP
        par = h % 2
        tile = slice(h // 2 * LANES, (h // 2 + 1) * LANES)
        q_h = jnp.where(lane // GQA_HD == par, qg[:, tile], 0.0).astype(BF16)
        k_h = kgs[0 if par == g else 1]
        v_h = vgs[0 if par == g else 1]
        ss = [_dot_nt(q_h[rows], k_h[rows]) for rows in reqs]
        for r, (rows, s) in enumerate(zip(reqs, ss)):
            finish(h, r, rows, _softmax_pv(s, v_h[rows], sink_ref[h]), O_M_W)


def _attn_ctx(z, sink, q_norm, w_uq, kv_norm, w_ukv, a, caches=None):
    nb = z.shape[0] // SEQ
    rq = CTX_REQS
    kvw = GQA_KV_HEADS * GQA_HD
    const = lambda b: (0, 0)
    row = lambda b: (b, 0)
    cache_w = (MLA_KV_LORA, MLA_ROPE, kvw, kvw)
    in_specs = [pl.BlockSpec(memory_space=pltpu.SMEM),
                pl.BlockSpec((rq * SEQ, ATTN_Z), row),
                pl.BlockSpec((1, MLA_Q_LORA), const),
                pl.BlockSpec(w_uq.shape, const),
                pl.BlockSpec((1, MLA_KV_LORA), const),
                pl.BlockSpec(w_ukv.shape, const)]
    args = [sink, z, q_norm, w_uq, kv_norm, w_ukv]
    aliases = {}
    if caches is not None:
        for j, c in enumerate(caches):
            aliases[len(args)] = 1 + j
            in_specs.append(pl.BlockSpec(memory_space=pl.ANY))
            args.append(c)
    kern = functools.partial(_attn_ctx_kernel, n_alias=len(aliases), a=a)
    if aliases:
        cache_specs = [pl.BlockSpec((rq, None, SEQ, w), lambda b: (b, a, 0, 0)) for w in cache_w]
    else:
        cache_specs = [pl.BlockSpec((rq, N_ATTN_LAYERS, SEQ, w), lambda b: (b, 0, 0, 0)) for w in cache_w]
    return pl.pallas_call(
        kern,
        grid=(nb // rq,),
        in_specs=in_specs,
        out_specs=[pl.BlockSpec((rq * SEQ, O_M_W + O_G_W), row)] + cache_specs,
        out_shape=[jax.ShapeDtypeStruct((nb * SEQ, O_M_W + O_G_W), BF16)]
        + [jax.ShapeDtypeStruct((nb, N_ATTN_LAYERS, SEQ, w), F32) for w in cache_w],
        input_output_aliases=aliases,
        name="attn_ctx",
    )(*args)


def _lat_prep_kernel(z_ref, qn_ref, wuq_ref, kvn_ref, wukv_ref, cm_ref, sm_ref, cg_ref, sg_ref,
                     qcat_o, kcat_o, vm_o, qg_o, kg_o, vg_o):
    z = z_ref[...]
    qcat, ckv_n, kv = _attn_project(z, qn_ref, wuq_ref, kvn_ref, wukv_ref)
    cm = _tile_heads(cm_ref[...])
    sm = _tile_heads(sm_ref[...])
    cg = jnp.concatenate([cg_ref[...]] * (O_G_W // LANES), axis=1)
    sg = jnp.concatenate([sg_ref[...]] * (O_G_W // LANES), axis=1)
    qcat_o[...] = (_rope(qcat, cm, sm, MLA_ROPE // 4) * MLA_SCALE).astype(BF16)
    kr_slot = _rope(z[:, ZC_KR:ZC_KR + LANES], cm[:, :LANES], sm[:, :LANES], MLA_ROPE // 4)
    kcat_o[...] = (kv[:, :QK_W] + _tile_heads(kr_slot)).astype(BF16)
    vm_o[...] = kv[:, QK_W:].astype(BF16)
    qg_o[...] = (_rope(z[:, ZC_QG:ZC_QG + O_G_W], cg, sg, GQA_HD // 4) * GQA_SCALE).astype(BF16)
    kvw = GQA_KV_HEADS * GQA_HD
    kg_o[...] = _rope(z[:, ZC_KG:ZC_KG + kvw], cg[:, :kvw], sg[:, :kvw], GQA_HD // 4).astype(BF16)
    vg_o[...] = z[:, ZC_VG:ZC_VG + kvw].astype(BF16)


def _lat_prep(z, q_norm, w_uq, kv_norm, w_ukv, cm, sm, cg, sg, *, bm):
    n = z.shape[0]
    per = DEC_SEQ // bm
    kvw = GQA_KV_HEADS * GQA_HD
    const = lambda i: (0, 0)
    row = lambda i: (i, 0)
    pos = lambda i: (i % per, 0)
    widths = (QK_W, QK_W, VM_W, O_G_W, kvw, kvw)
    return pl.pallas_call(
        _lat_prep_kernel,
        grid=(n // bm,),
        in_specs=[pl.BlockSpec((bm, ATTN_Z), row),
                  pl.BlockSpec((1, MLA_Q_LORA), const),
                  pl.BlockSpec(w_uq.shape, const),
                  pl.BlockSpec((1, MLA_KV_LORA), const),
                  pl.BlockSpec(w_ukv.shape, const),
                  pl.BlockSpec((bm, LANES), pos),
                  pl.BlockSpec((bm, LANES), pos),
                  pl.BlockSpec((bm, LANES), pos),
                  pl.BlockSpec((bm, LANES), pos)],
        out_specs=[pl.BlockSpec((bm, w), row) for w in widths],
        out_shape=[jax.ShapeDtypeStruct((n, w), BF16) for w in widths],
        name="lat_prep",
    )(z, q_norm, w_uq, kv_norm, w_ukv, cm, sm, cg, sg)


def _cache_kv_kernel(c_ref, kr_ref, w_ref, kcat_o, vm_o):
    kv = _dot(c_ref[...].astype(BF16), w_ref[...])
    kcat_o[...] = (kv[:, :QK_W] + _tile_heads(kr_ref[...])).astype(BF16)
    vm_o[...] = kv[:, QK_W:].astype(BF16)


def _cache_kv(cache_ckv, kr_slot, a, w_ukv):
    nb = cache_ckv.shape[0]
    return pl.pallas_call(
        _cache_kv_kernel,
        grid=(nb,),
        in_specs=[pl.BlockSpec((None, None, PAST_LEN, MLA_KV_LORA), lambda b: (b, a, 0, 0)),
                  pl.BlockSpec((None, PAST_LEN, LANES), lambda b: (b, 0, 0)),
                  pl.BlockSpec(w_ukv.shape, lambda b: (0, 0))],
        out_specs=[pl.BlockSpec((None, PAST_LEN, QK_W), lambda b: (b, 0, 0)),
                   pl.BlockSpec((None, PAST_LEN, VM_W), lambda b: (b, 0, 0))],
        out_shape=[jax.ShapeDtypeStruct((nb, PAST_LEN, QK_W), BF16),
                   jax.ShapeDtypeStruct((nb, PAST_LEN, VM_W), BF16)],
        name="cache_kv",
    )(cache_ckv, kr_slot, w_ukv)


def _lat_attn_kernel(sink_ref, qcat_ref, kcat_ref, vm_ref, qg_ref, kg_ref, vg_ref, kctx_ref, vctx_ref, o_ref):
    n = pl.program_id(1)
    qcat = qcat_ref[...]

    def mla_scores(h):
        return _dot_nt(qcat[:, h * LANES:(h + 1) * LANES], kcat_ref[:, h * LANES:(h + 1) * LANES])

    per_tile = LANES // MLA_V
    held = []
    low_half = lax.broadcasted_iota(jnp.int32, (qcat.shape[0], LANES), 1) < MLA_V

    def mla_out(h, s):
        j = h // per_tile
        held.append(_softmax_pv(s, vm_ref[:, j * LANES:(j + 1) * LANES]))
        if len(held) == per_tile:
            o_ref[:, j * LANES:(j + 1) * LANES] = jnp.where(low_half, held[0], held[1]).astype(o_ref.dtype)
            held.clear()

    _staggered(MLA_HEADS, mla_scores, mla_out, depth=2)

    bb = BAND_BLOCK
    nsub = qg_ref.shape[0] // bb
    rows = GQA_GROUP * bb
    tq = lax.broadcasted_iota(jnp.int32, (rows, 3 * bb), 0) & (bb - 1)
    kk = lax.broadcasted_iota(jnp.int32, (rows, 3 * bb), 1)
    qg = qg_ref[...]
    kctx = kctx_ref[...]
    vctx = vctx_ref[...]

    def gqa_scores(i):
        j, g = divmod(i, GQA_KV_HEADS)
        hs = slice(g * GQA_HD, (g + 1) * GQA_HD)
        blk = n * nsub + j
        start = pl.multiple_of(blk * bb, bb)
        lo = jnp.maximum(tq, (1 - blk) * bb)
        hi = jnp.minimum(tq + 2 * WINDOW, (DEC_SEQ // bb + 1 - blk) * bb - 1)
        q = _stack_group(qg[j * bb:(j + 1) * bb, :], g)
        s_loc = jnp.where((kk >= lo) & (kk <= hi), _dot_nt(q, kg_ref[pl.ds(start, 3 * bb), hs]), NEG_BIG)
        return s_loc, _dot_nt(q, kctx[:, hs]), start

    def gqa_out(i, res):
        j, g = divmod(i, GQA_KV_HEADS)
        hs = slice(g * GQA_HD, (g + 1) * GQA_HD)
        s_loc, s_ctx, start = res
        sink = _gqa_sink_column(sink_ref, g, bb)
        m = jnp.maximum(_row_max(jnp.maximum(_lane_fold(s_loc, jnp.maximum), _lane_fold(s_ctx, jnp.maximum))), sink)
        p_loc = jnp.exp(s_loc - m)
        p_ctx = jnp.exp(s_ctx - m)
        l = _row_sum(_lane_fold(p_loc, jnp.add) + _lane_fold(p_ctx, jnp.add)) + jnp.exp(sink - m)
        o = (_dot(p_loc.astype(BF16), vg_ref[pl.ds(start, 3 * bb), hs]) + _dot(p_ctx.astype(BF16), vctx[:, hs])) / l
        for jj in range(GQA_GROUP):
            h = g * GQA_GROUP + jj
            o_ref[j * bb:(j + 1) * bb, O_M_W + h * GQA_HD:O_M_W + (h + 1) * GQA_HD] = (
                o[jj * bb:(jj + 1) * bb].astype(o_ref.dtype))

    _staggered(nsub * GQA_KV_HEADS, gqa_scores, gqa_out, depth=3)


def _lat_attn(sink, qcat, kcat_all, vm_all, qg, kg_pad, vg_pad, kg_ctx, vg_ctx, *, qb):
    nb = DEC_BATCH
    nq = DEC_SEQ // qb
    tk = kcat_all.shape[1]
    tp = kg_pad.shape[1]
    kvw = GQA_KV_HEADS * GQA_HD
    qrow = lambda b, n: (b * nq + n, 0)
    per_b = lambda b, n: (b, 0, 0)
    return pl.pallas_call(
        _lat_attn_kernel,
        grid=(nb, nq),
        in_specs=[pl.BlockSpec(memory_space=pltpu.SMEM),
                  pl.BlockSpec((qb, QK_W), qrow),
                  pl.BlockSpec((None, tk, QK_W), per_b),
                  pl.BlockSpec((None, tk, VM_W), per_b),
                  pl.BlockSpec((qb, O_G_W), qrow),
                  pl.BlockSpec((None, tp, kvw), per_b),
                  pl.BlockSpec((None, tp, kvw), per_b),
                  pl.BlockSpec((None, PAST_LEN, kvw), per_b),
                  pl.BlockSpec((None, PAST_LEN, kvw), per_b)],
        out_specs=pl.BlockSpec((qb, O_M_W + O_G_W), qrow),
        out_shape=jax.ShapeDtypeStruct((nb * DEC_SEQ, O_M_W + O_G_W), BF16),
        name="lat_attn",
    )(sink, qcat, kcat_all, vm_all, qg, kg_pad, vg_pad, kg_ctx, vg_ctx)


def _row_tiles(x):
    return [x[i * SUBLANES:(i + 1) * SUBLANES, :] for i in range(x.shape[0] // SUBLANES)]


def _scan_masks(mask_ref):
    c = SCAN_CHUNK
    t = lax.broadcasted_iota(jnp.int32, (c, c), 0)
    s = lax.broadcasted_iota(jnp.int32, (c, c), 1)
    for li, m in enumerate(SCAN_LEVELS):
        sh = m.bit_length() - 1
        mask_ref[li] = jnp.where((t >> sh) == (s >> sh), 1.0, 0.0)
    sh = SCAN_DIAG.bit_length() - 1
    same = (t >> sh) == (s >> sh)
    mask_ref[len(SCAN_LEVELS)] = jnp.where(same & (s <= t), 1.0, 0.0)
    mask_ref[len(SCAN_LEVELS) + 1] = jnp.where(same & (s >= t), 1.0, 0.0)


def _level_rows(m, rev):
    mt = m // SUBLANES
    nt = SCAN_CHUNK // SUBLANES
    groups = []
    for g in range(nt // (2 * mt)):
        first = list(range(g * 2 * mt, g * 2 * mt + mt))
        second = list(range(g * 2 * mt + mt, (g + 1) * 2 * mt))
        if rev:
            groups.append((first, second, second[0], 0))
        else:
            groups.append((second, first, first[-1], SUBLANES - 1))
    return groups


def _scan_chunks(chains, mask_ref):
    c = SCAN_CHUNK
    half = c // 2
    nt = c // SUBLANES
    n_lv = len(SCAN_LEVELS)
    work = []
    for q, zf, v, lb, st, tri, rev in chains:
        f = lb + (1.0 - lb) * _sigmoid(zf)
        lf = jnp.log2(jnp.maximum(f, F_MIN))
        hi = lf.astype(BF16)
        mid = (lf - hi.astype(F32)).astype(BF16)
        work.append(dict(q=q, k=1.0 - f, v=v, vb=v.astype(BF16), st=st, tri=tri, rev=rev, hi=hi, mid=mid))
    for w in work:
        w["b"] = _dot(w["tri"], w["hi"]) + _dot(w["tri"], w["mid"])
        w["qt"], w["kt"], w["bt"], w["vt"] = (_row_tiles(w[n]) for n in ("q", "k", "b", "v"))
    for w in work:
        mid_row = SCAN_DIAG // 2 if w["rev"] else SCAN_DIAG // 2 - 1
        e = jnp.concatenate([t - t[mid_row:mid_row + 1, :] for t in w["bt"]], axis=0)
        a = _dot_nt((w["q"] * jnp.exp2(e)).astype(BF16), (w["k"] * jnp.exp2(-e)).astype(BF16))
        w["a"] = jnp.where(mask_ref[n_lv + (1 if w["rev"] else 0)] > 0.5, a, 0.0).astype(BF16)
    for w in work:
        o = _dot(w["a"], w["vb"]) + _dot_nt((w["q"] * jnp.exp2(w["b"])).astype(BF16), w["st"].astype(BF16))
        w["o"] = _row_tiles(o)
    for li, m in enumerate(SCAN_LEVELS):
        for w in work:
            q_parts, k_parts, w["q_idx"], w["k_idx"] = [], [], [], []
            for q_rows, k_rows, rt, rr in _level_rows(m, w["rev"]):
                r = w["bt"][rt][rr:rr + 1, :]
                q_parts += [w["qt"][i] * jnp.exp2(w["bt"][i] - r) for i in q_rows]
                k_parts += [w["kt"][i] * jnp.exp2(r - w["bt"][i]) for i in k_rows]
                w["q_idx"] += q_rows
                w["k_idx"] += k_rows
            al = _dot_nt(jnp.concatenate(q_parts, axis=0).astype(BF16),
                         jnp.concatenate(k_parts, axis=0).astype(BF16))
            if m != half:
                al = al * mask_ref[li, :half, :half]
            w["al"] = al.astype(BF16)
        for w in work:
            vl = jnp.concatenate([w["vt"][i] for i in w["k_idx"]], axis=0).astype(BF16)
            ol = _row_tiles(_dot(w["al"], vl))
            for j, i in enumerate(w["q_idx"]):
                w["o"][i] = w["o"][i] + ol[j]
    outs = []
    for w in work:
        b_tot = w["bt"][0][0:1, :] if w["rev"] else w["bt"][nt - 1][SUBLANES - 1:SUBLANES, :]
        k_end = (w["k"] * jnp.exp2(b_tot - w["b"])).astype(BF16)
        st_new = w["st"] * jnp.exp2(b_tot) + _dot_tn(w["vb"], k_end)
        outs.append((jnp.concatenate(w["o"], axis=0), st_new))
    return outs


def _scan_kernel(*refs, r_layer, has_s0, emit_state, has_alias, n_heads):
    lbl_ref, zq_ref, zff_ref, zfb_ref, zi_ref, zg_ref, gain_ref = refs[:7]
    pos = 7
    s0_ref = None
    if has_s0:
        s0_ref = refs[pos]
        pos += 1
    if has_alias:
        pos += 1
    o_ref = refs[pos]
    pos += 1
    st_ref = None
    if emit_state:
        st_ref = refs[pos]
        pos += 1
    q_scr, o_scr, st_scr, mask_scr = refs[pos:pos + 4]

    c = SCAN_CHUNK
    w = HG_DK
    t_len = zq_ref.shape[0]
    n_chunks = t_len // c
    lg = lbl_ref[...]
    mx = lg[0]
    for j in range(1, N_REC_LAYERS):
        mx = jnp.maximum(mx, lg[j])
    ex = [jnp.exp(lg[j] - mx) for j in range(N_REC_LAYERS)]
    den = ex[0]
    for j in range(1, N_REC_LAYERS):
        den = den + ex[j]
    lower = jnp.zeros_like(mx)
    for j in range(1, r_layer + 1):
        lower = lower + ex[j] / den

    zq = zq_ref[...]
    q_scr[...] = zq * _sigmoid(zq)
    _scan_masks(mask_scr)
    row = lax.broadcasted_iota(jnp.int32, (c, c), 0)
    col = lax.broadcasted_iota(jnp.int32, (c, c), 1)
    tris = [jnp.where(col <= row, 1.0, 0.0).astype(BF16), jnp.where(col >= row, 1.0, 0.0).astype(BF16)]
    zf_refs = (zff_ref, zfb_ref)
    for d in range(2):
        for hh in range(n_heads):
            st_scr[d, hh] = s0_ref[d, hh].T if has_s0 else jnp.zeros((HG_DV, HG_DK), F32)

    def body(i, carry):
        chains = []
        for hh in range(n_heads):
            lanes = slice(hh * w, (hh + 1) * w)
            for d in range(2):
                ci = (n_chunks - 1 - i) if d == 1 else i
                rows = pl.ds(ci * c, c) if isinstance(ci, int) else pl.ds(pl.multiple_of(ci * c, c), c)
                chains.append((d, hh, rows, lanes, q_scr[rows, lanes], zf_refs[d][rows, lanes],
                               zi_ref[rows, lanes], st_scr[d, hh]))
        outs = _scan_chunks([(q, zf, v, lower[d:d + 1, lanes], st, tris[d], d == 1)
                             for d, hh, rows, lanes, q, zf, v, st in chains], mask_scr)
        for (d, hh, rows, lanes, *_), (o, st_new) in zip(chains, outs):
            st_scr[d, hh] = st_new
            o_scr[d, rows, lanes] = o
        return carry

    if n_chunks <= 2:
        for i in range(n_chunks):
            body(i, 0)
    else:
        lax.fori_loop(0, n_chunks, body, 0)
    if emit_state:
        for d in range(2):
            for hh in range(n_heads):
                if has_alias:
                    st_ref[d, hh] = st_scr[d, hh].T
                else:
                    for layer in range(N_REC_LAYERS):
                        st_ref[layer, d, hh] = (st_scr[d, hh].T if layer == r_layer
                                                else jnp.zeros((HG_DK, HG_DV), F32))

    zg = zg_ref[...]
    gate = zg * _sigmoid(zg)
    for hh in range(n_heads):
        lanes = slice(hh * w, (hh + 1) * w)
        y = _rms(o_scr[0, :, lanes] + o_scr[1, :, lanes], gain_ref[:, lanes]) * gate[:, lanes]
        o_ref[:, lanes] = y.astype(o_ref.dtype)


def _scan(z, lb_logits, out_gain, r_layer, *, t_len, n_heads, s0=None, emit_state=False, state_out=None):
    nb = z.shape[0] // t_len
    nh = HG_HEADS
    sh = n_heads
    ng = nh // sh
    wb = sh * HG_DK
    col = lambda k: (lambda b, h: (b, k * ng + h))
    in_specs = [pl.BlockSpec((N_REC_LAYERS, 2, wb), lambda b, h: (0, 0, h))]
    big = t_len * wb * 4 >= SCAN_SINGLE_BUFFER_BYTES
    single = dict(pipeline_mode=pl.Buffered(1))
    in_specs += [pl.BlockSpec((t_len, wb), col(k), **(single if big and k in (0, 4) else {})) for k in range(5)]
    in_specs += [pl.BlockSpec((1, wb), lambda b, h: (0, h))]
    args = [lb_logits, z, z, z, z, z, out_gain]
    if s0 is not None:
        in_specs.append(pl.BlockSpec((None, None, 2, sh, HG_DK, HG_DV),
                                     lambda b, h: (b, r_layer, 0, h, 0, 0)))
        args.append(s0)
    out_specs = [pl.BlockSpec((t_len, wb), lambda b, h: (b, h))]
    out_shape = [jax.ShapeDtypeStruct((nb * t_len, nh * HG_DV), BF16)]
    aliases = {}
    if emit_state:
        if state_out is None:
            out_specs.append(pl.BlockSpec((None, N_REC_LAYERS, 2, sh, HG_DK, HG_DV), lambda b, h: (b, 0, 0, h, 0, 0)))
        else:
            out_specs.append(pl.BlockSpec((None, None, 2, sh, HG_DK, HG_DV), lambda b, h: (b, r_layer, 0, h, 0, 0)))
            aliases[len(args)] = 1
            in_specs.append(pl.BlockSpec(memory_space=pl.ANY))
            args.append(state_out)
        out_shape.append(jax.ShapeDtypeStruct((nb, N_REC_LAYERS, 2, nh, HG_DK, HG_DV), F32))
    kern = functools.partial(_scan_kernel, r_layer=r_layer, has_s0=s0 is not None, emit_state=emit_state,
                             has_alias=bool(aliases), n_heads=n_heads)
    return pl.pallas_call(
        kern,
        grid=(nb, ng),
        in_specs=in_specs,
        out_specs=out_specs,
        out_shape=out_shape,
        input_output_aliases=aliases,
        scratch_shapes=[pltpu.VMEM((t_len, wb), F32),
                        pltpu.VMEM((2, t_len, wb), F32),
                        pltpu.VMEM((2, sh, HG_DV, HG_DK), F32),
                        pltpu.VMEM((len(SCAN_LEVELS) + 2, SCAN_CHUNK, SCAN_CHUNK), F32)],
        name="hgrn_scan",
    )(*args)


def _one_hot_rows(posm, e0, g, cap):
    t_len = posm.shape[1]
    slot = lax.broadcasted_iota(jnp.int32, (cap, t_len), 0)
    parts = [jnp.where(posm[e:e + 1, :] == slot, 1.0, 0.0).astype(BF16) for e in range(e0, e0 + g)]
    return parts[0] if g == 1 else jnp.concatenate(parts, axis=0)


PREFIX_PIECE = 256


def _prefix_count(mask_f, before):
    pieces = []
    carry = jnp.zeros((mask_f.shape[0], 1), F32)
    for p0 in range(0, mask_f.shape[1], PREFIX_PIECE):
        piece = mask_f[:, p0:p0 + PREFIX_PIECE]
        pieces.append(_dot(piece.astype(BF16), before) + carry)
        carry = carry + jnp.sum(piece, axis=1, keepdims=True)
    return pieces[0] if len(pieces) == 1 else jnp.concatenate(pieces, axis=1)


def _sort_desc_segments(x, seg):
    w = x.shape[1]
    lane = lax.broadcasted_iota(jnp.int32, x.shape, 1)
    k = 2
    while k <= seg:
        j = k // 2
        while j >= 1:
            is_first = (lane & j) == 0
            partner = jnp.where(is_first, pltpu.roll(x, w - j, axis=1), pltpu.roll(x, j, axis=1))
            keep_max = is_first if k == seg else (is_first == ((lane & k) == 0))
            x = jnp.where(keep_max, jnp.maximum(x, partner), jnp.minimum(x, partner))
            j //= 2
        k *= 2
    return x


def _select_kernel(aff_ref, h_ref, g_ref, gate_ref, pos_ref, *, t_len):
    a = aff_ref[...]
    ne, w = a.shape
    cap = EC_CAPACITY * t_len // N_EXPERTS
    capf = jnp.float32(cap)
    srt = jnp.concatenate([_sort_desc_segments(a[r0:r0 + SUBLANES, :], t_len)
                           for r0 in range(0, ne, SUBLANES)], axis=0)
    src = lax.broadcasted_iota(jnp.int32, (PREFIX_PIECE, PREFIX_PIECE), 0)
    dst = lax.broadcasted_iota(jnp.int32, (PREFIX_PIECE, PREFIX_PIECE), 1)
    before = jnp.where(src < dst, 1.0, 0.0).astype(BF16)
    slot = lax.broadcasted_iota(jnp.int32, (cap, t_len), 0)
    g = max(1, min(ne, 512 // cap))
    for s in range(w // t_len):
        t0 = s * t_len
        c0 = s * cap
        seg = a[:, t0:t0 + t_len]
        thr = srt[:, t0 + cap - 1:t0 + cap]
        gt = jnp.where(seg > thr, 1.0, 0.0)
        eq = jnp.where(seg == thr, 1.0, 0.0)
        need = capf - jnp.sum(gt, axis=1, keepdims=True)
        eq_rank = _prefix_count(eq, before)
        sel = gt + eq * jnp.where(eq_rank < need, 1.0, 0.0)
        pos = _prefix_count(sel, before)
        posm = jnp.where(sel > 0.5, pos.astype(jnp.int32), -1)
        pos_ref[:, t0:t0 + t_len] = posm
        h = h_ref[t0:t0 + t_len, :]
        for e0 in range(0, ne, g):
            rows = _dot(_one_hot_rows(posm, e0, g, cap), h)
            g_ref[e0:e0 + g, c0:c0 + cap, :] = rows.reshape(g, cap, h.shape[1]).astype(g_ref.dtype)
        for e in range(ne):
            gate_ref[e, c0:c0 + cap, :] = jnp.sum(jnp.where(posm[e:e + 1, :] == slot, seg[e:e + 1, :], 0.0),
                                                  axis=1, keepdims=True)


def _select(aff_t, h, *, t_len):
    ne, n = aff_t.shape
    d = h.shape[1]
    w = SELECT_BLOCK
    capw = EC_CAPACITY * w // N_EXPERTS
    kern = functools.partial(_select_kernel, t_len=t_len)
    return pl.pallas_call(
        kern,
        grid=(n // w,),
        in_specs=[pl.BlockSpec((ne, w), lambda s: (0, s)),
                  pl.BlockSpec((w, d), lambda s: (s, 0))],
        out_specs=[pl.BlockSpec((ne, capw, d), lambda s: (0, s, 0)),
                   pl.BlockSpec((ne, capw, 1), lambda s: (0, s, 0)),
                   pl.BlockSpec((ne, w), lambda s: (0, s))],
        out_shape=[jax.ShapeDtypeStruct((ne, n // w * capw, d), BF16),
                   jax.ShapeDtypeStruct((ne, n // w * capw, 1), F32),
                   jax.ShapeDtypeStruct((ne, n), jnp.int32)],
        name="ec_select",
    )(aff_t, h)


def _ffn_kernel(xc_ref, xl_ref, gc_ref, gl_ref, wg_ref, wu_ref, wd_ref, yc_ref, yl_ref, accc, accl):
    f = pl.program_id(1)
    chunks = [(x_ref, acc, r0) for x_ref, acc in ((xc_ref, accc), (xl_ref, accl))
              for r0 in range(0, x_ref.shape[0], FFN_ROWS)]

    @pl.when(f == 0)
    def _():
        accc[...] = jnp.zeros_like(accc)
        accl[...] = jnp.zeros_like(accl)

    w = {}

    def weight(name, ref):
        if name not in w:
            w[name] = ref[...].astype(BF16)
        return w[name]

    def up(x_ref, acc, r0):
        x = x_ref[r0:r0 + FFN_ROWS, :]
        hg = _dot(x, weight("g", wg_ref))
        return hg, _dot(x, weight("u", wu_ref))

    def down(x_ref, acc, r0, hg, hu):
        hid = (hg * _sigmoid(hg) * hu).astype(BF16)
        acc[r0:r0 + FFN_ROWS, :] = acc[r0:r0 + FFN_ROWS, :] + _dot(hid, weight("d", wd_ref))

    pending = None
    for ch in chunks:
        cur = up(*ch)
        if pending is not None:
            down(*pending)
        pending = ch + cur
    down(*pending)

    @pl.when(f == pl.num_programs(1) - 1)
    def _():
        yc_ref[...] = (accc[...] * gc_ref[...]).astype(yc_ref.dtype)
        yl_ref[...] = (accl[...] * gl_ref[...]).astype(yl_ref.dtype)


def _ffn(xc, xl, gc, gl, w_gate, w_up, w_down, layer, *, bf):
    ne, nc, d = xc.shape
    nl = xl.shape[1]
    ff = w_gate.shape[3]
    per_e = lambda e, f: (e, 0, 0)
    return pl.pallas_call(
        _ffn_kernel,
        grid=(ne, ff // bf),
        in_specs=[pl.BlockSpec((None, nc, d), per_e),
                  pl.BlockSpec((None, nl, d), per_e),
                  pl.BlockSpec((None, nc, 1), per_e),
                  pl.BlockSpec((None, nl, 1), per_e),
                  pl.BlockSpec((None, None, d, bf), lambda e, f: (layer, e, 0, f)),
                  pl.BlockSpec((None, None, d, bf), lambda e, f: (layer, e, 0, f)),
                  pl.BlockSpec((None, None, bf, d), lambda e, f: (layer, e, f, 0))],
        out_specs=[pl.BlockSpec((None, nc, d), per_e),
                   pl.BlockSpec((None, nl, d), per_e)],
        out_shape=[jax.ShapeDtypeStruct((ne, nc, d), BF16),
                   jax.ShapeDtypeStruct((ne, nl, d), BF16)],
        scratch_shapes=[pltpu.VMEM((nc, d), F32), pltpu.VMEM((nl, d), F32)],
        name="expert_ffn",
    )(xc, xl, gc, gl, w_gate, w_up, w_down)


def _combine_kernel(pos_ref, y_ref, x_ref, m_ref, fn_ref, o_ref, *, cap, sets):
    ne = pos_ref.shape[0]
    d = x_ref.shape[1]
    tt = x_ref.shape[0] // sets
    g = max(1, min(ne, 512 // cap))
    for s in range(sets):
        posm = pos_ref[:, s * tt:(s + 1) * tt]
        acc = jnp.zeros((tt, d), F32)
        for e0 in range(0, ne, g):
            acc = acc + _dot_tn(_one_hot_rows(posm, e0, g, cap),
                                y_ref[e0:e0 + g, s * cap:(s + 1) * cap, :].reshape(g * cap, d))
        x = x_ref[s * tt:(s + 1) * tt, :] + m_ref[5:6, :] * acc
        o_ref[s * tt:(s + 1) * tt, :] = x if fn_ref is None else _rms(x, fn_ref[...])


def _combine(posm, y, x, mods, *, t_len, tt, base, sets=1, final_gain=None):
    ne, n = posm.shape
    d = x.shape[1]
    ns = n // t_len
    cap = y.shape[1] // ns
    nt = t_len // tt
    assert sets == 1 or nt == 1
    in_specs = [pl.BlockSpec((ne, sets * tt), lambda s, j: (0, s * nt + j)),
                pl.BlockSpec((ne, sets * cap, d), lambda s, j: (0, s, 0)),
                pl.BlockSpec((sets * tt, d), lambda s, j: (s * nt + j, 0)),
                pl.BlockSpec((None, 6, d), lambda s, j: (base(s), 0, 0))]
    args = [posm, y, x, mods]
    if final_gain is None:
        kern = lambda p, yr, xr, m, o: _combine_kernel(p, yr, xr, m, None, o, cap=cap, sets=sets)
    else:
        kern = functools.partial(_combine_kernel, cap=cap, sets=sets)
        in_specs.append(pl.BlockSpec((1, d), lambda s, j: (0, 0)))
        args.append(final_gain)
    return pl.pallas_call(
        kern,
        grid=(ns // sets, nt),
        in_specs=in_specs,
        out_specs=pl.BlockSpec((sets * tt, d), lambda s, j: (s * nt + j, 0)),
        out_shape=jax.ShapeDtypeStruct((n, d), F32),
        name="ec_combine",
    )(*args)


def _attn_in_weight(w):
    o_kr = MLA_Q_LORA + MLA_KV_LORA
    o_qg = o_kr + MLA_ROPE
    cols = jnp.concatenate([w[:, :o_kr], w[:, o_qg:], jnp.zeros((w.shape[0], KR_OFF), w.dtype), w[:, o_kr:o_qg]],
                           axis=1)
    return jnp.pad(cols, ((0, 0), (0, ATTN_Z - cols.shape[1]))).astype(BF16)


def _heads_split(w, n_heads, first, second):
    k = w.shape[0]
    w3 = w.reshape(k, n_heads, first + second)
    return jnp.concatenate([w3[:, :, :first].reshape(k, n_heads * first),
                            w3[:, :, first:].reshape(k, n_heads * second)], axis=1).astype(BF16)


def _head_slots(w, n_heads, lo, hi):
    k = w.shape[0]
    w3 = w.reshape(k, n_heads, -1)[:, :, lo:hi]
    return jnp.pad(w3, ((0, 0), (0, 0), (0, LANES - (hi - lo)))).reshape(k, n_heads * LANES)


def _mla_weights(w_uq, w_ukv):
    k = w_ukv.shape[0]
    v_cols = w_ukv.reshape(k, MLA_HEADS, MLA_NOPE + MLA_V)[:, :, MLA_NOPE:].reshape(k, VM_W)
    return (_head_slots(w_uq, MLA_HEADS, 0, MLA_NOPE + MLA_ROPE).astype(BF16),
            jnp.concatenate([_head_slots(w_ukv, MLA_HEADS, 0, MLA_NOPE), v_cols], axis=1).astype(BF16))


def _slot_tables(cos, sin):
    n = cos.shape[0]
    pad = LANES - KR_OFF - MLA_ROPE
    cos = jnp.concatenate([jnp.ones((n, KR_OFF), F32), cos, jnp.ones((n, pad), F32)], axis=1)
    sin = jnp.concatenate([jnp.zeros((n, KR_OFF), F32), sin, jnp.zeros((n, pad), F32)], axis=1)
    return cos, sin


def _rope_tables(width, n_rep):
    half = width // 2
    t = jnp.arange(DEC_SEQ)
    rows = (t // GRID_W).astype(F32)
    cols = (t % GRID_W).astype(F32)
    inv_freq = ROPE_BASE ** (-jnp.arange(0, half, 2, dtype=F32) / half)
    ar = rows[:, None] * inv_freq
    ac = cols[:, None] * inv_freq
    cos = jnp.concatenate([jnp.cos(ar), jnp.cos(ar), jnp.cos(ac), jnp.cos(ac)], axis=1)
    sin = jnp.concatenate([-jnp.sin(ar), jnp.sin(ar), -jnp.sin(ac), jnp.sin(ac)], axis=1)
    return jnp.tile(cos, (1, n_rep)), jnp.tile(sin, (1, n_rep))


def kernel(x_prompt, x_sample, c, cache_mla_ckv, cache_mla_krope, cache_gqa_k, cache_gqa_v, state_hgrn,
           c_ctx, norm1, norm2, w_mod, b_mod, w_attn_in, mla_q_norm, w_mla_uq, mla_kv_norm, w_mla_ukv,
           gqa_sink, w_attn_out, w_rec_in, rec_lb_logits, rec_out_norm, w_rec_out, w_router, w_gate,
           w_up, w_down, final_norm):
    d = D_MODEL
    xc = x_prompt.reshape(BATCH * SEQ, d)
    xl = x_sample.reshape(DEC_BATCH * DEC_SEQ, d)
    cvecs = jnp.concatenate([c_ctx[None, :], c, jnp.zeros((N_MOD_ROWS - N_MOD_GROUPS, d), F32)], axis=0)
    mods_all = _mod_vectors(cvecs, w_mod, b_mod).reshape(DEPTH, N_MOD_ROWS, 6, d)[:, :N_MOD_GROUPS]
    grp_c = _group_map(None, 0, 0)
    cm, sm = _slot_tables(*_rope_tables(MLA_ROPE, 1))
    cg, sg = _rope_tables(GQA_HD, LANES // GQA_HD)
    kvw = GQA_KV_HEADS * GQA_HD

    caches, st_all = None, None
    for layer in range(DEPTH):
        mods = mods_all[layer]
        g1 = norm1[layer].reshape(1, d)
        g2 = norm2[layer].reshape(1, d)
        if layer % 2 == 0:
            a = layer // 2
            w_in = _attn_in_weight(w_attn_in[a])
            w_uq, w_ukv = _mla_weights(w_mla_uq[a], w_mla_ukv[a])
            w_out = w_attn_out[a].astype(BF16)
            qn_g = mla_q_norm[a].reshape(1, -1)
            kvn_g = mla_kv_norm[a].reshape(1, -1)
            sink = gqa_sink[a]
            zc = _modproj(xc, g1, mods, w_in, sub=0, bm=512, grp=grp_c)
            zl = _modproj(xl, g1, mods, w_in, sub=0, bm=512, grp=_group_map(DEC_SEQ, 512, 1))
            oc, *caches = _attn_ctx(zc, sink, qn_g, _heads_split(w_mla_uq[a], MLA_HEADS, MLA_NOPE, MLA_ROPE), kvn_g,
                                    _heads_split(w_mla_ukv[a], MLA_HEADS, MLA_NOPE, MLA_V), a, caches)
            qcat, kcat, vm, qg, kg, vg = _lat_prep(zl, qn_g, w_uq, kvn_g, w_ukv, cm, sm, cg, sg, bm=256)
            kr_c = jnp.pad(cache_mla_krope[:, a], ((0, 0), (0, 0), (KR_OFF, LANES - KR_OFF - MLA_ROPE)))
            kcat_c, vm_c = _cache_kv(cache_mla_ckv, kr_c, a, w_ukv)
            per_b = lambda x: x.reshape(DEC_BATCH, DEC_SEQ, -1)
            cat = lambda lat, ctx: jnp.concatenate([per_b(lat), ctx], axis=1)
            band = lambda x: jnp.pad(per_b(x), ((0, 0), (BAND_BLOCK, BAND_BLOCK), (0, 0)))
            ol = _lat_attn(sink, qcat, cat(kcat, kcat_c), cat(vm, vm_c), qg, band(kg), band(vg),
                           cache_gqa_k[:, a].reshape(DEC_BATCH, PAST_LEN, kvw).astype(BF16),
                           cache_gqa_v[:, a].reshape(DEC_BATCH, PAST_LEN, kvw).astype(BF16), qb=LAT_QB)
        else:
            r = layer // 2
            w_in = w_rec_in[r].astype(BF16)
            w_out = w_rec_out[r].astype(BF16)
            og = rec_out_norm[r].reshape(1, -1)
            zc = _modproj(xc, g1, mods, w_in, sub=0, bm=512, grp=grp_c)
            zl = _modproj(xl, g1, mods, w_in, sub=0, bm=512, grp=_group_map(DEC_SEQ, 512, 1))
            oc, st_all = _scan(zc, rec_lb_logits, og, r, t_len=SEQ, n_heads=8, emit_state=True, state_out=st_all)
            (ol,) = _scan(zl, rec_lb_logits, og, r, t_len=DEC_SEQ, n_heads=4, s0=state_hgrn)
        wr_t = w_router[layer].T.astype(BF16)
        xc, hc, aff_c = _outproj_router(oc, w_out, xc, mods, g2, wr_t, bm=1024, grp=grp_c)
        xl, hl, aff_l = _outproj_router(ol, w_out, xl, mods, g2, wr_t, bm=1024, grp=_group_map(DEC_SEQ, 1024, 1))
        gxc, gate_c, pos_c = _select(aff_c, hc, t_len=SEQ)
        gxl, gate_l, pos_l = _select(aff_l, hl, t_len=DEC_SEQ)
        yc, yl = _ffn(gxc, gxl, gate_c, gate_l, w_gate, w_up, w_down, layer, bf=512)
        fn = final_norm.reshape(1, d) if layer == DEPTH - 1 else None
        xc = _combine(pos_c, yc, xc, mods, t_len=SEQ, tt=SEQ, base=lambda s: 0, sets=COMBINE_SETS, final_gain=fn)
        xl = _combine(pos_l, yl, xl, mods, t_len=DEC_SEQ, tt=512, base=lambda s: 1 + s, final_gain=fn)

    y_prompt = xc.reshape(BATCH, SEQ, d)
    y_sample = xl.reshape(DEC_BATCH, DEC_SEQ, d)
    ckv_all, kr_all, kg_all, vg_all = caches
    heads = lambda x: x.reshape(BATCH, N_ATTN_LAYERS, SEQ, GQA_KV_HEADS, GQA_HD)
    return (y_prompt, y_sample, ckv_all, kr_all, heads(kg_all), heads(vg_all), st_all)
```

```python
import functools

import jax
import jax.numpy as jnp
from jax import lax
from jax.experimental import pallas as pl
from jax.experimental.pallas import tpu as pltpu

F32 = jnp.float32
BF16 = jnp.bfloat16

D_MODEL = 1024
BATCH = 32
SEQ = 256
DEPTH = 4
DEC_BATCH = 2
DEC_SEQ = 2048
PAST_LEN = 512
GRID_W = 64
ROPE_BASE = 10000.0
NORM_EPS = 1e-6
NEG_BIG = -1e30
F_MIN = 1e-6
N_ATTN_LAYERS = (DEPTH + 1) // 2
N_REC_LAYERS = DEPTH // 2
MLA_HEADS = 8
MLA_Q_LORA = 384
MLA_KV_LORA = 256
MLA_NOPE = 64
MLA_ROPE = 32
MLA_V = 64
MLA_SCALE = (MLA_NOPE + MLA_ROPE) ** -0.5
GQA_HEADS = 8
GQA_KV_HEADS = 2
GQA_GROUP = GQA_HEADS // GQA_KV_HEADS
GQA_HD = 64
GQA_SCALE = GQA_HD ** -0.5
WINDOW = 128
BAND_BLOCK = 128
HG_HEADS = 8
HG_DK = 128
HG_DV = 128
REC_IN = 3 * HG_HEADS * HG_DK + 2 * HG_HEADS * HG_DV
N_EXPERTS = 16
EC_CAPACITY = 2
D_FF = 2048

LANES = 128
N_MOD_ROWS = 8
N_MOD_GROUPS = 1 + DEC_BATCH

ZC_Q = 0
ZC_KV = ZC_Q + MLA_Q_LORA
ZC_QG = ZC_KV + MLA_KV_LORA
ZC_KG = ZC_QG + GQA_HEADS * GQA_HD
ZC_VG = ZC_KG + GQA_KV_HEADS * GQA_HD
ZC_KR = ZC_VG + GQA_KV_HEADS * GQA_HD
ATTN_Z = ZC_KR + LANES
KR_OFF = MLA_NOPE
QK_W = MLA_HEADS * LANES
VM_W = MLA_HEADS * MLA_V
O_M_W = MLA_HEADS * MLA_V
O_G_W = GQA_HEADS * GQA_HD

SUBLANES = 8
SCAN_CHUNK = 128
SCAN_LEVELS = (64, 32, 16, 8)
SCAN_DIAG = 8
SELECT_BLOCK = 2048
FFN_ROWS = 256
LAT_QB = 512
CTX_REQS = 4
COMBINE_SETS = 4
SCAN_SINGLE_BUFFER_BYTES = 4 << 20


def _dot(a, b):
    return jnp.dot(a, b, preferred_element_type=F32)


def _dot_nt(a, b):
    return lax.dot_general(a, b, (((1,), (1,)), ((), ())), preferred_element_type=F32)


def _dot_tn(a, b):
    return lax.dot_general(a, b, (((0,), (0,)), ((), ())), preferred_element_type=F32)


def _sigmoid(x):
    return 1.0 / (1.0 + jnp.exp(-x))


def _rms(x, gain):
    return x * lax.rsqrt(_row_sum(x * x) * (1.0 / x.shape[-1]) + NORM_EPS) * gain


def _mod_kernel(c_ref, w_ref, b_ref, o_ref):
    c = c_ref[...]
    s = (c * _sigmoid(c)).astype(BF16)
    o_ref[...] = _dot(s, w_ref[...].astype(BF16)) + b_ref[...]


def _mod_vectors(cvecs, w_mod, b_mod):
    bn = 1536
    nd = 6 * D_MODEL
    return pl.pallas_call(
        _mod_kernel,
        grid=(DEPTH, nd // bn),
        in_specs=[pl.BlockSpec((N_MOD_ROWS, D_MODEL), lambda l, j: (0, 0)),
                  pl.BlockSpec((None, D_MODEL, bn), lambda l, j: (l, 0, j)),
                  pl.BlockSpec((None, 1, bn), lambda l, j: (l, 0, j))],
        out_specs=pl.BlockSpec((None, N_MOD_ROWS, bn), lambda l, j: (l, 0, j)),
        out_shape=jax.ShapeDtypeStruct((DEPTH, N_MOD_ROWS, nd), F32),
        name="mod_vectors",
    )(cvecs, w_mod, b_mod.reshape(DEPTH, 1, nd))


def _group_map(rows_per_group, bm, base):
    if rows_per_group is None:
        return lambda i: base
    return lambda i: base + (i * bm) // rows_per_group


def _modproj_kernel(x_ref, g_ref, m_ref, w_ref, o_ref, *, shift_row, chunk):
    h = _rms(x_ref[...], g_ref[...])
    h = (h * (1.0 + m_ref[shift_row + 1:shift_row + 2, :]) + m_ref[shift_row:shift_row + 1, :]).astype(BF16)
    nout = o_ref.shape[1]
    for j in range(0, nout, chunk):
        o_ref[:, j:j + chunk] = _dot(h, w_ref[:, j:j + chunk])


def _modproj(x, gain, mods, w, *, sub, bm, grp):
    n, d = x.shape
    nout = w.shape[1]
    chunk = nout if nout <= 1536 else 1024
    kern = functools.partial(_modproj_kernel, shift_row=3 * sub, chunk=chunk)
    return pl.pallas_call(
        kern,
        grid=(n // bm,),
        in_specs=[pl.BlockSpec((bm, d), lambda i: (i, 0)),
                  pl.BlockSpec((1, d), lambda i: (0, 0)),
                  pl.BlockSpec((None, 6, d), lambda i: (grp(i), 0, 0)),
                  pl.BlockSpec((d, nout), lambda i: (0, 0), pipeline_mode=pl.Buffered(1))],
        out_specs=pl.BlockSpec((bm, nout), lambda i: (i, 0)),
        out_shape=jax.ShapeDtypeStruct((n, nout), F32),
        name="modproj",
    )(x, gain, mods, w)


def _outproj_router_kernel(a_ref, w_ref, x_ref, m_ref, g_ref, wr_ref, x_out, h_out, aff_out):
    x = x_ref[...] + m_ref[2:3, :] * _dot(a_ref[...], w_ref[...])
    x_out[...] = x
    h = (_rms(x, g_ref[...]) * (1.0 + m_ref[4:5, :]) + m_ref[3:4, :]).astype(BF16)
    h_out[...] = h
    logits = _dot_nt(wr_ref[...], h)
    e = jnp.exp(logits - jnp.max(logits, axis=0, keepdims=True))
    aff_out[...] = e / jnp.sum(e, axis=0, keepdims=True)


def _outproj_router(a, w, x, mods, gain2, wr_t, *, bm, grp):
    n, d = x.shape
    k = a.shape[1]
    row = lambda i: (i, 0)
    const = lambda i: (0, 0)
    return pl.pallas_call(
        _outproj_router_kernel,
        grid=(n // bm,),
        in_specs=[pl.BlockSpec((bm, k), row),
                  pl.BlockSpec((k, d), const),
                  pl.BlockSpec((bm, d), row),
                  pl.BlockSpec((None, 6, d), lambda i: (grp(i), 0, 0)),
                  pl.BlockSpec((1, d), const),
                  pl.BlockSpec((N_EXPERTS, d), const)],
        out_specs=[pl.BlockSpec((bm, d), row),
                   pl.BlockSpec((bm, d), row),
                   pl.BlockSpec((N_EXPERTS, bm), lambda i: (0, i))],
        out_shape=[jax.ShapeDtypeStruct((n, d), F32),
                   jax.ShapeDtypeStruct((n, d), BF16),
                   jax.ShapeDtypeStruct((N_EXPERTS, n), F32)],
        name="outproj_router",
    )(a, w, x, mods, gain2, wr_t)


def _lane_fold(x, op):
    w = x.shape[1]
    if w <= LANES or w % LANES:
        return x
    acc = x[:, :LANES]
    for j in range(1, w // LANES):
        acc = op(acc, x[:, j * LANES:(j + 1) * LANES])
    return acc


def _row_max(s):
    return jnp.max(_lane_fold(s, jnp.maximum), axis=-1, keepdims=True)


def _row_sum(p):
    return jnp.sum(_lane_fold(p, jnp.add), axis=-1, keepdims=True)


def _softmax_pv(s, v, sink=None):
    m = _row_max(s)
    if sink is not None:
        m = jnp.maximum(m, sink)
    p = jnp.exp(s - m)
    l = _row_sum(p)
    if sink is not None:
        l = l + jnp.exp(sink - m)
    return _dot(p.astype(BF16), v) / l


def _swap_pairs(x, q):
    w = x.shape[1]
    lane = lax.broadcasted_iota(jnp.int32, x.shape, 1)
    first = (lane % (2 * q)) < q
    return jnp.where(first, pltpu.roll(x, w - q, axis=1), pltpu.roll(x, q, axis=1))


def _rope(x, cos, sin, q):
    return x * cos + _swap_pairs(x, q) * sin


def _attn_project(z, qn_ref, wuq_ref, kvn_ref, wukv_ref):
    cq = _rms(z[:, ZC_Q:ZC_Q + MLA_Q_LORA], qn_ref[...]).astype(BF16)
    qcat = _dot(cq, wuq_ref[...])
    ckv_n = _rms(z[:, ZC_KV:ZC_KV + MLA_KV_LORA], kvn_ref[...])
    kv = _dot(ckv_n.astype(BF16), wukv_ref[...])
    return qcat, ckv_n, kv


def _tile_heads(kr_slot):
    return jnp.concatenate([kr_slot] * MLA_HEADS, axis=1)


def _staggered(n, first, second, depth):
    pending = []
    for i in range(n):
        pending.append((i, first(i)))
        if len(pending) > depth:
            second(*pending.pop(0))
    for item in pending:
        second(*item)


def _gqa_sink_column(sink_ref, g, rows):
    return jnp.concatenate([jnp.full((rows, 1), sink_ref[g * GQA_GROUP + j], F32) for j in range(GQA_GROUP)], axis=0)


def _stack_group(qg, g):
    return jnp.concatenate([qg[:, (g * GQA_GROUP + j) * GQA_HD:(g * GQA_GROUP + j + 1) * GQA_HD]
                            for j in range(GQA_GROUP)], axis=0)


def _attn_ctx_kernel(*refs, n_alias, a):
    sink_ref, z_ref, qn_ref, wuq_ref, kvn_ref, wukv_ref = refs[:6]
    o_ref, ckv_ref, kr_ref, kg_ref, vg_ref = refs[6 + n_alias:]
    t_len = SEQ
    z = z_ref[...]
    cq = _rms(z[:, ZC_Q:ZC_Q + MLA_Q_LORA], qn_ref[...]).astype(BF16)
    qm = _dot(cq, wuq_ref[...])
    ckv_n = _rms(z[:, ZC_KV:ZC_KV + MLA_KV_LORA], kvn_ref[...])
    kv = _dot(ckv_n.astype(BF16), wukv_ref[...])
    kr = z[:, ZC_KR + KR_OFF:ZC_KR + KR_OFF + MLA_ROPE]
    kg = z[:, ZC_KG:ZC_KG + GQA_KV_HEADS * GQA_HD]
    vg = z[:, ZC_VG:ZC_VG + GQA_KV_HEADS * GQA_HD]
    reqs = [slice(r * t_len, (r + 1) * t_len) for r in range(CTX_REQS)]
    for ref, val in ((ckv_ref, ckv_n), (kr_ref, kr), (kg_ref, kg), (vg_ref, vg)):
        for r, rows in enumerate(reqs):
            if n_alias:
                ref[r] = val[rows]
            else:
                for layer in range(N_ATTN_LAYERS):
                    ref[r, layer] = val[rows] if layer == a else jnp.zeros_like(val[rows])
    qn_w = MLA_HEADS * MLA_NOPE
    n_rows = z.shape[0]
    lane = lax.broadcasted_iota(jnp.int32, (n_rows, LANES), 1)
    low_half = lax.broadcasted_iota(jnp.int32, (t_len, LANES), 1) < MLA_V
    qn = qm[:, :qn_w] * MLA_SCALE
    qr = qm[:, qn_w:] * MLA_SCALE
    kn = kv[:, :qn_w].astype(BF16)
    vm = kv[:, qn_w:].astype(BF16)
    kr_slot = z[:, ZC_KR:ZC_KR + LANES]
    kr_rep = kr_slot
    for i in range(1, LANES // MLA_ROPE):
        kr_rep = kr_rep + pltpu.roll(kr_slot, i * MLA_ROPE, axis=1)
    kr_rep = kr_rep.astype(BF16)
    nope_per, rope_per = LANES // MLA_NOPE, LANES // MLA_ROPE
    held = {}

    def finish(h, r, rows, o, col0):
        tile = slice(col0 + h // 2 * LANES, col0 + (h // 2 + 1) * LANES)
        if h % 2 == 0:
            held[r] = o
        else:
            o_ref[rows, tile] = jnp.where(low_half, held.pop(r), o).astype(o_ref.dtype)

    for h in range(MLA_HEADS):
        nt = slice(h // nope_per * LANES, (h // nope_per + 1) * LANES)
        rt = slice(h // rope_per * LANES, (h // rope_per + 1) * LANES)
        q_n = jnp.where(lane // MLA_NOPE == h % nope_per, qn[:, nt], 0.0)
        q_r = jnp.where(lane // MLA_ROPE == h % rope_per, qr[:, rt], 0.0)
        q2 = jnp.concatenate([q_n, q_r], axis=1).astype(BF16)
        k2 = jnp.concatenate([kn[:, nt], kr_rep], axis=1)
        ss = [_dot_nt(q2[rows], k2[rows]) for rows in reqs]
        for r, (rows, s) in enumerate(zip(reqs, ss)):
            finish(h, r, rows, _softmax_pv(s, vm[rows, nt]), 0)
    qg = z[:, ZC_QG:ZC_QG + O_G_W] * GQA_SCALE
    kgs = [kg.astype(BF16), pltpu.roll(kg, GQA_HD, axis=1).astype(BF16)]
    vgs = [vg.astype(BF16), pltpu.roll(vg, GQA_HD, axis=1).astype(BF16)]
    for h in range(GQA_HEADS):
        g = h // GQA_GROUP
        par = h % 2
        tile = slice(h // 2 * LANES, (h // 2 + 1) * LANES)
        q_h = jnp.where(lane // GQA_HD == par, qg[:, tile], 0.0).astype(BF16)
        k_h = kgs[0 if par == g else 1]
        v_h = vgs[0 if par == g else 1]
        ss = [_dot_nt(q_h[rows], k_h[rows]) for rows in reqs]
        for r, (rows, s) in enumerate(zip(reqs, ss)):
            finish(h, r, rows, _softmax_pv(s, v_h[rows], sink_ref[h]), O_M_W)


def _attn_ctx(z, sink, q_norm, w_uq, kv_norm, w_ukv, a, caches=None):
    nb = z.shape[0] // SEQ
    rq = CTX_REQS
    kvw = GQA_KV_HEADS * GQA_HD
    const = lambda b: (0, 0)
    row = lambda b: (b, 0)
    cache_w = (MLA_KV_LORA, MLA_ROPE, kvw, kvw)
    in_specs = [pl.BlockSpec(memory_space=pltpu.SMEM),
                pl.BlockSpec((rq * SEQ, ATTN_Z), row),
                pl.BlockSpec((1, MLA_Q_LORA), const),
                pl.BlockSpec(w_uq.shape, const),
                pl.BlockSpec((1, MLA_KV_LORA), const),
                pl.BlockSpec(w_ukv.shape, const)]
    args = [sink, z, q_norm, w_uq, kv_norm, w_ukv]
    aliases = {}
    if caches is not None:
        for j, c in enumerate(caches):
            aliases[len(args)] = 1 + j
            in_specs.append(pl.BlockSpec(memory_space=pl.ANY))
            args.append(c)
    kern = functools.partial(_attn_ctx_kernel, n_alias=len(aliases), a=a)
    if aliases:
        cache_specs = [pl.BlockSpec((rq, None, SEQ, w), lambda b: (b, a, 0, 0)) for w in cache_w]
    else:
        cache_specs = [pl.BlockSpec((rq, N_ATTN_LAYERS, SEQ, w), lambda b: (b, 0, 0, 0)) for w in cache_w]
    return pl.pallas_call(
        kern,
        grid=(nb // rq,),
        in_specs=in_specs,
        out_specs=[pl.BlockSpec((rq * SEQ, O_M_W + O_G_W), row)] + cache_specs,
        out_shape=[jax.ShapeDtypeStruct((nb * SEQ, O_M_W + O_G_W), BF16)]
        + [jax.ShapeDtypeStruct((nb, N_ATTN_LAYERS, SEQ, w), F32) for w in cache_w],
        input_output_aliases=aliases,
        name="attn_ctx",
    )(*args)


def _lat_prep_kernel(z_ref, qn_ref, wuq_ref, kvn_ref, wukv_ref, cm_ref, sm_ref, cg_ref, sg_ref,
                     qcat_o, kcat_o, vm_o, qg_o, kg_o, vg_o):
    z = z_ref[...]
    qcat, ckv_n, kv = _attn_project(z, qn_ref, wuq_ref, kvn_ref, wukv_ref)
    cm = _tile_heads(cm_ref[...])
    sm = _tile_heads(sm_ref[...])
    cg = jnp.concatenate([cg_ref[...]] * (O_G_W // LANES), axis=1)
    sg = jnp.concatenate([sg_ref[...]] * (O_G_W // LANES), axis=1)
    qcat_o[...] = (_rope(qcat, cm, sm, MLA_ROPE // 4) * MLA_SCALE).astype(BF16)
    kr_slot = _rope(z[:, ZC_KR:ZC_KR + LANES], cm[:, :LANES], sm[:, :LANES], MLA_ROPE // 4)
    kcat_o[...] = (kv[:, :QK_W] + _tile_heads(kr_slot)).astype(BF16)
    vm_o[...] = kv[:, QK_W:].astype(BF16)
    qg_o[...] = (_rope(z[:, ZC_QG:ZC_QG + O_G_W], cg, sg, GQA_HD // 4) * GQA_SCALE).astype(BF16)
    kvw = GQA_KV_HEADS * GQA_HD
    kg_o[...] = _rope(z[:, ZC_KG:ZC_KG + kvw], cg[:, :kvw], sg[:, :kvw], GQA_HD // 4).astype(BF16)
    vg_o[...] = z[:, ZC_VG:ZC_VG + kvw].astype(BF16)


def _lat_prep(z, q_norm, w_uq, kv_norm, w_ukv, cm, sm, cg, sg, *, bm):
    n = z.shape[0]
    per = DEC_SEQ // bm
    kvw = GQA_KV_HEADS * GQA_HD
    const = lambda i: (0, 0)
    row = lambda i: (i, 0)
    pos = lambda i: (i % per, 0)
    widths = (QK_W, QK_W, VM_W, O_G_W, kvw, kvw)
    return pl.pallas_call(
        _lat_prep_kernel,
        grid=(n // bm,),
        in_specs=[pl.BlockSpec((bm, ATTN_Z), row),
                  pl.BlockSpec((1, MLA_Q_LORA), const),
                  pl.BlockSpec(w_uq.shape, const),
                  pl.BlockSpec((1, MLA_KV_LORA), const),
                  pl.BlockSpec(w_ukv.shape, const),
                  pl.BlockSpec((bm, LANES), pos),
                  pl.BlockSpec((bm, LANES), pos),
                  pl.BlockSpec((bm, LANES), pos),
                  pl.BlockSpec((bm, LANES), pos)],
        out_specs=[pl.BlockSpec((bm, w), row) for w in widths],
        out_shape=[jax.ShapeDtypeStruct((n, w), BF16) for w in widths],
        name="lat_prep",
    )(z, q_norm, w_uq, kv_norm, w_ukv, cm, sm, cg, sg)


def _cache_kv_kernel(c_ref, kr_ref, w_ref, kcat_o, vm_o):
    kv = _dot(c_ref[...].astype(BF16), w_ref[...])
    kcat_o[...] = (kv[:, :QK_W] + _tile_heads(kr_ref[...])).astype(BF16)
    vm_o[...] = kv[:, QK_W:].astype(BF16)


def _cache_kv(cache_ckv, kr_slot, a, w_ukv):
    nb = cache_ckv.shape[0]
    return pl.pallas_call(
        _cache_kv_kernel,
        grid=(nb,),
        in_specs=[pl.BlockSpec((None, None, PAST_LEN, MLA_KV_LORA), lambda b: (b, a, 0, 0)),
                  pl.BlockSpec((None, PAST_LEN, LANES), lambda b: (b, 0, 0)),
                  pl.BlockSpec(w_ukv.shape, lambda b: (0, 0))],
        out_specs=[pl.BlockSpec((None, PAST_LEN, QK_W), lambda b: (b, 0, 0)),
                   pl.BlockSpec((None, PAST_LEN, VM_W), lambda b: (b, 0, 0))],
        out_shape=[jax.ShapeDtypeStruct((nb, PAST_LEN, QK_W), BF16),
                   jax.ShapeDtypeStruct((nb, PAST_LEN, VM_W), BF16)],
        name="cache_kv",
    )(cache_ckv, kr_slot, w_ukv)


def _lat_attn_kernel(sink_ref, qcat_ref, kcat_ref, vm_ref, qg_ref, kg_ref, vg_ref, kctx_ref, vctx_ref, o_ref):
    n = pl.program_id(1)
    qcat = qcat_ref[...]

    def mla_scores(h):
        return _dot_nt(qcat[:, h * LANES:(h + 1) * LANES], kcat_ref[:, h * LANES:(h + 1) * LANES])

    per_tile = LANES // MLA_V
    held = []
    low_half = lax.broadcasted_iota(jnp.int32, (qcat.shape[0], LANES), 1) < MLA_V

    def mla_out(h, s):
        j = h // per_tile
        held.append(_softmax_pv(s, vm_ref[:, j * LANES:(j + 1) * LANES]))
        if len(held) == per_tile:
            o_ref[:, j * LANES:(j + 1) * LANES] = jnp.where(low_half, held[0], held[1]).astype(o_ref.dtype)
            held.clear()

    _staggered(MLA_HEADS, mla_scores, mla_out, depth=2)

    bb = BAND_BLOCK
    nsub = qg_ref.shape[0] // bb
    rows = GQA_GROUP * bb
    tq = lax.broadcasted_iota(jnp.int32, (rows, 3 * bb), 0) & (bb - 1)
    kk = lax.broadcasted_iota(jnp.int32, (rows, 3 * bb), 1)
    qg = qg_ref[...]
    kctx = kctx_ref[...]
    vctx = vctx_ref[...]

    def gqa_scores(i):
        j, g = divmod(i, GQA_KV_HEADS)
        hs = slice(g * GQA_HD, (g + 1) * GQA_HD)
        blk = n * nsub + j
        start = pl.multiple_of(blk * bb, bb)
        lo = jnp.maximum(tq, (1 - blk) * bb)
        hi = jnp.minimum(tq + 2 * WINDOW, (DEC_SEQ // bb + 1 - blk) * bb - 1)
        q = _stack_group(qg[j * bb:(j + 1) * bb, :], g)
        s_loc = jnp.where((kk >= lo) & (kk <= hi), _dot_nt(q, kg_ref[pl.ds(start, 3 * bb), hs]), NEG_BIG)
        return s_loc, _dot_nt(q, kctx[:, hs]), start

    def gqa_out(i, res):
        j, g = divmod(i, GQA_KV_HEADS)
        hs = slice(g * GQA_HD, (g + 1) * GQA_HD)
        s_loc, s_ctx, start = res
        sink = _gqa_sink_column(sink_ref, g, bb)
        m = jnp.maximum(_row_max(jnp.maximum(_lane_fold(s_loc, jnp.maximum), _lane_fold(s_ctx, jnp.maximum))), sink)
        p_loc = jnp.exp(s_loc - m)
        p_ctx = jnp.exp(s_ctx - m)
        l = _row_sum(_lane_fold(p_loc, jnp.add) + _lane_fold(p_ctx, jnp.add)) + jnp.exp(sink - m)
        o = (_dot(p_loc.astype(BF16), vg_ref[pl.ds(start, 3 * bb), hs]) + _dot(p_ctx.astype(BF16), vctx[:, hs])) / l
        for jj in range(GQA_GROUP):
            h = g * GQA_GROUP + jj
            o_ref[j * bb:(j + 1) * bb, O_M_W + h * GQA_HD:O_M_W + (h + 1) * GQA_HD] = (
                o[jj * bb:(jj + 1) * bb].astype(o_ref.dtype))

    _staggered(nsub * GQA_KV_HEADS, gqa_scores, gqa_out, depth=3)


def _lat_attn(sink, qcat, kcat_all, vm_all, qg, kg_pad, vg_pad, kg_ctx, vg_ctx, *, qb):
    nb = DEC_BATCH
    nq = DEC_SEQ // qb
    tk = kcat_all.shape[1]
    tp = kg_pad.shape[1]
    kvw = GQA_KV_HEADS * GQA_HD
    qrow = lambda b, n: (b * nq + n, 0)
    per_b = lambda b, n: (b, 0, 0)
    return pl.pallas_call(
        _lat_attn_kernel,
        grid=(nb, nq),
        in_specs=[pl.BlockSpec(memory_space=pltpu.SMEM),
                  pl.BlockSpec((qb, QK_W), qrow),
                  pl.BlockSpec((None, tk, QK_W), per_b),
                  pl.BlockSpec((None, tk, VM_W), per_b),
                  pl.BlockSpec((qb, O_G_W), qrow),
                  pl.BlockSpec((None, tp, kvw), per_b),
                  pl.BlockSpec((None, tp, kvw), per_b),
                  pl.BlockSpec((None, PAST_LEN, kvw), per_b),
                  pl.BlockSpec((None, PAST_LEN, kvw), per_b)],
        out_specs=pl.BlockSpec((qb, O_M_W + O_G_W), qrow),
        out_shape=jax.ShapeDtypeStruct((nb * DEC_SEQ, O_M_W + O_G_W), BF16),
        name="lat_attn",
    )(sink, qcat, kcat_all, vm_all, qg, kg_pad, vg_pad, kg_ctx, vg_ctx)


def _row_tiles(x):
    return [x[i * SUBLANES:(i + 1) * SUBLANES, :] for i in range(x.shape[0] // SUBLANES)]


def _scan_masks(mask_ref):
    c = SCAN_CHUNK
    t = lax.broadcasted_iota(jnp.int32, (c, c), 0)
    s = lax.broadcasted_iota(jnp.int32, (c, c), 1)
    for li, m in enumerate(SCAN_LEVELS):
        sh = m.bit_length() - 1
        mask_ref[li] = jnp.where((t >> sh) == (s >> sh), 1.0, 0.0)
    sh = SCAN_DIAG.bit_length() - 1
    same = (t >> sh) == (s >> sh)
    mask_ref[len(SCAN_LEVELS)] = jnp.where(same & (s <= t), 1.0, 0.0)
    mask_ref[len(SCAN_LEVELS) + 1] = jnp.where(same & (s >= t), 1.0, 0.0)


def _level_rows(m, rev):
    mt = m // SUBLANES
    nt = SCAN_CHUNK // SUBLANES
    groups = []
    for g in range(nt // (2 * mt)):
        first = list(range(g * 2 * mt, g * 2 * mt + mt))
        second = list(range(g * 2 * mt + mt, (g + 1) * 2 * mt))
        if rev:
            groups.append((first, second, second[0], 0))
        else:
            groups.append((second, first, first[-1], SUBLANES - 1))
    return groups


def _scan_chunks(chains, mask_ref):
    c = SCAN_CHUNK
    half = c // 2
    nt = c // SUBLANES
    n_lv = len(SCAN_LEVELS)
    work = []
    for q, zf, v, lb, st, tri, rev in chains:
        f = lb + (1.0 - lb) * _sigmoid(zf)
        lf = jnp.log2(jnp.maximum(f, F_MIN))
        hi = lf.astype(BF16)
        mid = (lf - hi.astype(F32)).astype(BF16)
        work.append(dict(q=q, k=1.0 - f, v=v, vb=v.astype(BF16), st=st, tri=tri, rev=rev, hi=hi, mid=mid))
    for w in work:
        w["b"] = _dot(w["tri"], w["hi"]) + _dot(w["tri"], w["mid"])
        w["qt"], w["kt"], w["bt"], w["vt"] = (_row_tiles(w[n]) for n in ("q", "k", "b", "v"))
    for w in work:
        mid_row = SCAN_DIAG // 2 if w["rev"] else SCAN_DIAG // 2 - 1
        e = jnp.concatenate([t - t[mid_row:mid_row + 1, :] for t in w["bt"]], axis=0)
        a = _dot_nt((w["q"] * jnp.exp2(e)).astype(BF16), (w["k"] * jnp.exp2(-e)).astype(BF16))
        w["a"] = jnp.where(mask_ref[n_lv + (1 if w["rev"] else 0)] > 0.5, a, 0.0).astype(BF16)
    for w in work:
        o = _dot(w["a"], w["vb"]) + _dot_nt((w["q"] * jnp.exp2(w["b"])).astype(BF16), w["st"].astype(BF16))
        w["o"] = _row_tiles(o)
    for li, m in enumerate(SCAN_LEVELS):
        for w in work:
            q_parts, k_parts, w["q_idx"], w["k_idx"] = [], [], [], []
            for q_rows, k_rows, rt, rr in _level_rows(m, w["rev"]):
                r = w["bt"][rt][rr:rr + 1, :]
                q_parts += [w["qt"][i] * jnp.exp2(w["bt"][i] - r) for i in q_rows]
                k_parts += [w["kt"][i] * jnp.exp2(r - w["bt"][i]) for i in k_rows]
                w["q_idx"] += q_rows
                w["k_idx"] += k_rows
            al = _dot_nt(jnp.concatenate(q_parts, axis=0).astype(BF16),
                         jnp.concatenate(k_parts, axis=0).astype(BF16))
            if m != half:
                al = al * mask_ref[li, :half, :half]
            w["al"] = al.astype(BF16)
        for w in work:
            vl = jnp.concatenate([w["vt"][i] for i in w["k_idx"]], axis=0).astype(BF16)
            ol = _row_tiles(_dot(w["al"], vl))
            for j, i in enumerate(w["q_idx"]):
                w["o"][i] = w["o"][i] + ol[j]
    outs = []
    for w in work:
        b_tot = w["bt"][0][0:1, :] if w["rev"] else w["bt"][nt - 1][SUBLANES - 1:SUBLANES, :]
        k_end = (w["k"] * jnp.exp2(b_tot - w["b"])).astype(BF16)
        st_new = w["st"] * jnp.exp2(b_tot) + _dot_tn(w["vb"], k_end)
        outs.append((jnp.concatenate(w["o"], axis=0), st_new))
    return outs


def _scan_kernel(*refs, r_layer, has_s0, emit_state, has_alias, n_heads):
    lbl_ref, zq_ref, zff_ref, zfb_ref, zi_ref, zg_ref, gain_ref = refs[:7]
    pos = 7
    s0_ref = None
    if has_s0:
        s0_ref = refs[pos]
        pos += 1
    if has_alias:
        pos += 1
    o_ref = refs[pos]
    pos += 1
    st_ref = None
    if emit_state:
        st_ref = refs[pos]
        pos += 1
    q_scr, o_scr, st_scr, mask_scr = refs[pos:pos + 4]

    c = SCAN_CHUNK
    w = HG_DK
    t_len = zq_ref.shape[0]
    n_chunks = t_len // c
    lg = lbl_ref[...]
    mx = lg[0]
    for j in range(1, N_REC_LAYERS):
        mx = jnp.maximum(mx, lg[j])
    ex = [jnp.exp(lg[j] - mx) for j in range(N_REC_LAYERS)]
    den = ex[0]
    for j in range(1, N_REC_LAYERS):
        den = den + ex[j]
    lower = jnp.zeros_like(mx)
    for j in range(1, r_layer + 1):
        lower = lower + ex[j] / den

    zq = zq_ref[...]
    q_scr[...] = zq * _sigmoid(zq)
    _scan_masks(mask_scr)
    row = lax.broadcasted_iota(jnp.int32, (c, c), 0)
    col = lax.broadcasted_iota(jnp.int32, (c, c), 1)
    tris = [jnp.where(col <= row, 1.0, 0.0).astype(BF16), jnp.where(col >= row, 1.0, 0.0).astype(BF16)]
    zf_refs = (zff_ref, zfb_ref)
    for d in range(2):
        for hh in range(n_heads):
            st_scr[d, hh] = s0_ref[d, hh].T if has_s0 else jnp.zeros((HG_DV, HG_DK), F32)

    def body(i, carry):
        chains = []
        for hh in range(n_heads):
            lanes = slice(hh * w, (hh + 1) * w)
            for d in range(2):
                ci = (n_chunks - 1 - i) if d == 1 else i
                rows = pl.ds(ci * c, c) if isinstance(ci, int) else pl.ds(pl.multiple_of(ci * c, c), c)
                chains.append((d, hh, rows, lanes, q_scr[rows, lanes], zf_refs[d][rows, lanes],
                               zi_ref[rows, lanes], st_scr[d, hh]))
        outs = _scan_chunks([(q, zf, v, lower[d:d + 1, lanes], st, tris[d], d == 1)
                             for d, hh, rows, lanes, q, zf, v, st in chains], mask_scr)
        for (d, hh, rows, lanes, *_), (o, st_new) in zip(chains, outs):
            st_scr[d, hh] = st_new
            o_scr[d, rows, lanes] = o
        return carry

    if n_chunks <= 2:
        for i in range(n_chunks):
            body(i, 0)
    else:
        lax.fori_loop(0, n_chunks, body, 0)
    if emit_state:
        for d in range(2):
            for hh in range(n_heads):
                if has_alias:
                    st_ref[d, hh] = st_scr[d, hh].T
                else:
                    for layer in range(N_REC_LAYERS):
                        st_ref[layer, d, hh] = (st_scr[d, hh].T if layer == r_layer
                                                else jnp.zeros((HG_DK, HG_DV), F32))

    zg = zg_ref[...]
    gate = zg * _sigmoid(zg)
    for hh in range(n_heads):
        lanes = slice(hh * w, (hh + 1) * w)
        y = _rms(o_scr[0, :, lanes] + o_scr[1, :, lanes], gain_ref[:, lanes]) * gate[:, lanes]
        o_ref[:, lanes] = y.astype(o_ref.dtype)


def _scan(z, lb_logits, out_gain, r_layer, *, t_len, n_heads, s0=None, emit_state=False, state_out=None):
    nb = z.shape[0] // t_len
    nh = HG_HEADS
    sh = n_heads
    ng = nh // sh
    wb = sh * HG_DK
    col = lambda k: (lambda b, h: (b, k * ng + h))
    in_specs = [pl.BlockSpec((N_REC_LAYERS, 2, wb), lambda b, h: (0, 0, h))]
    big = t_len * wb * 4 >= SCAN_SINGLE_BUFFER_BYTES
    single = dict(pipeline_mode=pl.Buffered(1))
    in_specs += [pl.BlockSpec((t_len, wb), col(k), **(single if big and k in (0, 4) else {})) for k in range(5)]
    in_specs += [pl.BlockSpec((1, wb), lambda b, h: (0, h))]
    args = [lb_logits, z, z, z, z, z, out_gain]
    if s0 is not None:
        in_specs.append(pl.BlockSpec((None, None, 2, sh, HG_DK, HG_DV),
                                     lambda b, h: (b, r_layer, 0, h, 0, 0)))
        args.append(s0)
    out_specs = [pl.BlockSpec((t_len, wb), lambda b, h: (b, h))]
    out_shape = [jax.ShapeDtypeStruct((nb * t_len, nh * HG_DV), BF16)]
    aliases = {}
    if emit_state:
        if state_out is None:
            out_specs.append(pl.BlockSpec((None, N_REC_LAYERS, 2, sh, HG_DK, HG_DV), lambda b, h: (b, 0, 0, h, 0, 0)))
        else:
            out_specs.append(pl.BlockSpec((None, None, 2, sh, HG_DK, HG_DV), lambda b, h: (b, r_layer, 0, h, 0, 0)))
            aliases[len(args)] = 1
            in_specs.append(pl.BlockSpec(memory_space=pl.ANY))
            args.append(state_out)
        out_shape.append(jax.ShapeDtypeStruct((nb, N_REC_LAYERS, 2, nh, HG_DK, HG_DV), F32))
    kern = functools.partial(_scan_kernel, r_layer=r_layer, has_s0=s0 is not None, emit_state=emit_state,
                             has_alias=bool(aliases), n_heads=n_heads)
    return pl.pallas_call(
        kern,
        grid=(nb, ng),
        in_specs=in_specs,
        out_specs=out_specs,
        out_shape=out_shape,
        input_output_aliases=aliases,
        scratch_shapes=[pltpu.VMEM((t_len, wb), F32),
                        pltpu.VMEM((2, t_len, wb), F32),
                        pltpu.VMEM((2, sh, HG_DV, HG_DK), F32),
                        pltpu.VMEM((len(SCAN_LEVELS) + 2, SCAN_CHUNK, SCAN_CHUNK), F32)],
        name="hgrn_scan",
    )(*args)


def _one_hot_rows(posm, e0, g, cap):
    t_len = posm.shape[1]
    slot = lax.broadcasted_iota(jnp.int32, (cap, t_len), 0)
    parts = [jnp.where(posm[e:e + 1, :] == slot, 1.0, 0.0).astype(BF16) for e in range(e0, e0 + g)]
    return parts[0] if g == 1 else jnp.concatenate(parts, axis=0)


PREFIX_PIECE = 256


def _prefix_count(mask_f, before):
    pieces = []
    carry = jnp.zeros((mask_f.shape[0], 1), F32)
    for p0 in range(0, mask_f.shape[1], PREFIX_PIECE):
        piece = mask_f[:, p0:p0 + PREFIX_PIECE]
        pieces.append(_dot(piece.astype(BF16), before) + carry)
        carry = carry + jnp.sum(piece, axis=1, keepdims=True)
    return pieces[0] if len(pieces) == 1 else jnp.concatenate(pieces, axis=1)


def _sort_desc_segments(x, seg):
    w = x.shape[1]
    lane = lax.broadcasted_iota(jnp.int32, x.shape, 1)
    k = 2
    while k <= seg:
        j = k // 2
        while j >= 1:
            is_first = (lane & j) == 0
            partner = jnp.where(is_first, pltpu.roll(x, w - j, axis=1), pltpu.roll(x, j, axis=1))
            keep_max = is_first if k == seg else (is_first == ((lane & k) == 0))
            x = jnp.where(keep_max, jnp.maximum(x, partner), jnp.minimum(x, partner))
            j //= 2
        k *= 2
    return x


def _select_kernel(aff_ref, h_ref, g_ref, gate_ref, pos_ref, *, t_len):
    a = aff_ref[...]
    ne, w = a.shape
    cap = EC_CAPACITY * t_len // N_EXPERTS
    capf = jnp.float32(cap)
    srt = jnp.concatenate([_sort_desc_segments(a[r0:r0 + SUBLANES, :], t_len)
                           for r0 in range(0, ne, SUBLANES)], axis=0)
    src = lax.broadcasted_iota(jnp.int32, (PREFIX_PIECE, PREFIX_PIECE), 0)
    dst = lax.broadcasted_iota(jnp.int32, (PREFIX_PIECE, PREFIX_PIECE), 1)
    before = jnp.where(src < dst, 1.0, 0.0).astype(BF16)
    slot = lax.broadcasted_iota(jnp.int32, (cap, t_len), 0)
    g = max(1, min(ne, 512 // cap))
    n_sets = w // t_len
    segs = [a[:, s * t_len:(s + 1) * t_len] for s in range(n_sets)]
    thrs = [srt[:, s * t_len + cap - 1:s * t_len + cap] for s in range(n_sets)]
    gts = [jnp.where(seg > thr, 1.0, 0.0) for seg, thr in zip(segs, thrs)]
    eqs = [jnp.where(seg == thr, 1.0, 0.0) for seg, thr in zip(segs, thrs)]
    needs = [capf - jnp.sum(gt, axis=1, keepdims=True) for gt in gts]
    eq_ranks = [_prefix_count(eq, before) for eq in eqs]
    sels = [gt + eq * jnp.where(rank < need, 1.0, 0.0) for gt, eq, rank, need in zip(gts, eqs, eq_ranks, needs)]
    poss = [_prefix_count(sel, before) for sel in sels]
    posms = [jnp.where(sel > 0.5, pos.astype(jnp.int32), -1) for sel, pos in zip(sels, poss)]
    for s, posm in enumerate(posms):
        pos_ref[:, s * t_len:(s + 1) * t_len] = posm
    for s, posm in enumerate(posms):
        h = h_ref[s * t_len:(s + 1) * t_len, :]
        for e0 in range(0, ne, g):
            rows = _dot(_one_hot_rows(posm, e0, g, cap), h)
            g_ref[e0:e0 + g, s * cap:(s + 1) * cap, :] = rows.reshape(g, cap, h.shape[1]).astype(g_ref.dtype)
    for s, (posm, seg) in enumerate(zip(posms, segs)):
        for e in range(ne):
            gate_ref[e, s * cap:(s + 1) * cap, :] = jnp.sum(jnp.where(posm[e:e + 1, :] == slot, seg[e:e + 1, :], 0.0),
                                                            axis=1, keepdims=True)


def _select(aff_t, h, *, t_len):
    ne, n = aff_t.shape
    d = h.shape[1]
    w = SELECT_BLOCK
    capw = EC_CAPACITY * w // N_EXPERTS
    kern = functools.partial(_select_kernel, t_len=t_len)
    return pl.pallas_call(
        kern,
        grid=(n // w,),
        in_specs=[pl.BlockSpec((ne, w), lambda s: (0, s)),
                  pl.BlockSpec((w, d), lambda s: (s, 0))],
        out_specs=[pl.BlockSpec((ne, capw, d), lambda s: (0, s, 0)),
                   pl.BlockSpec((ne, capw, 1), lambda s: (0, s, 0)),
                   pl.BlockSpec((ne, w), lambda s: (0, s))],
        out_shape=[jax.ShapeDtypeStruct((ne, n // w * capw, d), BF16),
                   jax.ShapeDtypeStruct((ne, n // w * capw, 1), F32),
                   jax.ShapeDtypeStruct((ne, n), jnp.int32)],
        name="ec_select",
    )(aff_t, h)


def _ffn_kernel(xc_ref, xl_ref, gc_ref, gl_ref, wg_ref, wu_ref, wd_ref, yc_ref, yl_ref, accc, accl):
    f = pl.program_id(1)
    chunks = [(x_ref, acc, r0) for x_ref, acc in ((xc_ref, accc), (xl_ref, accl))
              for r0 in range(0, x_ref.shape[0], FFN_ROWS)]

    @pl.when(f == 0)
    def _():
        accc[...] = jnp.zeros_like(accc)
        accl[...] = jnp.zeros_like(accl)

    w = {}

    def weight(name, ref):
        if name not in w:
            w[name] = ref[...].astype(BF16)
        return w[name]

    def up(x_ref, acc, r0):
        x = x_ref[r0:r0 + FFN_ROWS, :]
        hg = _dot(x, weight("g", wg_ref))
        return hg, _dot(x, weight("u", wu_ref))

    def down(x_ref, acc, r0, hg, hu):
        hid = (hg * _sigmoid(hg) * hu).astype(BF16)
        acc[r0:r0 + FFN_ROWS, :] = acc[r0:r0 + FFN_ROWS, :] + _dot(hid, weight("d", wd_ref))

    pending = None
    for ch in chunks:
        cur = up(*ch)
        if pending is not None:
            down(*pending)
        pending = ch + cur
    down(*pending)

    @pl.when(f == pl.num_programs(1) - 1)
    def _():
        yc_ref[...] = (accc[...] * gc_ref[...]).astype(yc_ref.dtype)
        yl_ref[...] = (accl[...] * gl_ref[...]).astype(yl_ref.dtype)


def _ffn(xc, xl, gc, gl, w_gate, w_up, w_down, layer, *, bf):
    ne, nc, d = xc.shape
    nl = xl.shape[1]
    ff = w_gate.shape[3]
    per_e = lambda e, f: (e, 0, 0)
    return pl.pallas_call(
        _ffn_kernel,
        grid=(ne, ff // bf),
        in_specs=[pl.BlockSpec((None, nc, d), per_e),
                  pl.BlockSpec((None, nl, d), per_e),
                  pl.BlockSpec((None, nc, 1), per_e),
                  pl.BlockSpec((None, nl, 1), per_e),
                  pl.BlockSpec((None, None, d, bf), lambda e, f: (layer, e, 0, f)),
                  pl.BlockSpec((None, None, d, bf), lambda e, f: (layer, e, 0, f)),
                  pl.BlockSpec((None, None, bf, d), lambda e, f: (layer, e, f, 0))],
        out_specs=[pl.BlockSpec((None, nc, d), per_e),
                   pl.BlockSpec((None, nl, d), per_e)],
        out_shape=[jax.ShapeDtypeStruct((ne, nc, d), BF16),
                   jax.ShapeDtypeStruct((ne, nl, d), BF16)],
        scratch_shapes=[pltpu.VMEM((nc, d), F32), pltpu.VMEM((nl, d), F32)],
        name="expert_ffn",
    )(xc, xl, gc, gl, w_gate, w_up, w_down)


def _combine_kernel(pos_ref, y_ref, x_ref, m_ref, fn_ref, o_ref, *, cap, sets):
    ne = pos_ref.shape[0]
    d = x_ref.shape[1]
    tt = x_ref.shape[0] // sets
    g = max(1, min(ne, 512 // cap))
    for s in range(sets):
        posm = pos_ref[:, s * tt:(s + 1) * tt]
        acc = jnp.zeros((tt, d), F32)
        for e0 in range(0, ne, g):
            acc = acc + _dot_tn(_one_hot_rows(posm, e0, g, cap),
                                y_ref[e0:e0 + g, s * cap:(s + 1) * cap, :].reshape(g * cap, d))
        x = x_ref[s * tt:(s + 1) * tt, :] + m_ref[5:6, :] * acc
        o_ref[s * tt:(s + 1) * tt, :] = x if fn_ref is None else _rms(x, fn_ref[...])


def _combine(posm, y, x, mods, *, t_len, tt, base, sets=1, final_gain=None):
    ne, n = posm.shape
    d = x.shape[1]
    ns = n // t_len
    cap = y.shape[1] // ns
    nt = t_len // tt
    assert sets == 1 or nt == 1
    in_specs = [pl.BlockSpec((ne, sets * tt), lambda s, j: (0, s * nt + j)),
                pl.BlockSpec((ne, sets * cap, d), lambda s, j: (0, s, 0)),
                pl.BlockSpec((sets * tt, d), lambda s, j: (s * nt + j, 0)),
                pl.BlockSpec((None, 6, d), lambda s, j: (base(s), 0, 0))]
    args = [posm, y, x, mods]
    if final_gain is None:
        kern = lambda p, yr, xr, m, o: _combine_kernel(p, yr, xr, m, None, o, cap=cap, sets=sets)
    else:
        kern = functools.partial(_combine_kernel, cap=cap, sets=sets)
        in_specs.append(pl.BlockSpec((1, d), lambda s, j: (0, 0)))
        args.append(final_gain)
    return pl.pallas_call(
        kern,
        grid=(ns // sets, nt),
        in_specs=in_specs,
        out_specs=pl.BlockSpec((sets * tt, d), lambda s, j: (s * nt + j, 0)),
        out_shape=jax.ShapeDtypeStruct((n, d), F32),
        name="ec_combine",
    )(*args)


def _attn_in_weight(w):
    o_kr = MLA_Q_LORA + MLA_KV_LORA
    o_qg = o_kr + MLA_ROPE
    cols = jnp.concatenate([w[:, :o_kr], w[:, o_qg:], jnp.zeros((w.shape[0], KR_OFF), w.dtype), w[:, o_kr:o_qg]],
                           axis=1)
    return jnp.pad(cols, ((0, 0), (0, ATTN_Z - cols.shape[1]))).astype(BF16)


def _heads_split(w, n_heads, first, second):
    k = w.shape[0]
    w3 = w.reshape(k, n_heads, first + second)
    return jnp.concatenate([w3[:, :, :first].reshape(k, n_heads * first),
                            w3[:, :, first:].reshape(k, n_heads * second)], axis=1).astype(BF16)


def _head_slots(w, n_heads, lo, hi):
    k = w.shape[0]
    w3 = w.reshape(k, n_heads, -1)[:, :, lo:hi]
    return jnp.pad(w3, ((0, 0), (0, 0), (0, LANES - (hi - lo)))).reshape(k, n_heads * LANES)


def _mla_weights(w_uq, w_ukv):
    k = w_ukv.shape[0]
    v_cols = w_ukv.reshape(k, MLA_HEADS, MLA_NOPE + MLA_V)[:, :, MLA_NOPE:].reshape(k, VM_W)
    return (_head_slots(w_uq, MLA_HEADS, 0, MLA_NOPE + MLA_ROPE).astype(BF16),
            jnp.concatenate([_head_slots(w_ukv, MLA_HEADS, 0, MLA_NOPE), v_cols], axis=1).astype(BF16))


def _slot_tables(cos, sin):
    n = cos.shape[0]
    pad = LANES - KR_OFF - MLA_ROPE
    cos = jnp.concatenate([jnp.ones((n, KR_OFF), F32), cos, jnp.ones((n, pad), F32)], axis=1)
    sin = jnp.concatenate([jnp.zeros((n, KR_OFF), F32), sin, jnp.zeros((n, pad), F32)], axis=1)
    return cos, sin


def _rope_tables(width, n_rep):
    half = width // 2
    t = jnp.arange(DEC_SEQ)
    rows = (t // GRID_W).astype(F32)
    cols = (t % GRID_W).astype(F32)
    inv_freq = ROPE_BASE ** (-jnp.arange(0, half, 2, dtype=F32) / half)
    ar = rows[:, None] * inv_freq
    ac = cols[:, None] * inv_freq
    cos = jnp.concatenate([jnp.cos(ar), jnp.cos(ar), jnp.cos(ac), jnp.cos(ac)], axis=1)
    sin = jnp.concatenate([-jnp.sin(ar), jnp.sin(ar), -jnp.sin(ac), jnp.sin(ac)], axis=1)
    return jnp.tile(cos, (1, n_rep)), jnp.tile(sin, (1, n_rep))


def kernel(x_prompt, x_sample, c, cache_mla_ckv, cache_mla_krope, cache_gqa_k, cache_gqa_v, state_hgrn,
           c_ctx, norm1, norm2, w_mod, b_mod, w_attn_in, mla_q_norm, w_mla_uq, mla_kv_norm, w_mla_ukv,
           gqa_sink, w_attn_out, w_rec_in, rec_lb_logits, rec_out_norm, w_rec_out, w_router, w_gate,
           w_up, w_down, final_norm):
    d = D_MODEL
    xc = x_prompt.reshape(BATCH * SEQ, d)
    xl = x_sample.reshape(DEC_BATCH * DEC_SEQ, d)
    cvecs = jnp.concatenate([c_ctx[None, :], c, jnp.zeros((N_MOD_ROWS - N_MOD_GROUPS, d), F32)], axis=0)
    mods_all = _mod_vectors(cvecs, w_mod, b_mod).reshape(DEPTH, N_MOD_ROWS, 6, d)[:, :N_MOD_GROUPS]
    grp_c = _group_map(None, 0, 0)
    cm, sm = _slot_tables(*_rope_tables(MLA_ROPE, 1))
    cg, sg = _rope_tables(GQA_HD, LANES // GQA_HD)
    kvw = GQA_KV_HEADS * GQA_HD

    caches, st_all = None, None
    for layer in range(DEPTH):
        mods = mods_all[layer]
        g1 = norm1[layer].reshape(1, d)
        g2 = norm2[layer].reshape(1, d)
        if layer % 2 == 0:
            a = layer // 2
            w_in = _attn_in_weight(w_attn_in[a])
            w_uq, w_ukv = _mla_weights(w_mla_uq[a], w_mla_ukv[a])
            w_out = w_attn_out[a].astype(BF16)
            qn_g = mla_q_norm[a].reshape(1, -1)
            kvn_g = mla_kv_norm[a].reshape(1, -1)
            sink = gqa_sink[a]
            zc = _modproj(xc, g1, mods, w_in, sub=0, bm=512, grp=grp_c)
            zl = _modproj(xl, g1, mods, w_in, sub=0, bm=512, grp=_group_map(DEC_SEQ, 512, 1))
            oc, *caches = _attn_ctx(zc, sink, qn_g, _heads_split(w_mla_uq[a], MLA_HEADS, MLA_NOPE, MLA_ROPE), kvn_g,
                                    _heads_split(w_mla_ukv[a], MLA_HEADS, MLA_NOPE, MLA_V), a, caches)
            qcat, kcat, vm, qg, kg, vg = _lat_prep(zl, qn_g, w_uq, kvn_g, w_ukv, cm, sm, cg, sg, bm=256)
            kr_c = jnp.pad(cache_mla_krope[:, a], ((0, 0), (0, 0), (KR_OFF, LANES - KR_OFF - MLA_ROPE)))
            kcat_c, vm_c = _cache_kv(cache_mla_ckv, kr_c, a, w_ukv)
            per_b = lambda x: x.reshape(DEC_BATCH, DEC_SEQ, -1)
            cat = lambda lat, ctx: jnp.concatenate([per_b(lat), ctx], axis=1)
            band = lambda x: jnp.pad(per_b(x), ((0, 0), (BAND_BLOCK, BAND_BLOCK), (0, 0)))
            ol = _lat_attn(sink, qcat, cat(kcat, kcat_c), cat(vm, vm_c), qg, band(kg), band(vg),
                           cache_gqa_k[:, a].reshape(DEC_BATCH, PAST_LEN, kvw).astype(BF16),
                           cache_gqa_v[:, a].reshape(DEC_BATCH, PAST_LEN, kvw).astype(BF16), qb=LAT_QB)
        else:
            r = layer // 2
            w_in = w_rec_in[r].astype(BF16)
            w_out = w_rec_out[r].astype(BF16)
            og = rec_out_norm[r].reshape(1, -1)
            zc = _modproj(xc, g1, mods, w_in, sub=0, bm=512, grp=grp_c)
            zl = _modproj(xl, g1, mods, w_in, sub=0, bm=512, grp=_group_map(DEC_SEQ, 512, 1))
            oc, st_all = _scan(zc, rec_lb_logits, og, r, t_len=SEQ, n_heads=8, emit_state=True, state_out=st_all)
            (ol,) = _scan(zl, rec_lb_logits, og, r, t_len=DEC_SEQ, n_heads=4, s0=state_hgrn)
        wr_t = w_router[layer].T.astype(BF16)
        xc, hc, aff_c = _outproj_router(oc, w_out, xc, mods, g2, wr_t, bm=1024, grp=grp_c)
        xl, hl, aff_l = _outproj_router(ol, w_out, xl, mods, g2, wr_t, bm=1024, grp=_group_map(DEC_SEQ, 1024, 1))
        gxc, gate_c, pos_c = _select(aff_c, hc, t_len=SEQ)
        gxl, gate_l, pos_l = _select(aff_l, hl, t_len=DEC_SEQ)
        yc, yl = _ffn(gxc, gxl, gate_c, gate_l, w_gate, w_up, w_down, layer, bf=512)
        fn = final_norm.reshape(1, d) if layer == DEPTH - 1 else None
        xc = _combine(pos_c, yc, xc, mods, t_len=SEQ, tt=SEQ, base=lambda s: 0, sets=COMBINE_SETS, final_gain=fn)
        xl = _combine(pos_l, yl, xl, mods, t_len=DEC_SEQ, tt=512, base=lambda s: 1 + s, final_gain=fn)

    y_prompt = xc.reshape(BATCH, SEQ, d)
    y_sample = xl.reshape(DEC_BATCH, DEC_SEQ, d)
    ckv_all, kr_all, kg_all, vg_all = caches
    heads = lambda x: x.reshape(BATCH, N_ATTN_LAYERS, SEQ, GQA_KV_HEADS, GQA_HD)
    return (y_prompt, y_sample, ckv_all, kr_all, heads(kg_all), heads(vg_all), st_all)
```
